```python
import jax, jax.numpy as jnp
from jax import lax
import numpy as np

D_MODEL = 2048
BATCH = 16
SEQ = 256
DEPTH = 2
DEC_BATCH = 2
DEC_SEQ = 4096
PAST_LEN = 512

GRID_W = 64
HEAD_DIM = 128
H_GQA = 8
KV_GQA = 2
H_NAT = 8
H_WIN = 8
KV_WIN = 2
BRANCH_W = 1024
N_BRANCH = 3
D_FF = 5632
Q_BLOCK = 128
WIN_RADIUS = 128
NAT_ROWS = 8
NAT_COLS = 16
ROPE_THETA = 10000.0
EPS = 1e-6
NEG_INF = -1e30
SCALE = HEAD_DIM ** -0.5
IN_SIZES = (H_GQA * HEAD_DIM, KV_GQA * HEAD_DIM, KV_GQA * HEAD_DIM,
            H_NAT * HEAD_DIM, H_NAT * HEAD_DIM, H_NAT * HEAD_DIM,
            H_WIN * HEAD_DIM, KV_WIN * HEAD_DIM, KV_WIN * HEAD_DIM,
            N_BRANCH * D_MODEL)
D_IN = sum(IN_SIZES)

kernel_name = "hybrid_diffusion_prefix_trunk_step"


def rmsnorm(x, g):
    xf = x.astype(jnp.float32)
    y = xf * lax.rsqrt(jnp.mean(xf * xf, axis=-1, keepdims=True) + EPS)
    return (y * g.astype(jnp.float32)).astype(x.dtype)


def modulation(cond, w_mod, b_mod):
    m = (jax.nn.silu(cond) @ w_mod + b_mod)[..., None, :]
    return jnp.split(m, 9, axis=-1)


def adaln(x, g, shift, scale):
    return rmsnorm(x, g) * (1 + scale) + shift


def swiglu(h, w_gu, w_down):
    a, b = jnp.split(h @ w_gu, 2, axis=-1)
    return (jax.nn.silu(a) * b) @ w_down


def project_in(h, w_in):
    z = h @ w_in
    idx = [int(i) for i in np.cumsum(IN_SIZES)[:-1]]
    parts = jnp.split(z, idx, axis=-1)
    B, T = h.shape[:2]
    heads = [p.reshape(B, T, -1, HEAD_DIM) for p in parts[:9]]
    return heads, parts[9]


def rope_2d(x):
    N, hd = x.shape[1], x.shape[-1]
    half = hd // 2
    nf = half // 2
    t = jnp.arange(N)
    row = (t // GRID_W).astype(jnp.float32)
    col = (t % GRID_W).astype(jnp.float32)
    inv = 1.0 / (ROPE_THETA ** (jnp.arange(nf, dtype=jnp.float32) / nf))

    def rot(seg, pos):
        ang = pos[:, None] * inv[None, :]
        cos = jnp.cos(ang)[None, :, None, :]
        sin = jnp.sin(ang)[None, :, None, :]
        s1, s2 = seg[..., :nf], seg[..., nf:]
        return jnp.concatenate([s1 * cos - s2 * sin, s2 * cos + s1 * sin], axis=-1)

    xf = x.astype(jnp.float32)
    return jnp.concatenate([rot(xf[..., :half], row), rot(xf[..., half:], col)], axis=-1).astype(x.dtype)


def joint_softmax(parts, sink=None):
    s = [p.astype(jnp.float32) for p in parts]
    if sink is not None:
        s.append(jnp.broadcast_to(sink.astype(jnp.float32), s[0].shape[:-1] + (1,)))
    p = jax.nn.softmax(jnp.concatenate(s, axis=-1), axis=-1)
    out, start = [], 0
    for q in parts:
        n = q.shape[-1]
        out.append(p[..., start:start + n])
        start += n
    return out


def dense_attn(q, k, v, sink=None):
    B, T, Hq, hd = q.shape
    Hkv = k.shape[2]
    G = Hq // Hkv
    qb = q.reshape(B, T // Q_BLOCK, Q_BLOCK, Hkv, G, hd).transpose(1, 0, 2, 3, 4, 5)

    def block(qblk):
        s = jnp.einsum('bqkgd,bskd->bkgqs', qblk, k).astype(jnp.float32) * SCALE
        (p,) = joint_softmax([s], sink)
        return jnp.einsum('bkgqs,bskd->bqkgd', p.astype(v.dtype), v)

    o = lax.map(block, qb)
    return o.transpose(1, 0, 2, 3, 4, 5).reshape(B, T, Hq * hd)


def window_attn(q, k, v, kc, vc, sink):
    B, N, Hq, hd = q.shape
    Hkv = k.shape[2]
    G = Hq // Hkv
    nb = N // Q_BLOCK

    def band(x):
        xp = jnp.pad(x, ((0, 0), (Q_BLOCK, Q_BLOCK), (0, 0), (0, 0))).reshape(B, nb + 2, Q_BLOCK, Hkv, hd)
        return jnp.concatenate([xp[:, :-2], xp[:, 1:-1], xp[:, 2:]], axis=2)

    kb, vb = band(k), band(v)
    qb = q.reshape(B, nb, Q_BLOCK, Hkv, G, hd)
    blk = jnp.arange(nb)[:, None] * Q_BLOCK
    qpos = blk + jnp.arange(Q_BLOCK)[None, :]
    kpos = blk - Q_BLOCK + jnp.arange(3 * Q_BLOCK)[None, :]
    mask = ((jnp.abs(kpos[:, None, :] - qpos[:, :, None]) <= WIN_RADIUS)
            & ((kpos >= 0) & (kpos < N))[:, None, :])
    s_loc = jnp.einsum('bnqkgd,bnskd->bnkgqs', qb, kb).astype(jnp.float32) * SCALE
    s_loc = jnp.where(mask[None, :, None, None], s_loc, NEG_INF)
    s_ctx = jnp.einsum('bnqkgd,bskd->bnkgqs', qb, kc).astype(jnp.float32) * SCALE
    p_ctx, p_loc = joint_softmax([s_ctx, s_loc], sink)
    o = (jnp.einsum('bnkgqs,bskd->bnqkgd', p_ctx.astype(vc.dtype), vc)
         + jnp.einsum('bnkgqs,bnskd->bnqkgd', p_loc.astype(vb.dtype), vb))
    return o.reshape(B, N, Hq * hd)


def nat_attn(q, k, v, kc, vc, rpb):
    B, N, H, hd = q.shape
    rows = N // GRID_W
    wr = min(NAT_ROWS, rows)
    wc = NAT_COLS
    r = jnp.arange(rows)
    rs = jnp.clip(r - wr // 2, 0, rows - wr)
    row_idx = rs[:, None] + jnp.arange(wr)[None, :]
    c = jnp.arange(GRID_W)
    cs = jnp.clip(c - wc // 2, 0, GRID_W - wc)
    col_ok = (c[None, :] >= cs[:, None]) & (c[None, :] < cs[:, None] + wc)
    dr = row_idx - r[:, None] + NAT_ROWS - 1
    dc = jnp.clip(c[None, :] - c[:, None] + wc - 1, 0, 2 * wc - 2)
    bias = rpb[:, dr[:, :, None, None], dc[None, None, :, :]]
    bias = bias.astype(jnp.float32).transpose(1, 0, 3, 2, 4).reshape(rows, H, GRID_W, wr * GRID_W)
    ok = jnp.broadcast_to(col_ok[:, None, :], (GRID_W, wr, GRID_W)).reshape(GRID_W, wr * GRID_W)
    bias = jnp.where(ok[None, None], bias, NEG_INF)
    kg = k.reshape(B, rows, GRID_W, H, hd)[:, row_idx].reshape(B, rows, wr * GRID_W, H, hd)
    vg = v.reshape(B, rows, GRID_W, H, hd)[:, row_idx].reshape(B, rows, wr * GRID_W, H, hd)
    qg = q.reshape(B, rows, GRID_W, H, hd)
    s_loc = jnp.einsum('brqhd,brkhd->brhqk', qg, kg).astype(jnp.float32) * SCALE + bias[None]
    s_ctx = jnp.einsum('brqhd,bshd->brhqs', qg, kc).astype(jnp.float32) * SCALE
    p_ctx, p_loc = joint_softmax([s_ctx, s_loc])
    o = (jnp.einsum('brhqs,bshd->brqhd', p_ctx.astype(vc.dtype), vc)
         + jnp.einsum('brhqk,brkhd->brqhd', p_loc.astype(vg.dtype), vg))
    return o.reshape(B, N, H * hd)


def merge(gates, outs, w_branch, w_o):
    gs = jnp.split(gates, N_BRANCH, axis=-1)
    m = sum(jax.nn.sigmoid(g) * (o @ w_branch[i]) for i, (g, o) in enumerate(zip(gs, outs)))
    return m @ w_o


def mixer_context(h, w_in, g_q, g_k, sink, w_branch, w_o):
    (qa, ka, va, qb, kb, vb, qc, kc, vc), gates = project_in(h, w_in)
    qa = rmsnorm(qa, g_q)
    ka = rmsnorm(ka, g_k)
    sink_r = sink.reshape(KV_WIN, H_WIN // KV_WIN, 1, 1)
    oa = dense_attn(qa, ka, va)
    ob = dense_attn(qb, kb, vb)
    oc = dense_attn(qc, kc, vc, sink_r)
    return merge(gates, (oa, ob, oc), w_branch, w_o), (ka, va, kb, vb, kc, vc)


def mixer_latent(h, ctx, w_in, g_q, g_k, rpb, sink, w_branch, w_o):
    ka_c, va_c, kb_c, vb_c, kc_c, vc_c = ctx
    (qa, ka, va, qb, kb, vb, qc, kc, vc), gates = project_in(h, w_in)
    qa = rope_2d(rmsnorm(qa, g_q))
    ka = rope_2d(rmsnorm(ka, g_k))
    sink_r = sink.reshape(KV_WIN, H_WIN // KV_WIN, 1, 1)
    oa = dense_attn(qa, jnp.concatenate([ka_c.astype(ka.dtype), ka], axis=1),
                    jnp.concatenate([va_c.astype(va.dtype), va], axis=1))
    ob = nat_attn(qb, kb, vb, kb_c, vb_c, rpb)
    oc = window_attn(rope_2d(qc), rope_2d(kc), vc, kc_c, vc_c, sink_r)
    return merge(gates, (oa, ob, oc), w_branch, w_o), None


def trunk_layer(x, mods, g_norm, w1_gu, w1_down, w2_gu, w2_down, mix):
    sh1, sc1, gt1, sh2, sc2, gt2, sh3, sc3, gt3 = mods
    x = x + 0.5 * gt1 * swiglu(adaln(x, g_norm[0], sh1, sc1), w1_gu, w1_down)
    m, aux = mix(adaln(x, g_norm[1], sh2, sc2))
    x = x + gt2 * m
    x = x + 0.5 * gt3 * swiglu(adaln(x, g_norm[2], sh3, sc3), w2_gu, w2_down)
    return x, aux


def setup_inputs(seed: int = 0) -> dict:
    key = jax.random.key(seed)
    ks = jax.random.split(key, 32)
    f32 = jnp.float32

    def nrm(k, shape, s):
        return jax.random.normal(k, shape, f32) * s

    hd = HEAD_DIM
    return {
        'x_prompt': nrm(ks[0], (BATCH, SEQ, D_MODEL), 1.0),
        'x_sample': nrm(ks[1], (DEC_BATCH, DEC_SEQ, D_MODEL), 1.0),
        'cache_k_gqa': nrm(ks[2], (DEC_BATCH, DEPTH, PAST_LEN, KV_GQA, hd), 1.0),
        'cache_v_gqa': nrm(ks[3], (DEC_BATCH, DEPTH, PAST_LEN, KV_GQA, hd), 1.0),
        'cache_k_nat': nrm(ks[4], (DEC_BATCH, DEPTH, PAST_LEN, H_NAT, hd), 1.0),
        'cache_v_nat': nrm(ks[5], (DEC_BATCH, DEPTH, PAST_LEN, H_NAT, hd), 1.0),
        'cache_k_win': nrm(ks[6], (DEC_BATCH, DEPTH, PAST_LEN, KV_WIN, hd), 1.0),
        'cache_v_win': nrm(ks[7], (DEC_BATCH, DEPTH, PAST_LEN, KV_WIN, hd), 1.0),
        'c': nrm(ks[8], (DEC_BATCH, D_MODEL), 1.0),
        'c_ctx': nrm(ks[9], (D_MODEL,), 1.0),
        'w_mod': nrm(ks[10], (DEPTH, D_MODEL, 9 * D_MODEL), 0.5 * D_MODEL ** -0.5),
        'b_mod': nrm(ks[11], (DEPTH, 9 * D_MODEL), 0.02),
        'g_norm': 1.0 + nrm(ks[12], (DEPTH, 3, D_MODEL), 0.02),
        'w_ffn1_gu': nrm(ks[13], (DEPTH, D_MODEL, 2 * D_FF), D_MODEL ** -0.5),
        'w_ffn1_down': nrm(ks[14], (DEPTH, D_FF, D_MODEL), D_FF ** -0.5),
        'w_in': nrm(ks[15], (DEPTH, D_MODEL, D_IN), D_MODEL ** -0.5),
        'g_q': 1.0 + nrm(ks[16], (DEPTH, hd), 0.02),
        'g_k': 1.0 + nrm(ks[17], (DEPTH, hd), 0.02),
        'rpb': nrm(ks[18], (DEPTH, H_NAT, 2 * NAT_ROWS - 1, 2 * NAT_COLS - 1), 0.1),
        'sink': nrm(ks[19], (DEPTH, H_WIN), 0.5),
        'w_branch': nrm(ks[20], (DEPTH, N_BRANCH, BRANCH_W, D_MODEL), BRANCH_W ** -0.5),
        'w_o': nrm(ks[21], (DEPTH, D_MODEL, D_MODEL), D_MODEL ** -0.5),
        'w_ffn2_gu': nrm(ks[22], (DEPTH, D_MODEL, 2 * D_FF), D_MODEL ** -0.5),
        'w_ffn2_down': nrm(ks[23], (DEPTH, D_FF, D_MODEL), D_FF ** -0.5),
        'g_final': 1.0 + nrm(ks[24], (D_MODEL,), 0.02),
    }


def reference(x_prompt, x_sample, cache_k_gqa, cache_v_gqa, cache_k_nat, cache_v_nat, cache_k_win, cache_v_win,
              c, c_ctx, w_mod, b_mod, g_norm, w_ffn1_gu, w_ffn1_down, w_in, g_q, g_k, rpb, sink,
              w_branch, w_o, w_ffn2_gu, w_ffn2_down, g_final):
    y_p = x_prompt
    y_s = x_sample
    new = [[] for _ in range(6)]
    for l in range(DEPTH):
        mods_ctx = modulation(c_ctx, w_mod[l], b_mod[l])
        y_p, ctx_kv = trunk_layer(
            y_p, mods_ctx, g_norm[l], w_ffn1_gu[l], w_ffn1_down[l], w_ffn2_gu[l], w_ffn2_down[l],
            lambda h: mixer_context(h, w_in[l], g_q[l], g_k[l], sink[l], w_branch[l], w_o[l]))
        for i in range(6):
            new[i].append(ctx_kv[i])
        mods_lat = modulation(c, w_mod[l], b_mod[l])
        cache_l = (cache_k_gqa[:, l], cache_v_gqa[:, l], cache_k_nat[:, l], cache_v_nat[:, l],
                   cache_k_win[:, l], cache_v_win[:, l])
        y_s, _ = trunk_layer(
            y_s, mods_lat, g_norm[l], w_ffn1_gu[l], w_ffn1_down[l], w_ffn2_gu[l], w_ffn2_down[l],
            lambda h: mixer_latent(h, cache_l, w_in[l], g_q[l], g_k[l], rpb[l], sink[l], w_branch[l], w_o[l]))
    y_prompt = rmsnorm(y_p, g_final)
    y_sample = rmsnorm(y_s, g_final)
    new_k_gqa = jnp.stack(new[0], axis=1)
    new_v_gqa = jnp.stack(new[1], axis=1)
    new_k_nat = jnp.stack(new[2], axis=1)
    new_v_nat = jnp.stack(new[3], axis=1)
    new_k_win = jnp.stack(new[4], axis=1)
    new_v_win = jnp.stack(new[5], axis=1)
    return (y_prompt, y_sample, new_k_gqa, new_v_gqa, new_k_nat, new_v_nat, new_k_win, new_v_win)
```

```python
import functools
import math

import numpy as np
import jax
import jax.numpy as jnp
from jax import lax
from jax.experimental import pallas as pl
from jax.experimental.pallas import tpu as pltpu

F32 = jnp.float32
BF16 = jnp.bfloat16

HEAD_DIM = 128
N_HEADS = 8
KV_GQA = 2
KV_WIN = 2
GRID_W = 64
NAT_ROWS = 8
NAT_COLS = 16
WIN_RADIUS = 128
Q_BLOCK = 128
ROPE_THETA = 10000.0
EPS = 1e-6
NEG_INF = -1e30
LOG2E = math.log2(math.e)
Q_SCALE = (HEAD_DIM ** -0.5) * LOG2E

OFF_QA, OFF_KA, OFF_VA = 0, 1024, 1280
OFF_QB, OFF_KB, OFF_VB = 1536, 2560, 3584
OFF_QC, OFF_KC, OFF_VC = 4608, 5632, 5888
OFF_GATES = 6144
QKV_W = 6144
COL_TILE = 512
KVN_W = 3072

V7X_VMEM_BYTES = 64 * 1024 * 1024
VMEM_CAP_BYTES = 60 * 1024 * 1024
NAT_QROWS = 4
NAT_KROWS = NAT_ROWS + NAT_QROWS


def _vmem_limit(nbytes):
    return int(min(VMEM_CAP_BYTES, max(32 * 1024 * 1024, nbytes * 3 // 2)))


def _params(sem, nbytes):
    return pltpu.CompilerParams(dimension_semantics=sem, vmem_limit_bytes=_vmem_limit(nbytes))


def _divisor_tile(n, cap, mult):
    if n <= cap:
        return n
    t = (cap // mult) * mult
    while t > mult and n % t:
        t -= mult
    assert n % t == 0, (n, cap, mult)
    return t


def _adaln(x, g, shift, scale):
    ms = jnp.mean(x * x, axis=-1, keepdims=True)
    y = x * lax.rsqrt(ms + EPS)
    return (y * g) * (1.0 + scale) + shift


def _headnorm(y, g):
    ms = jnp.mean(y * y, axis=-1, keepdims=True)
    return (y * lax.rsqrt(ms + EPS)) * g


def _silu(a):
    return a * jax.nn.sigmoid(a)


def _qk_t(q, k):
    return lax.dot_general(q, k, (((1,), (1,)), ((), ())), preferred_element_type=F32)


def _mm(a, b):
    return jnp.dot(a, b, preferred_element_type=F32)


def _mod_kernel(c_ref, w_ref, b_ref, o_ref):
    a = _silu(c_ref[...]).astype(BF16)
    o_ref[...] = _mm(a, w_ref[...].astype(BF16)) + b_ref[...]


def _modulation(cond8, w_mod, b_mod):
    depth, d, n = w_mod.shape
    tn = _divisor_tile(n, 1024, 128)
    nbytes = 2 * d * tn * 4 + d * tn * 2 + 4 * 8 * tn * 4 + 8 * d * 4
    return pl.pallas_call(
        _mod_kernel,
        grid=(depth, n // tn),
        in_specs=[
            pl.BlockSpec((8, d), lambda l, j: (0, 0)),
            pl.BlockSpec((None, d, tn), lambda l, j: (l, 0, j)),
            pl.BlockSpec((None, 1, tn), lambda l, j: (l, 0, j)),
        ],
        out_specs=pl.BlockSpec((None, 8, tn), lambda l, j: (l, 0, j)),
        out_shape=jax.ShapeDtypeStruct((depth, 8, n), F32),
        compiler_params=_params(("parallel", "parallel"), nbytes),
        name="modulation",
    )(cond8, w_mod, b_mod.reshape(depth, 1, n))


def _gu_kernel(x_ref, g_ref, sh_ref, sc_ref, wg_ref, wu_ref, t_ref, h_ref, *, row_fn, norm_idx):
    @pl.when(pl.program_id(1) == 0)
    def _():
        r = row_fn(pl.program_id(0))
        h = _adaln(x_ref[...], g_ref[norm_idx:norm_idx + 1, :],
                   sh_ref[pl.ds(r, 1), :], sc_ref[pl.ds(r, 1), :])
        h_ref[...] = h.astype(BF16)

    h = h_ref[...]
    a = _mm(h, wg_ref[...].astype(BF16))
    b = _mm(h, wu_ref[...].astype(BF16))
    t_ref[...] = (_silu(a) * b).astype(BF16)


def _ffn_gu(x, mods, g_norm, w_gu, *, l, norm_idx, k_shift, k_scale, tm, row_fn):
    m, d = x.shape
    d_ff = w_gu.shape[2] // 2
    tf = _divisor_tile(d_ff, 512, 128)
    nj = d_ff // tf
    nbytes = (2 * tm * d * 4 + tm * d * 2 + 4 * d * tf * 4 + 2 * d * tf * 2
              + 2 * tm * tf * 4 + 2 * tm * tf * 2)
    kern = functools.partial(_gu_kernel, row_fn=row_fn, norm_idx=norm_idx)
    return pl.pallas_call(
        kern,
        grid=(m // tm, nj),
        in_specs=[
            pl.BlockSpec((tm, d), lambda i, j: (i, 0)),
            pl.BlockSpec((None, 3, d), lambda i, j: (l, 0, 0)),
            pl.BlockSpec((None, 8, d), lambda i, j: (l, 0, k_shift)),
            pl.BlockSpec((None, 8, d), lambda i, j: (l, 0, k_scale)),
            pl.BlockSpec((None, d, tf), lambda i, j: (l, 0, j)),
            pl.BlockSpec((None, d, tf), lambda i, j: (l, 0, nj + j)),
        ],
        out_specs=pl.BlockSpec((tm, tf), lambda i, j: (i, j)),
        out_shape=jax.ShapeDtypeStruct((m, d_ff), BF16),
        scratch_shapes=[pltpu.VMEM((tm, d), BF16)],
        compiler_params=_params(("parallel", "arbitrary"), nbytes),
        name="ffn_gate_up",
    )(x, g_norm, mods, mods, w_gu, w_gu)


def _down_kernel(t_ref, w_ref, x_ref, gt_ref, o_ref, *, row_fn):
    r = row_fn(pl.program_id(0))
    y = _mm(t_ref[...], w_ref[...].astype(BF16))
    o_ref[...] = x_ref[...] + (0.5 * gt_ref[pl.ds(r, 1), :]) * y


def _ffn_down(t, w_down, x, mods, *, l, k_gate, tm, row_fn):
    m, d_ff = t.shape
    d = x.shape[1]
    tn = _divisor_tile(d, 256, 128)
    nbytes = 2 * tm * d_ff * 2 + 2 * d_ff * tn * 4 + d_ff * tn * 2 + 5 * tm * tn * 4
    kern = functools.partial(_down_kernel, row_fn=row_fn)
    return pl.pallas_call(
        kern,
        grid=(m // tm, d // tn),
        in_specs=[
            pl.BlockSpec((tm, d_ff), lambda i, j: (i, 0)),
            pl.BlockSpec((None, d_ff, tn), lambda i, j: (l, 0, j)),
            pl.BlockSpec((tm, tn), lambda i, j: (i, j)),
            pl.BlockSpec((None, 8, tn), lambda i, j: (l, 0, k_gate * (d // tn) + j)),
        ],
        out_specs=pl.BlockSpec((tm, tn), lambda i, j: (i, j)),
        out_shape=jax.ShapeDtypeStruct((m, d), F32),
        compiler_params=_params(("parallel", "parallel"), nbytes),
        name="ffn_down",
    )(t, w_down, x, mods)


def _kvn_block(j):
    return jnp.where(j <= 2, 0, jnp.where(j >= 9, 5, jnp.clip(j - 4, 1, 4)))


def _inproj_kernel(*refs, is_ctx, row_fn, norm_idx):
    if is_ctx:
        (x_ref, g_ref, sh_ref, sc_ref, gq_ref, gk_ref, w_ref, zq_ref, h_ref, kvn_ref) = refs
    else:
        (x_ref, g_ref, sh_ref, sc_ref, gq_ref, gk_ref, cos_ref, sa_ref, sb_ref, w_ref,
         zq_ref, h_ref) = refs
    j = pl.program_id(1)

    @pl.when(j == 0)
    def _():
        r = row_fn(pl.program_id(0))
        h = _adaln(x_ref[...], g_ref[norm_idx:norm_idx + 1, :],
                   sh_ref[pl.ds(r, 1), :], sc_ref[pl.ds(r, 1), :])
        h_ref[...] = h.astype(BF16)

    acc = _mm(h_ref[...], w_ref[...].astype(BF16))

    def rope(y):
        if is_ctx:
            return y
        return (y * cos_ref[...] + pltpu.roll(y, 96, 1) * sa_ref[...]
                + pltpu.roll(y, 32, 1) * sb_ref[...])

    def head(hh):
        return acc[:, hh * HEAD_DIM:(hh + 1) * HEAD_DIM]

    def cols(hh):
        return slice(hh * HEAD_DIM, (hh + 1) * HEAD_DIM)

    @pl.when(j < 2)
    def _():
        for hh in range(4):
            y = _headnorm(head(hh), gq_ref[...])
            zq_ref[:, cols(hh)] = (rope(y) * Q_SCALE).astype(BF16)

    @pl.when(j == 2)
    def _():
        for hh in range(2):
            y = _headnorm(head(hh), gk_ref[...])
            if is_ctx:
                kvn_ref[:, cols(hh)] = y
            zq_ref[:, cols(hh)] = rope(y).astype(BF16)
        va = acc[:, 256:512]
        if is_ctx:
            kvn_ref[:, 256:512] = va
        zq_ref[:, 256:512] = va.astype(BF16)

    @pl.when((j == 3) | (j == 4))
    def _():
        zq_ref[...] = (acc * Q_SCALE).astype(BF16)

    @pl.when((j >= 5) & (j <= 8))
    def _():
        if is_ctx:
            kvn_ref[...] = acc
        zq_ref[...] = acc.astype(BF16)

    @pl.when((j == 9) | (j == 10))
    def _():
        for hh in range(4):
            zq_ref[:, cols(hh)] = (rope(head(hh)) * Q_SCALE).astype(BF16)

    @pl.when(j == 11)
    def _():
        for hh in range(2):
            y = head(hh)
            if is_ctx:
                kvn_ref[:, cols(hh)] = y
            zq_ref[:, cols(hh)] = rope(y).astype(BF16)
        vc = acc[:, 256:512]
        if is_ctx:
            kvn_ref[:, 256:512] = vc
        zq_ref[:, 256:512] = vc.astype(BF16)


def _inproj(x, mods, g_norm, g_q, g_k, w_in, rope_tabs, *, l, tm, row_fn, is_ctx):
    m, d = x.shape
    depth = w_in.shape[0]
    nj = QKV_W // COL_TILE
    in_specs = [
        pl.BlockSpec((tm, d), lambda i, j: (i, 0)),
        pl.BlockSpec((None, 3, d), lambda i, j: (l, 0, 0)),
        pl.BlockSpec((None, 8, d), lambda i, j: (l, 0, 3)),
        pl.BlockSpec((None, 8, d), lambda i, j: (l, 0, 4)),
        pl.BlockSpec((None, 1, HEAD_DIM), lambda i, j: (l, 0, 0)),
        pl.BlockSpec((None, 1, HEAD_DIM), lambda i, j: (l, 0, 0)),
    ]
    args = [x, g_norm, mods, mods, g_q.reshape(depth, 1, HEAD_DIM), g_k.reshape(depth, 1, HEAD_DIM)]
    if not is_ctx:
        n_pos = rope_tabs[0].shape[0]
        per = n_pos // tm
        for tab in rope_tabs:
            in_specs.append(pl.BlockSpec((tm, HEAD_DIM), lambda i, j: (i % per, 0)))
            args.append(tab)
    in_specs.append(pl.BlockSpec((None, d, COL_TILE), lambda i, j: (l, 0, j)))
    args.append(w_in)
    out_specs = [
        pl.BlockSpec((tm, COL_TILE), lambda i, j: (i, j)),
        pl.BlockSpec((tm, d), lambda i, j: (i, 0)),
    ]
    out_shape = [jax.ShapeDtypeStruct((m, QKV_W), BF16), jax.ShapeDtypeStruct((m, d), BF16)]
    if is_ctx:
        out_specs.append(pl.BlockSpec((tm, COL_TILE), lambda i, j: (i, _kvn_block(j))))
        out_shape.append(jax.ShapeDtypeStruct((m, KVN_W), F32))
    nbytes = (2 * tm * d * 4 + 2 * tm * d * 2 + 2 * d * COL_TILE * 4 + d * COL_TILE * 2
              + 3 * tm * COL_TILE * 4 + 2 * tm * COL_TILE * 2 + 6 * tm * HEAD_DIM * 4)
    kern = functools.partial(_inproj_kernel, is_ctx=is_ctx, row_fn=row_fn, norm_idx=1)
    return pl.pallas_call(
        kern,
        grid=(m // tm, nj),
        in_specs=in_specs,
        out_specs=out_specs,
        out_shape=out_shape,
        compiler_params=_params(("parallel", "arbitrary"), nbytes),
        name="inproj_ctx" if is_ctx else "inproj_lat",
    )(*args)


def _ctx_attn_kernel(z_ref, sink_ref, o_ref, *, l, seq):
    def attend(q_off, n_q, k_off, v_off, sinks):
        q = jnp.concatenate(
            [z_ref[:, q_off + h * HEAD_DIM:q_off + (h + 1) * HEAD_DIM] for h in range(n_q)], axis=0)
        k = z_ref[:, k_off:k_off + HEAD_DIM]
        v = z_ref[:, v_off:v_off + HEAD_DIM]
        s = _qk_t(q, k)
        m = jnp.max(s, axis=-1, keepdims=True)
        if sinks is not None:
            sink_col = jnp.concatenate(
                [jnp.full((seq, 1), sk, F32) for sk in sinks], axis=0)
            m = jnp.maximum(m, sink_col)
        p = jnp.exp2(s - m)
        den = jnp.sum(p, axis=-1, keepdims=True)
        if sinks is not None:
            den = den + jnp.exp2(sink_col - m)
        return _mm(p.astype(BF16), v) / den

    def store(o, o_off, n_q):
        for h in range(n_q):
            o_ref[:, o_off + h * HEAD_DIM:o_off + (h + 1) * HEAD_DIM] = (
                o[h * seq:(h + 1) * seq].astype(BF16))

    grp = N_HEADS // KV_GQA
    for g in range(KV_GQA):
        o = attend(OFF_QA + g * grp * HEAD_DIM, grp, OFF_KA + g * HEAD_DIM,
                   OFF_VA + g * HEAD_DIM, None)
        store(o, g * grp * HEAD_DIM, grp)
    for h in range(N_HEADS):
        o = attend(OFF_QB + h * HEAD_DIM, 1, OFF_KB + h * HEAD_DIM, OFF_VB + h * HEAD_DIM, None)
        store(o, 1024 + h * HEAD_DIM, 1)
    grp = N_HEADS // KV_WIN
    for g in range(KV_WIN):
        sinks = [sink_ref[l, g * grp + h] for h in range(grp)]
        o = attend(OFF_QC + g * grp * HEAD_DIM, grp, OFF_KC + g * HEAD_DIM,
                   OFF_VC + g * HEAD_DIM, sinks)
        store(o, 2048 + g * grp * HEAD_DIM, grp)


def _ctx_attention(zq, sink2, *, l, batch, seq):
    kern = functools.partial(_ctx_attn_kernel, l=l, seq=seq)
    nbytes = 2 * seq * QKV_W * 2 + 2 * seq * 3072 * 2 + 8 * 4 * seq * seq * 4
    return pl.pallas_call(
        kern,
        grid=(batch,),
        in_specs=[
            pl.BlockSpec((seq, QKV_W), lambda b: (b, 0)),
            pl.BlockSpec(memory_space=pltpu.SMEM),
        ],
        out_specs=pl.BlockSpec((seq, 3072), lambda b: (b, 0)),
        out_shape=jax.ShapeDtypeStruct((batch * seq, 3072), BF16),
        compiler_params=_params(("parallel",), nbytes),
        name="attn_ctx",
    )(zq, sink2)


def _lat_gqa_kernel(q_ref, kl_ref, vl_ref, kc_ref, vc_ref, o_ref, k_sc, v_sc, s_sc,
                    *, past, n_lat, tq, chunk):
    grp = N_HEADS // KV_GQA
    total = past + n_lat

    @pl.when(pl.program_id(2) == 0)
    def _():
        k_sc[0:past, :] = kc_ref[...].astype(BF16)
        k_sc[past:total, :] = kl_ref[...]
        v_sc[0:past, 0:HEAD_DIM] = vc_ref[...].astype(BF16)
        v_sc[past:total, 0:HEAD_DIM] = vl_ref[...]
        v_sc[:, HEAD_DIM:2 * HEAD_DIM] = jnp.ones((total, HEAD_DIM), BF16)

    q = jnp.concatenate([q_ref[:, h * HEAD_DIM:(h + 1) * HEAD_DIM] for h in range(grp)], axis=0)
    n_chunk = total // chunk
    m = None
    for c in range(n_chunk):
        s = _qk_t(q, k_sc[c * chunk:(c + 1) * chunk, :])
        s_sc[:, c * chunk:(c + 1) * chunk] = s
        mc = jnp.max(s, axis=-1, keepdims=True)
        m = mc if m is None else jnp.maximum(m, mc)
    acc = jnp.zeros((grp * tq, 2 * HEAD_DIM), F32)
    for c in range(n_chunk):
        p = jnp.exp2(s_sc[:, c * chunk:(c + 1) * chunk] - m).astype(BF16)
        acc = acc + _mm(p, v_sc[c * chunk:(c + 1) * chunk, :])
    o = acc[:, 0:HEAD_DIM] / acc[:, HEAD_DIM:2 * HEAD_DIM]
    for h in range(grp):
        o_ref[:, h * HEAD_DIM:(h + 1) * HEAD_DIM] = o[h * tq:(h + 1) * tq].astype(BF16)


def _lat_gqa_attention(zq, cache_k, cache_v, *, l, n_b, n_lat):
    past = cache_k.shape[2]
    grp = N_HEADS // KV_GQA
    tq = _divisor_tile(n_lat, 128, 16)
    nq = n_lat // tq
    total = past + n_lat
    chunk = _divisor_tile(total, 1536, 128)
    wq = grp * HEAD_DIM
    ka_blk = OFF_KA // HEAD_DIM
    va_blk = OFF_VA // HEAD_DIM
    kern = functools.partial(_lat_gqa_kernel, past=past, n_lat=n_lat, tq=tq, chunk=chunk)
    nbytes = (4 * tq * wq * 2 + 4 * n_lat * HEAD_DIM * 2 + 4 * past * HEAD_DIM * 4
              + total * HEAD_DIM * 2 * 3 + grp * tq * total * 4 + 4 * grp * tq * chunk * 4)
    return pl.pallas_call(
        kern,
        grid=(n_b, KV_GQA, nq),
        in_specs=[
            pl.BlockSpec((tq, wq), lambda b, g, t: (b * nq + t, g)),
            pl.BlockSpec((n_lat, HEAD_DIM), lambda b, g, t: (b, ka_blk + g)),
            pl.BlockSpec((n_lat, HEAD_DIM), lambda b, g, t: (b, va_blk + g)),
            pl.BlockSpec((None, None, past, HEAD_DIM), lambda b, g, t: (b, l, 0, g)),
            pl.BlockSpec((None, None, past, HEAD_DIM), lambda b, g, t: (b, l, 0, g)),
        ],
        out_specs=pl.BlockSpec((tq, wq), lambda b, g, t: (b * nq + t, g)),
        out_shape=jax.ShapeDtypeStruct((n_b * n_lat, N_HEADS * HEAD_DIM), BF16),
        scratch_shapes=[
            pltpu.VMEM((total, HEAD_DIM), BF16),
            pltpu.VMEM((total, 2 * HEAD_DIM), BF16),
            pltpu.VMEM((grp * tq, total), F32),
        ],
        compiler_params=_params(("parallel", "parallel", "arbitrary"), nbytes),
        name="attn_lat_global",
    )(zq, zq, zq, cache_k, cache_v)


def _nat_bias_tables(rpb_l, rows):
    w, wr, wc = GRID_W, NAT_ROWS, NAT_COLS
    c = np.arange(w)
    cs = np.clip(c - wc // 2, 0, w - wc)
    col_ok = (c[None, :] >= cs[:, None]) & (c[None, :] < cs[:, None] + wc)
    dc = np.clip(c[None, :] - c[:, None] + wc - 1, 0, 2 * wc - 2)
    t_tab = jnp.zeros(rpb_l.shape[:2] + (w, w), F32)
    for d in range(2 * wc - 1):
        t_tab = jnp.where(jnp.asarray(dc == d)[None, None], rpb_l[:, :, d][:, :, None, None], t_tab)
    t_tab = jnp.where(jnp.asarray(col_ok)[None, None], t_tab.astype(F32), NEG_INF)
    neg_blk = jnp.full((rpb_l.shape[0], w, w), NEG_INF, F32)
    classes = []
    for r0 in (0, NAT_QROWS, rows - NAT_QROWS):
        ws = int(np.clip(r0 - wr // 2, 0, rows - NAT_KROWS))
        q_rows = []
        for a in range(NAT_QROWS):
            qr = r0 + a
            rs = int(np.clip(qr - wr // 2, 0, rows - wr))
            blks = []
            for b in range(NAT_KROWS):
                kr = ws + b
                blks.append(t_tab[:, kr - qr + wr - 1] if rs <= kr < rs + wr else neg_blk)
            q_rows.append(jnp.concatenate(blks, axis=-1))
        classes.append(jnp.concatenate(q_rows, axis=-2))
    return jnp.stack(classes, axis=0) * LOG2E


def _nat_kernel(q_ref, k_ref, v_ref, kc_ref, vc_ref, bias_ref, o_ref, kc_sc, vc_sc, *, rows):
    tq = NAT_QROWS * GRID_W
    tk = NAT_KROWS * GRID_W
    n_tiles = rows // NAT_QROWS
    kc_sc[...] = kc_ref[...].astype(BF16)
    vc_sc[...] = vc_ref[...].astype(BF16)

    def body(t, carry):
        r0 = t * NAT_QROWS
        ws = jnp.clip(r0 - NAT_ROWS // 2, 0, rows - NAT_KROWS)
        cls = jnp.where(t == 0, 0, jnp.where(t == n_tiles - 1, 2, 1))
        q = q_ref[pl.ds(pl.multiple_of(t * tq, tq), tq), :]
        k0 = pl.multiple_of(ws * GRID_W, GRID_W)
        s_loc = _qk_t(q, k_ref[pl.ds(k0, tk), :]) + bias_ref[cls]
        s_ctx = _qk_t(q, kc_sc[...])
        m = jnp.maximum(jnp.max(s_loc, axis=-1, keepdims=True),
                        jnp.max(s_ctx, axis=-1, keepdims=True))
        p_loc = jnp.exp2(s_loc - m)
        p_ctx = jnp.exp2(s_ctx - m)
        den = jnp.sum(p_loc, axis=-1, keepdims=True) + jnp.sum(p_ctx, axis=-1, keepdims=True)
        o = _mm(p_ctx.astype(BF16), vc_sc[...]) + _mm(p_loc.astype(BF16), v_ref[pl.ds(k0, tk), :])
        o_ref[pl.ds(pl.multiple_of(t * tq, tq), tq), :] = (o / den).astype(BF16)
        return carry

    lax.fori_loop(0, n_tiles, body, 0)


def _nat_attention(zq, cache_k, cache_v, bias, *, l, n_b, n_lat):
    past = cache_k.shape[2]
    rows = n_lat // GRID_W
    tq = NAT_QROWS * GRID_W
    tk = NAT_KROWS * GRID_W
    qb, kb, vb = OFF_QB // HEAD_DIM, OFF_KB // HEAD_DIM, OFF_VB // HEAD_DIM
    kern = functools.partial(_nat_kernel, rows=rows)
    nbytes = (8 * n_lat * HEAD_DIM * 2 + 4 * past * HEAD_DIM * 4 + 2 * past * HEAD_DIM * 2
              + 2 * 3 * tq * tk * 4 + 6 * tq * (tk + past) * 4)
    return pl.pallas_call(
        kern,
        grid=(n_b, N_HEADS),
        in_specs=[
            pl.BlockSpec((n_lat, HEAD_DIM), lambda b, h: (b, qb + h)),
            pl.BlockSpec((n_lat, HEAD_DIM), lambda b, h: (b, kb + h)),
            pl.BlockSpec((n_lat, HEAD_DIM), lambda b, h: (b, vb + h)),
            pl.BlockSpec((None, None, past, HEAD_DIM), lambda b, h: (b, l, 0, h)),
            pl.BlockSpec((None, None, past, HEAD_DIM), lambda b, h: (b, l, 0, h)),
            pl.BlockSpec((3, None, tq, tk), lambda b, h: (0, h, 0, 0)),
        ],
        out_specs=pl.BlockSpec((n_lat, HEAD_DIM), lambda b, h: (b, h)),
        out_shape=jax.ShapeDtypeStruct((n_b * n_lat, N_HEADS * HEAD_DIM), BF16),
        scratch_shapes=[pltpu.VMEM((past, HEAD_DIM), BF16), pltpu.VMEM((past, HEAD_DIM), BF16)],
        compiler_params=_params(("parallel", "parallel"), nbytes),
        name="attn_lat_nat",
    )(zq, zq, zq, cache_k, cache_v, bias)


def _win_kernel(q_ref, k_ref, v_ref, kc_ref, vc_ref, sink_ref, o_ref, k_sc, v_sc, bias_sc, p_sc,
                *, l, past, n_lat):
    grp = N_HEADS // KV_WIN
    band = 3 * Q_BLOCK
    n_blk = n_lat // Q_BLOCK
    g = pl.program_id(1)
    k_sc[0:past, :] = kc_ref[...].astype(BF16)
    v_sc[0:past, 0:HEAD_DIM] = vc_ref[...].astype(BF16)
    v_sc[:, HEAD_DIM:2 * HEAD_DIM] = jnp.ones((past + band, HEAD_DIM), BF16)
    qi = lax.broadcasted_iota(jnp.int32, (Q_BLOCK, band), 0)
    kj = lax.broadcasted_iota(jnp.int32, (Q_BLOCK, band), 1)
    for cls, off in enumerate((0, -Q_BLOCK, -2 * Q_BLOCK)):
        bias_sc[cls] = jnp.where(jnp.abs(kj + off - qi) <= WIN_RADIUS, 0.0, NEG_INF).astype(F32)

    def body(n, carry):
        start = pl.multiple_of(jnp.clip((n - 1) * Q_BLOCK, 0, n_lat - band), Q_BLOCK)
        cls = jnp.where(n == 0, 0, jnp.where(n == n_blk - 1, 2, 1))
        k_sc[past:past + band, :] = k_ref[pl.ds(start, band), :]
        v_sc[past:past + band, 0:HEAD_DIM] = v_ref[pl.ds(start, band), :]
        q0 = pl.multiple_of(n * Q_BLOCK, Q_BLOCK)
        qn = q_ref[pl.ds(q0, Q_BLOCK), :]
        q = jnp.concatenate([qn[:, h * HEAD_DIM:(h + 1) * HEAD_DIM] for h in range(grp)], axis=0)
        s = _qk_t(q, k_sc[...])
        bias = bias_sc[cls]
        e_sinks = []
        for h in range(grp):
            sh = s[h * Q_BLOCK:(h + 1) * Q_BLOCK]
            s_ctx = sh[:, 0:past]
            s_loc = sh[:, past:past + band] + bias
            sk = sink_ref[l, g * grp + h]
            m = jnp.maximum(jnp.maximum(jnp.max(s_ctx, axis=-1, keepdims=True),
                                        jnp.max(s_loc, axis=-1, keepdims=True)), sk)
            p_sc[h * Q_BLOCK:(h + 1) * Q_BLOCK, 0:past] = jnp.exp2(s_ctx - m).astype(BF16)
            p_sc[h * Q_BLOCK:(h + 1) * Q_BLOCK, past:past + band] = jnp.exp2(s_loc - m).astype(BF16)
            e_sinks.append(jnp.exp2(sk - m))
        acc = _mm(p_sc[...], v_sc[...])
        for h in range(grp):
            a = acc[h * Q_BLOCK:(h + 1) * Q_BLOCK]
            o = a[:, 0:HEAD_DIM] / (a[:, HEAD_DIM:2 * HEAD_DIM] + e_sinks[h])
            o_ref[pl.ds(q0, Q_BLOCK), h * HEAD_DIM:(h + 1) * HEAD_DIM] = o.astype(BF16)
        return carry

    lax.fori_loop(0, n_blk, body, 0)


def _win_attention(zq, cache_k, cache_v, sink2, *, l, n_b, n_lat):
    past = cache_k.shape[2]
    grp = N_HEADS // KV_WIN
    band = 3 * Q_BLOCK
    wq = grp * HEAD_DIM
    qc_blk, kc_blk, vc_blk = OFF_QC // wq, OFF_KC // HEAD_DIM, OFF_VC // HEAD_DIM
    kern = functools.partial(_win_kernel, l=l, past=past, n_lat=n_lat)
    nbytes = (4 * n_lat * wq * 2 + 4 * n_lat * HEAD_DIM * 2 + 4 * past * HEAD_DIM * 4
              + (past + band) * HEAD_DIM * 2 * 3 + 3 * Q_BLOCK * band * 4
              + 4 * grp * Q_BLOCK * (past + band) * 4)
    return pl.pallas_call(
        kern,
        grid=(n_b, KV_WIN),
        in_specs=[
            pl.BlockSpec((n_lat, wq), lambda b, g: (b, qc_blk + g)),
            pl.BlockSpec((n_lat, HEAD_DIM), lambda b, g: (b, kc_blk + g)),
            pl.BlockSpec((n_lat, HEAD_DIM), lambda b, g: (b, vc_blk + g)),
            pl.BlockSpec((None, None, past, HEAD_DIM), lambda b, g: (b, l, 0, g)),
            pl.BlockSpec((None, None, past, HEAD_DIM), lambda b, g: (b, l, 0, g)),
            pl.BlockSpec(memory_space=pltpu.SMEM),
        ],
        out_specs=pl.BlockSpec((n_lat, wq), lambda b, g: (b, g)),
        out_shape=jax.ShapeDtypeStruct((n_b * n_lat, N_HEADS * HEAD_DIM), BF16),
        scratch_shapes=[
            pltpu.VMEM((past + band, HEAD_DIM), BF16),
            pltpu.VMEM((past + band, 2 * HEAD_DIM), BF16),
            pltpu.VMEM((3, Q_BLOCK, band), F32),
            pltpu.VMEM((grp * Q_BLOCK, past + band), BF16),
        ],
        compiler_params=_params(("parallel", "parallel"), nbytes),
        name="attn_lat_window",
    )(zq, zq, zq, cache_k, cache_v, sink2)


def _merge_kernel(h_ref, oa_ref, ob_ref, oc_ref, wga_ref, wgb_ref, wgc_ref,
                  wba_ref, wbb_ref, wbc_ref, m_ref):
    h = h_ref[...]
    acc = None
    for o_ref, wg_ref, wb_ref in ((oa_ref, wga_ref, wba_ref), (ob_ref, wgb_ref, wbb_ref),
                                  (oc_ref, wgc_ref, wbc_ref)):
        gate = jax.nn.sigmoid(_mm(h, wg_ref[...].astype(BF16)))
        term = gate * _mm(o_ref[...], wb_ref[...].astype(BF16))
        acc = term if acc is None else acc + term
    m_ref[...] = acc.astype(BF16)


def _merge(h, outs, w_in, w_branch, *, l, tm):
    m, d = h.shape
    bw = w_branch.shape[2]
    tn = _divisor_tile(d, 256, 128)
    g0 = OFF_GATES // tn
    per = d // tn

    def o_spec(width, first):
        return pl.BlockSpec((tm, bw), lambda i, j: (i, first))

    in_specs = [pl.BlockSpec((tm, d), lambda i, j: (i, 0))]
    args = [h]
    for o, first in outs:
        in_specs.append(o_spec(bw, first))
        args.append(o)
    for k in range(3):
        in_specs.append(pl.BlockSpec((None, d, tn), lambda i, j, k=k: (l, 0, g0 + k * per + j)))
        args.append(w_in)
    for k in range(3):
        in_specs.append(pl.BlockSpec((None, None, bw, tn), lambda i, j, k=k: (l, k, 0, j)))
        args.append(w_branch)
    nbytes = (2 * tm * d * 2 + 6 * tm * bw * 2 + 6 * d * tn * 4 + 6 * bw * tn * 4
              + 3 * (d + bw) * tn * 2 + 6 * tm * tn * 4)
    return pl.pallas_call(
        _merge_kernel,
        grid=(m // tm, d // tn),
        in_specs=in_specs,
        out_specs=pl.BlockSpec((tm, tn), lambda i, j: (i, j)),
        out_shape=jax.ShapeDtypeStruct((m, d), BF16),
        compiler_params=_params(("parallel", "parallel"), nbytes),
        name="branch_merge",
    )(*args)


def _outproj_kernel(m_ref, w_ref, x_ref, gt_ref, o_ref, *, row_fn):
    r = row_fn(pl.program_id(0))
    y = _mm(m_ref[...], w_ref[...].astype(BF16))
    o_ref[...] = x_ref[...] + gt_ref[pl.ds(r, 1), :] * y


def _outproj(mm, w_o, x, mods, *, l, tm, row_fn):
    m, d = x.shape
    tn = _divisor_tile(d, 512, 128)
    kern = functools.partial(_outproj_kernel, row_fn=row_fn)
    nbytes = 2 * tm * d * 2 + 2 * d * tn * 4 + d * tn * 2 + 5 * tm * tn * 4
    return pl.pallas_call(
        kern,
        grid=(m // tm, d // tn),
        in_specs=[
            pl.BlockSpec((tm, d), lambda i, j: (i, 0)),
            pl.BlockSpec((None, d, tn), lambda i, j: (l, 0, j)),
            pl.BlockSpec((tm, tn), lambda i, j: (i, j)),
            pl.BlockSpec((None, 8, tn), lambda i, j: (l, 0, 5 * (d // tn) + j)),
        ],
        out_specs=pl.BlockSpec((tm, tn), lambda i, j: (i, j)),
        out_shape=jax.ShapeDtypeStruct((m, d), F32),
        compiler_params=_params(("parallel", "parallel"), nbytes),
        name="out_proj",
    )(mm, w_o, x, mods)


def _final_norm_kernel(x_ref, g_ref, o_ref):
    x = x_ref[...]
    ms = jnp.mean(x * x, axis=-1, keepdims=True)
    o_ref[...] = (x * lax.rsqrt(ms + EPS)) * g_ref[...]


def _final_norm(x, g_final):
    m, d = x.shape
    tm = _divisor_tile(m, 512, 8)
    return pl.pallas_call(
        _final_norm_kernel,
        grid=(m // tm,),
        in_specs=[pl.BlockSpec((tm, d), lambda i: (i, 0)), pl.BlockSpec((1, d), lambda i: (0, 0))],
        out_specs=pl.BlockSpec((tm, d), lambda i: (i, 0)),
        out_shape=jax.ShapeDtypeStruct((m, d), F32),
        compiler_params=_params(("parallel",), 4 * tm * d * 4),
        name="final_norm",
    )(x, g_final.reshape(1, d))


def _rope_tables(n_lat):
    half = HEAD_DIM // 2
    nf = half // 2
    t = jnp.arange(n_lat)
    row = (t // GRID_W).astype(F32)
    col = (t % GRID_W).astype(F32)
    inv = 1.0 / (ROPE_THETA ** (jnp.arange(nf, dtype=F32) / nf))
    ang_r = row[:, None] * inv[None, :]
    ang_c = col[:, None] * inv[None, :]
    cr, sr, cc, sc = jnp.cos(ang_r), jnp.sin(ang_r), jnp.cos(ang_c), jnp.sin(ang_c)
    zero = jnp.zeros_like(sr)
    cos = jnp.concatenate([cr, cr, cc, cc], axis=-1)
    sin_a = jnp.concatenate([-sr, zero, -sc, zero], axis=-1)
    sin_b = jnp.concatenate([zero, sr, zero, sc], axis=-1)
    return cos, sin_a, sin_b


def _trunk_layer(x, *, l, tm, row_fn, is_ctx, mods, g_norm, w_ffn1_gu, w_ffn1_down, w_in, g_q, g_k,
                 w_branch, w_o, w_ffn2_gu, w_ffn2_down, attend, rope_tabs):
    t = _ffn_gu(x, mods, g_norm, w_ffn1_gu, l=l, norm_idx=0, k_shift=0, k_scale=1, tm=tm,
                row_fn=row_fn)
    x = _ffn_down(t, w_ffn1_down, x, mods, l=l, k_gate=2, tm=tm, row_fn=row_fn)
    res = _inproj(x, mods, g_norm, g_q, g_k, w_in, rope_tabs, l=l, tm=tm, row_fn=row_fn,
                  is_ctx=is_ctx)
    zq, h2 = res[0], res[1]
    outs = attend(zq)
    mm = _merge(h2, outs, w_in, w_branch, l=l, tm=tm)
    x = _outproj(mm, w_o, x, mods, l=l, tm=tm, row_fn=row_fn)
    t = _ffn_gu(x, mods, g_norm, w_ffn2_gu, l=l, norm_idx=2, k_shift=6, k_scale=7, tm=tm,
                row_fn=row_fn)
    x = _ffn_down(t, w_ffn2_down, x, mods, l=l, k_gate=8, tm=tm, row_fn=row_fn)
    return x, (res[2] if is_ctx else None)


def kernel(x_prompt, x_sample, cache_k_gqa, cache_v_gqa, cache_k_nat, cache_v_nat, cache_k_win, cache_v_win, c, c_ctx, w_mod, b_mod, g_norm, w_ffn1_gu, w_ffn1_down, w_in, g_q, g_k, rpb, sink, w_branch, w_o, w_ffn2_gu, w_ffn2_down, g_final):
    batch, seq, d = x_prompt.shape
    n_b, n_lat, _ = x_sample.shape
    depth = w_mod.shape[0]
    past = cache_k_gqa.shape[2]
    rows = n_lat // GRID_W
    assert n_lat % GRID_W == 0 and rows % NAT_QROWS == 0 and rows >= NAT_KROWS + NAT_QROWS
    assert n_lat % Q_BLOCK == 0 and n_lat >= 3 * Q_BLOCK and n_b + 1 <= 8

    m_ctx = batch * seq
    tm_ctx = _divisor_tile(m_ctx, 1024, 16)
    tm_lat = _divisor_tile(n_lat, 1024, 16)
    per_lat = n_lat // tm_lat

    cond8 = jnp.zeros((8, d), F32).at[0].set(c_ctx).at[1:1 + n_b].set(c)
    mods = _modulation(cond8, w_mod, b_mod)

    sink2 = sink * LOG2E
    rope_tabs = _rope_tables(n_lat)
    ck_gqa = cache_k_gqa.reshape(n_b, depth, past, KV_GQA * HEAD_DIM)
    cv_gqa = cache_v_gqa.reshape(n_b, depth, past, KV_GQA * HEAD_DIM)
    ck_nat = cache_k_nat.reshape(n_b, depth, past, N_HEADS * HEAD_DIM)
    cv_nat = cache_v_nat.reshape(n_b, depth, past, N_HEADS * HEAD_DIM)
    ck_win = cache_k_win.reshape(n_b, depth, past, KV_WIN * HEAD_DIM)
    cv_win = cache_v_win.reshape(n_b, depth, past, KV_WIN * HEAD_DIM)

    weights = dict(g_norm=g_norm, w_ffn1_gu=w_ffn1_gu, w_ffn1_down=w_ffn1_down, w_in=w_in,
                   g_q=g_q, g_k=g_k, w_branch=w_branch, w_o=w_o, w_ffn2_gu=w_ffn2_gu,
                   w_ffn2_down=w_ffn2_down, mods=mods)

    y_p = x_prompt.reshape(m_ctx, d)
    y_s = x_sample.reshape(n_b * n_lat, d)
    kvn = []
    for l in range(depth):
        def attend_ctx(zq, l=l):
            o = _ctx_attention(zq, sink2, l=l, batch=batch, seq=seq)
            return [(o, 0), (o, 1), (o, 2)]

        y_p, kvn_l = _trunk_layer(y_p, l=l, tm=tm_ctx, row_fn=lambda i: 0, is_ctx=True,
                                  attend=attend_ctx, rope_tabs=None, **weights)
        kvn.append(kvn_l)

        nat_bias = _nat_bias_tables(rpb[l], rows)

        def attend_lat(zq, l=l, nat_bias=nat_bias):
            oa = _lat_gqa_attention(zq, ck_gqa, cv_gqa, l=l, n_b=n_b, n_lat=n_lat)
            ob = _nat_attention(zq, ck_nat, cv_nat, nat_bias, l=l, n_b=n_b, n_lat=n_lat)
            oc = _win_attention(zq, ck_win, cv_win, sink2, l=l, n_b=n_b, n_lat=n_lat)
            return [(oa, 0), (ob, 0), (oc, 0)]

        y_s, _ = _trunk_layer(y_s, l=l, tm=tm_lat, row_fn=lambda i: 1 + i // per_lat, is_ctx=False,
                              attend=attend_lat, rope_tabs=rope_tabs, **weights)

    y_prompt = _final_norm(y_p, g_final).reshape(batch, seq, d)
    y_sample = _final_norm(y_s, g_final).reshape(n_b, n_lat, d)

    kv = jnp.stack(kvn, axis=1)
    kv = kv.reshape(batch, seq, depth, KVN_W).transpose(0, 2, 1, 3)

    def part(off, heads):
        return kv[..., off:off + heads * HEAD_DIM].reshape(batch, depth, seq, heads, HEAD_DIM)

    return (y_prompt, y_sample,
            part(0, KV_GQA), part(256, KV_GQA),
            part(512, N_HEADS), part(1536, N_HEADS),
            part(2560, KV_WIN), part(2816, KV_WIN))
```

```python
import functools
import math

import numpy as np
import jax
import jax.numpy as jnp
from jax import lax
from jax.experimental import pallas as pl
from jax.experimental.pallas import tpu as pltpu

F32 = jnp.float32
BF16 = jnp.bfloat16

HEAD_DIM = 128
N_HEADS = 8
KV_GQA = 2
KV_WIN = 2
GRID_W = 64
NAT_ROWS = 8
NAT_COLS = 16
WIN_RADIUS = 128
Q_BLOCK = 128
ROPE_THETA = 10000.0
EPS = 1e-6
NEG_INF = -1e30
LOG2E = math.log2(math.e)
Q_SCALE = (HEAD_DIM ** -0.5) * LOG2E

OFF_QA, OFF_KA, OFF_VA = 0, 1024, 1280
OFF_QB, OFF_KB, OFF_VB = 1536, 2560, 3584
OFF_QC, OFF_KC, OFF_VC = 4608, 5632, 5888
OFF_GATES = 6144
QKV_W = 6144
COL_TILE = 512
KVN_W = 3072

V7X_VMEM_BYTES = 64 * 1024 * 1024
VMEM_CAP_BYTES = 60 * 1024 * 1024
NAT_QROWS = 4
NAT_KROWS = NAT_ROWS + NAT_QROWS


def _vmem_limit(nbytes):
    return int(min(VMEM_CAP_BYTES, max(32 * 1024 * 1024, nbytes * 3 // 2)))


def _params(sem, nbytes):
    return pltpu.CompilerParams(dimension_semantics=sem, vmem_limit_bytes=_vmem_limit(nbytes))


def _divisor_tile(n, cap, mult):
    if n <= cap:
        return n
    t = (cap // mult) * mult
    while t > mult and n % t:
        t -= mult
    assert n % t == 0, (n, cap, mult)
    return t


def _adaln(x, g, shift, scale):
    ms = jnp.mean(x * x, axis=-1, keepdims=True)
    y = x * lax.rsqrt(ms + EPS)
    return (y * g) * (1.0 + scale) + shift


def _headnorm(y, g):
    ms = jnp.mean(y * y, axis=-1, keepdims=True)
    return (y * lax.rsqrt(ms + EPS)) * g


def _silu(a):
    return a * jax.nn.sigmoid(a)


def _qk_t(q, k):
    return lax.dot_general(q, k, (((1,), (1,)), ((), ())), preferred_element_type=F32)


def _mm(a, b):
    return jnp.dot(a, b, preferred_element_type=F32)


def _mod_kernel(c_ref, w_ref, b_ref, o_ref):
    a = _silu(c_ref[...]).astype(BF16)
    o_ref[...] = _mm(a, w_ref[...].astype(BF16)) + b_ref[...]


def _modulation(cond8, w_mod, b_mod):
    depth, d, n = w_mod.shape
    tn = _divisor_tile(n, 1024, 128)
    nbytes = 2 * d * tn * 4 + d * tn * 2 + 4 * 8 * tn * 4 + 8 * d * 4
    return pl.pallas_call(
        _mod_kernel,
        grid=(depth, n // tn),
        in_specs=[
            pl.BlockSpec((8, d), lambda l, j: (0, 0)),
            pl.BlockSpec((None, d, tn), lambda l, j: (l, 0, j)),
            pl.BlockSpec((None, 1, tn), lambda l, j: (l, 0, j)),
        ],
        out_specs=pl.BlockSpec((None, 8, tn), lambda l, j: (l, 0, j)),
        out_shape=jax.ShapeDtypeStruct((depth, 8, n), F32),
        compiler_params=_params(("parallel", "parallel"), nbytes),
        name="modulation",
    )(cond8, w_mod, b_mod.reshape(depth, 1, n))


def _gu_kernel(x_ref, g_ref, sh_ref, sc_ref, wg_ref, wu_ref, t_ref, h_ref, *, row_fn, norm_idx):
    def step(h):
        a = _mm(h, wg_ref[...].astype(BF16))
        b = _mm(h, wu_ref[...].astype(BF16))
        t_ref[...] = (_silu(a) * b).astype(BF16)

    @pl.when(pl.program_id(1) == 0)
    def _():
        r = row_fn(pl.program_id(0))
        h = _adaln(x_ref[...], g_ref[norm_idx:norm_idx + 1, :],
                   sh_ref[pl.ds(r, 1), :], sc_ref[pl.ds(r, 1), :]).astype(BF16)
        h_ref[...] = h
        step(h)

    @pl.when(pl.program_id(1) > 0)
    def _():
        step(h_ref[...])


def _ffn_gu(x, mods, g_norm, w_gu, *, l, norm_idx, k_shift, k_scale, tm, row_fn):
    m, d = x.shape
    d_ff = w_gu.shape[2] // 2
    tf = _divisor_tile(d_ff, 512, 128)
    nj = d_ff // tf
    nbytes = (2 * tm * d * 4 + tm * d * 2 + 4 * d * tf * 4 + 2 * d * tf * 2
              + 2 * tm * tf * 4 + 2 * tm * tf * 2)
    kern = functools.partial(_gu_kernel, row_fn=row_fn, norm_idx=norm_idx)
    return pl.pallas_call(
        kern,
        grid=(m // tm, nj),
        in_specs=[
            pl.BlockSpec((tm, d), lambda i, j: (i, 0)),
            pl.BlockSpec((None, 3, d), lambda i, j: (l, 0, 0)),
            pl.BlockSpec((None, 8, d), lambda i, j: (l, 0, k_shift)),
            pl.BlockSpec((None, 8, d), lambda i, j: (l, 0, k_scale)),
            pl.BlockSpec((None, d, tf), lambda i, j: (l, 0, j)),
            pl.BlockSpec((None, d, tf), lambda i, j: (l, 0, nj + j)),
        ],
        out_specs=pl.BlockSpec((tm, tf), lambda i, j: (i, j)),
        out_shape=jax.ShapeDtypeStruct((m, d_ff), BF16),
        scratch_shapes=[pltpu.VMEM((tm, d), BF16)],
        compiler_params=_params(("parallel", "arbitrary"), nbytes),
        name="ffn_gate_up",
    )(x, g_norm, mods, mods, w_gu, w_gu)


def _down_kernel(t_ref, w_ref, x_ref, gt_ref, o_ref, *, row_fn):
    r = row_fn(pl.program_id(0))
    y = _mm(t_ref[...], w_ref[...].astype(BF16))
    o_ref[...] = x_ref[...] + (0.5 * gt_ref[pl.ds(r, 1), :]) * y


def _ffn_down(t, w_down, x, mods, *, l, k_gate, tm, row_fn):
    m, d_ff = t.shape
    d = x.shape[1]
    tn = _divisor_tile(d, 256, 128)
    nbytes = 2 * tm * d_ff * 2 + 2 * d_ff * tn * 4 + d_ff * tn * 2 + 5 * tm * tn * 4
    kern = functools.partial(_down_kernel, row_fn=row_fn)
    return pl.pallas_call(
        kern,
        grid=(m // tm, d // tn),
        in_specs=[
            pl.BlockSpec((tm, d_ff), lambda i, j: (i, 0)),
            pl.BlockSpec((None, d_ff, tn), lambda i, j: (l, 0, j)),
            pl.BlockSpec((tm, tn), lambda i, j: (i, j)),
            pl.BlockSpec((None, 8, tn), lambda i, j: (l, 0, k_gate * (d // tn) + j)),
        ],
        out_specs=pl.BlockSpec((tm, tn), lambda i, j: (i, j)),
        out_shape=jax.ShapeDtypeStruct((m, d), F32),
        compiler_params=_params(("parallel", "parallel"), nbytes),
        name="ffn_down",
    )(t, w_down, x, mods)


def _kvn_block(j):
    return jnp.where(j <= 2, 0, jnp.where(j >= 9, 5, jnp.clip(j - 4, 1, 4)))


def _inproj_kernel(*refs, is_ctx, row_fn, norm_idx):
    if is_ctx:
        (x_ref, g_ref, sh_ref, sc_ref, gq_ref, gk_ref, w_ref, zq_ref, h_ref, kvn_ref,
         acc_ref) = refs
    else:
        (x_ref, g_ref, sh_ref, sc_ref, gq_ref, gk_ref, cos_ref, sa_ref, sb_ref, w_ref,
         zq_ref, h_ref, acc_ref) = refs
    j = pl.program_id(1)

    def rope(y):
        if is_ctx:
            return y
        return (y * cos_ref[...] + pltpu.roll(y, 96, 1) * sa_ref[...]
                + pltpu.roll(y, 32, 1) * sb_ref[...])

    def cols(hh):
        return slice(hh * HEAD_DIM, (hh + 1) * HEAD_DIM)

    def project(h):
        return _mm(h, w_ref[...].astype(BF16))

    def query_norm_rope(acc):
        for hh in range(4):
            y = _headnorm(acc[:, cols(hh)], gq_ref[...])
            zq_ref[:, cols(hh)] = (rope(y) * Q_SCALE).astype(BF16)

    @pl.when(j == 0)
    def _():
        r = row_fn(pl.program_id(0))
        h = _adaln(x_ref[...], g_ref[norm_idx:norm_idx + 1, :],
                   sh_ref[pl.ds(r, 1), :], sc_ref[pl.ds(r, 1), :]).astype(BF16)
        h_ref[...] = h
        acc_ref[...] = project(h)

    @pl.when((j == 1) | (j == 2))
    def _():
        acc_ref[...] = project(h_ref[...])

    @pl.when(j < 2)
    def _():
        query_norm_rope(acc_ref[...])

    @pl.when(j == 2)
    def _():
        acc = acc_ref[...]
        for hh in range(2):
            y = _headnorm(acc[:, cols(hh)], gk_ref[...])
            if is_ctx:
                kvn_ref[:, cols(hh)] = y
            zq_ref[:, cols(hh)] = rope(y).astype(BF16)
        va = acc[:, 256:512]
        if is_ctx:
            kvn_ref[:, 256:512] = va
        zq_ref[:, 256:512] = va.astype(BF16)

    @pl.when((j == 3) | (j == 4))
    def _():
        zq_ref[...] = (project(h_ref[...]) * Q_SCALE).astype(BF16)

    @pl.when((j >= 5) & (j <= 8))
    def _():
        acc = project(h_ref[...])
        if is_ctx:
            kvn_ref[...] = acc
        zq_ref[...] = acc.astype(BF16)

    @pl.when((j == 9) | (j == 10))
    def _():
        acc = project(h_ref[...])
        for hh in range(4):
            zq_ref[:, cols(hh)] = (rope(acc[:, cols(hh)]) * Q_SCALE).astype(BF16)

    @pl.when(j == 11)
    def _():
        acc = project(h_ref[...])
        for hh in range(2):
            y = acc[:, cols(hh)]
            if is_ctx:
                kvn_ref[:, cols(hh)] = y
            zq_ref[:, cols(hh)] = rope(y).astype(BF16)
        vc = acc[:, 256:512]
        if is_ctx:
            kvn_ref[:, 256:512] = vc
        zq_ref[:, 256:512] = vc.astype(BF16)


def _inproj(x, mods, g_norm, g_q, g_k, w_in, rope_tabs, *, l, tm, row_fn, is_ctx):
    m, d = x.shape
    depth = w_in.shape[0]
    nj = QKV_W // COL_TILE
    in_specs = [
        pl.BlockSpec((tm, d), lambda i, j: (i, 0)),
        pl.BlockSpec((None, 3, d), lambda i, j: (l, 0, 0)),
        pl.BlockSpec((None, 8, d), lambda i, j: (l, 0, 3)),
        pl.BlockSpec((None, 8, d), lambda i, j: (l, 0, 4)),
        pl.BlockSpec((None, 1, HEAD_DIM), lambda i, j: (l, 0, 0)),
        pl.BlockSpec((None, 1, HEAD_DIM), lambda i, j: (l, 0, 0)),
    ]
    args = [x, g_norm, mods, mods, g_q.reshape(depth, 1, HEAD_DIM), g_k.reshape(depth, 1, HEAD_DIM)]
    if not is_ctx:
        n_pos = rope_tabs[0].shape[0]
        per = n_pos // tm
        for tab in rope_tabs:
            in_specs.append(pl.BlockSpec((tm, HEAD_DIM), lambda i, j: (i % per, 0)))
            args.append(tab)
    in_specs.append(pl.BlockSpec((None, d, COL_TILE), lambda i, j: (l, 0, j)))
    args.append(w_in)
    out_specs = [
        pl.BlockSpec((tm, COL_TILE), lambda i, j: (i, j)),
        pl.BlockSpec((tm, d), lambda i, j: (i, 0)),
    ]
    out_shape = [jax.ShapeDtypeStruct((m, QKV_W), BF16), jax.ShapeDtypeStruct((m, d), BF16)]
    if is_ctx:
        out_specs.append(pl.BlockSpec((tm, COL_TILE), lambda i, j: (i, _kvn_block(j))))
        out_shape.append(jax.ShapeDtypeStruct((m, KVN_W), F32))
    nbytes = (2 * tm * d * 4 + 2 * tm * d * 2 + 2 * d * COL_TILE * 4 + d * COL_TILE * 2
              + 3 * tm * COL_TILE * 4 + 2 * tm * COL_TILE * 2 + 6 * tm * HEAD_DIM * 4)
    kern = functools.partial(_inproj_kernel, is_ctx=is_ctx, row_fn=row_fn, norm_idx=1)
    return pl.pallas_call(
        kern,
        grid=(m // tm, nj),
        in_specs=in_specs,
        out_specs=out_specs,
        out_shape=out_shape,
        scratch_shapes=[pltpu.VMEM((tm, COL_TILE), F32)],
        compiler_params=_params(("parallel", "arbitrary"), nbytes),
        name="inproj_ctx" if is_ctx else "inproj_lat",
    )(*args)


def _ctx_attn_kernel(z_ref, sink_ref, o_ref, *, l, seq):
    def attend(q_off, n_q, k_off, v_off, sinks):
        q = jnp.concatenate(
            [z_ref[:, q_off + h * HEAD_DIM:q_off + (h + 1) * HEAD_DIM] for h in range(n_q)], axis=0)
        k = z_ref[:, k_off:k_off + HEAD_DIM]
        v = z_ref[:, v_off:v_off + HEAD_DIM]
        s = _qk_t(q, k)
        m = jnp.max(s, axis=-1, keepdims=True)
        if sinks is not None:
            sink_col = jnp.concatenate(
                [jnp.full((seq, 1), sk, F32) for sk in sinks], axis=0)
            m = jnp.maximum(m, sink_col)
        p = jnp.exp2(s - m)
        den = jnp.sum(p, axis=-1, keepdims=True)
        if sinks is not None:
            den = den + jnp.exp2(sink_col - m)
        return _mm(p.astype(BF16), v) / den

    def store(o, o_off, n_q):
        for h in range(n_q):
            o_ref[:, o_off + h * HEAD_DIM:o_off + (h + 1) * HEAD_DIM] = (
                o[h * seq:(h + 1) * seq].astype(BF16))

    grp = N_HEADS // KV_GQA
    for g in range(KV_GQA):
        o = attend(OFF_QA + g * grp * HEAD_DIM, grp, OFF_KA + g * HEAD_DIM,
                   OFF_VA + g * HEAD_DIM, None)
        store(o, g * grp * HEAD_DIM, grp)
    for h in range(N_HEADS):
        o = attend(OFF_QB + h * HEAD_DIM, 1, OFF_KB + h * HEAD_DIM, OFF_VB + h * HEAD_DIM, None)
        store(o, 1024 + h * HEAD_DIM, 1)
    grp = N_HEADS // KV_WIN
    for g in range(KV_WIN):
        sinks = [sink_ref[l, g * grp + h] for h in range(grp)]
        o = attend(OFF_QC + g * grp * HEAD_DIM, grp, OFF_KC + g * HEAD_DIM,
                   OFF_VC + g * HEAD_DIM, sinks)
        store(o, 2048 + g * grp * HEAD_DIM, grp)


def _ctx_attention(zq, sink2, *, l, batch, seq):
    kern = functools.partial(_ctx_attn_kernel, l=l, seq=seq)
    nbytes = 2 * seq * QKV_W * 2 + 2 * seq * 3072 * 2 + 8 * 4 * seq * seq * 4
    return pl.pallas_call(
        kern,
        grid=(batch,),
        in_specs=[
            pl.BlockSpec((seq, QKV_W), lambda b: (b, 0)),
            pl.BlockSpec(memory_space=pltpu.SMEM),
        ],
        out_specs=pl.BlockSpec((seq, 3072), lambda b: (b, 0)),
        out_shape=jax.ShapeDtypeStruct((batch * seq, 3072), BF16),
        compiler_params=_params(("parallel",), nbytes),
        name="attn_ctx",
    )(zq, sink2)


def _lat_gqa_kernel(q_ref, kl_ref, vl_ref, kc_ref, vc_ref, o_ref, k_sc, v_sc, s0_sc, s1_sc,
                    m0_sc, m1_sc, *, past, n_lat, tq, chunk):
    grp = N_HEADS // KV_GQA
    total = past + n_lat
    n_chunk = total // chunk
    nq = n_lat // tq
    k_sc[0:past, :] = kc_ref[...].astype(BF16)
    k_sc[past:total, :] = kl_ref[...]
    v_sc[0:past, 0:HEAD_DIM] = vc_ref[...].astype(BF16)
    v_sc[past:total, 0:HEAD_DIM] = vl_ref[...]
    v_sc[:, HEAD_DIM:2 * HEAD_DIM] = jnp.ones((total, HEAD_DIM), BF16)

    def row0(t):
        return t * tq if isinstance(t, int) else pl.multiple_of(t * tq, tq)

    def scores(t, s_sc, m_sc):
        qn = q_ref[pl.ds(row0(t), tq), :]
        q = jnp.concatenate([qn[:, h * HEAD_DIM:(h + 1) * HEAD_DIM] for h in range(grp)], axis=0)
        m = None
        for c in range(n_chunk):
            s = _qk_t(q, k_sc[c * chunk:(c + 1) * chunk, :])
            s_sc[:, c * chunk:(c + 1) * chunk] = s
            mc = jnp.max(s, axis=-1, keepdims=True)
            m = mc if m is None else jnp.maximum(m, mc)
        m_sc[...] = m

    def values(t, s_sc, m_sc):
        m = m_sc[...]
        acc = None
        for c in range(n_chunk):
            p = jnp.exp2(s_sc[:, c * chunk:(c + 1) * chunk] - m).astype(BF16)
            term = _mm(p, v_sc[c * chunk:(c + 1) * chunk, :])
            acc = term if acc is None else acc + term
        o = acc[:, 0:HEAD_DIM] / acc[:, HEAD_DIM:2 * HEAD_DIM]
        for h in range(grp):
            o_ref[pl.ds(row0(t), tq), h * HEAD_DIM:(h + 1) * HEAD_DIM] = (
                o[h * tq:(h + 1) * tq].astype(BF16))

    scores(0, s0_sc, m0_sc)

    def body(k, carry):
        t = 2 * k
        scores(t + 1, s1_sc, m1_sc)
        values(t, s0_sc, m0_sc)
        scores(t + 2, s0_sc, m0_sc)
        values(t + 1, s1_sc, m1_sc)
        return carry

    lax.fori_loop(0, nq // 2 - 1, body, 0)
    scores(nq - 1, s1_sc, m1_sc)
    values(nq - 2, s0_sc, m0_sc)
    values(nq - 1, s1_sc, m1_sc)


def _lat_gqa_attention(zq, cache_k, cache_v, *, l, n_b, n_lat):
    past = cache_k.shape[2]
    grp = N_HEADS // KV_GQA
    tq = _divisor_tile(n_lat, 128, 16)
    assert (n_lat // tq) % 2 == 0
    total = past + n_lat
    chunk = _divisor_tile(total, 1536, 128)
    wq = grp * HEAD_DIM
    ka_blk = OFF_KA // HEAD_DIM
    va_blk = OFF_VA // HEAD_DIM
    kern = functools.partial(_lat_gqa_kernel, past=past, n_lat=n_lat, tq=tq, chunk=chunk)
    nbytes = (4 * n_lat * wq * 2 + 4 * n_lat * HEAD_DIM * 2 + 4 * past * HEAD_DIM * 4
              + total * HEAD_DIM * 2 * 3 + 2 * grp * tq * total * 4 + 4 * grp * tq * chunk * 4)
    return pl.pallas_call(
        kern,
        grid=(n_b, KV_GQA),
        in_specs=[
            pl.BlockSpec((n_lat, wq), lambda b, g: (b, g)),
            pl.BlockSpec((n_lat, HEAD_DIM), lambda b, g: (b, ka_blk + g)),
            pl.BlockSpec((n_lat, HEAD_DIM), lambda b, g: (b, va_blk + g)),
            pl.BlockSpec((None, None, past, HEAD_DIM), lambda b, g: (b, l, 0, g)),
            pl.BlockSpec((None, None, past, HEAD_DIM), lambda b, g: (b, l, 0, g)),
        ],
        out_specs=pl.BlockSpec((n_lat, wq), lambda b, g: (b, g)),
        out_shape=jax.ShapeDtypeStruct((n_b * n_lat, N_HEADS * HEAD_DIM), BF16),
        scratch_shapes=[
            pltpu.VMEM((total, HEAD_DIM), BF16),
            pltpu.VMEM((total, 2 * HEAD_DIM), BF16),
            pltpu.VMEM((grp * tq, total), F32),
            pltpu.VMEM((grp * tq, total), F32),
            pltpu.VMEM((grp * tq, 1), F32),
            pltpu.VMEM((grp * tq, 1), F32),
        ],
        compiler_params=_params(("parallel", "parallel"), nbytes),
        name="attn_lat_global",
    )(zq, zq, zq, cache_k, cache_v)


def _nat_bias_tables(rpb_l, rows):
    w, wr, wc = GRID_W, NAT_ROWS, NAT_COLS
    c = np.arange(w)
    cs = np.clip(c - wc // 2, 0, w - wc)
    col_ok = (c[None, :] >= cs[:, None]) & (c[None, :] < cs[:, None] + wc)
    dc = np.clip(c[None, :] - c[:, None] + wc - 1, 0, 2 * wc - 2)
    t_tab = jnp.zeros(rpb_l.shape[:2] + (w, w), F32)
    for d in range(2 * wc - 1):
        t_tab = jnp.where(jnp.asarray(dc == d)[None, None], rpb_l[:, :, d][:, :, None, None], t_tab)
    t_tab = jnp.where(jnp.asarray(col_ok)[None, None], t_tab.astype(F32), NEG_INF)
    neg_blk = jnp.full((rpb_l.shape[0], w, w), NEG_INF, F32)
    classes = []
    for r0 in (0, NAT_QROWS, rows - NAT_QROWS):
        ws = int(np.clip(r0 - wr // 2, 0, rows - NAT_KROWS))
        q_rows = []
        for a in range(NAT_QROWS):
            qr = r0 + a
            rs = int(np.clip(qr - wr // 2, 0, rows - wr))
            blks = []
            for b in range(NAT_KROWS):
                kr = ws + b
                blks.append(t_tab[:, kr - qr + wr - 1] if rs <= kr < rs + wr else neg_blk)
            q_rows.append(jnp.concatenate(blks, axis=-1))
        classes.append(jnp.concatenate(q_rows, axis=-2))
    return jnp.stack(classes, axis=0) * LOG2E


def _nat_kernel(q_ref, k_ref, v_ref, kc_ref, vc_ref, bias_ref, o_ref, kc_sc, vc_sc, *, rows):
    tq = NAT_QROWS * GRID_W
    tk = NAT_KROWS * GRID_W
    n_tiles = rows // NAT_QROWS
    kc_sc[...] = kc_ref[...].astype(BF16)
    vc_sc[...] = vc_ref[...].astype(BF16)

    def body(t, carry):
        r0 = t * NAT_QROWS
        ws = jnp.clip(r0 - NAT_ROWS // 2, 0, rows - NAT_KROWS)
        cls = jnp.where(t == 0, 0, jnp.where(t == n_tiles - 1, 2, 1))
        q = q_ref[pl.ds(pl.multiple_of(t * tq, tq), tq), :]
        k0 = pl.multiple_of(ws * GRID_W, GRID_W)
        s_loc = _qk_t(q, k_ref[pl.ds(k0, tk), :]) + bias_ref[cls]
        s_ctx = _qk_t(q, kc_sc[...])
        m = jnp.maximum(jnp.max(s_loc, axis=-1, keepdims=True),
                        jnp.max(s_ctx, axis=-1, keepdims=True))
        p_loc = jnp.exp2(s_loc - m)
        p_ctx = jnp.exp2(s_ctx - m)
        den = jnp.sum(p_loc, axis=-1, keepdims=True) + jnp.sum(p_ctx, axis=-1, keepdims=True)
        o = _mm(p_ctx.astype(BF16), vc_sc[...]) + _mm(p_loc.astype(BF16), v_ref[pl.ds(k0, tk), :])
        o_ref[pl.ds(pl.multiple_of(t * tq, tq), tq), :] = (o / den).astype(BF16)
        return carry

    lax.fori_loop(0, n_tiles, body, 0, unroll=2)


def _nat_attention(zq, cache_k, cache_v, bias, *, l, n_b, n_lat):
    past = cache_k.shape[2]
    rows = n_lat // GRID_W
    tq = NAT_QROWS * GRID_W
    tk = NAT_KROWS * GRID_W
    qb, kb, vb = OFF_QB // HEAD_DIM, OFF_KB // HEAD_DIM, OFF_VB // HEAD_DIM
    kern = functools.partial(_nat_kernel, rows=rows)
    nbytes = (8 * n_lat * HEAD_DIM * 2 + 4 * past * HEAD_DIM * 4 + 2 * past * HEAD_DIM * 2
              + 2 * 3 * tq * tk * 4 + 6 * tq * (tk + past) * 4)
    return pl.pallas_call(
        kern,
        grid=(n_b, N_HEADS),
        in_specs=[
            pl.BlockSpec((n_lat, HEAD_DIM), lambda b, h: (b, qb + h)),
            pl.BlockSpec((n_lat, HEAD_DIM), lambda b, h: (b, kb + h)),
            pl.BlockSpec((n_lat, HEAD_DIM), lambda b, h: (b, vb + h)),
            pl.BlockSpec((None, None, past, HEAD_DIM), lambda b, h: (b, l, 0, h)),
            pl.BlockSpec((None, None, past, HEAD_DIM), lambda b, h: (b, l, 0, h)),
            pl.BlockSpec((3, None, tq, tk), lambda b, h: (0, h, 0, 0)),
        ],
        out_specs=pl.BlockSpec((n_lat, HEAD_DIM), lambda b, h: (b, h)),
        out_shape=jax.ShapeDtypeStruct((n_b * n_lat, N_HEADS * HEAD_DIM), BF16),
        scratch_shapes=[pltpu.VMEM((past, HEAD_DIM), BF16), pltpu.VMEM((past, HEAD_DIM), BF16)],
        compiler_params=_params(("parallel", "parallel"), nbytes),
        name="attn_lat_nat",
    )(zq, zq, zq, cache_k, cache_v, bias)


def _win_kernel(q_ref, k_ref, v_ref, kc_ref, vc_ref, sink_ref, o_ref, kc_sc, vc_sc, bias_sc,
                *, l, past, n_lat):
    grp = N_HEADS // KV_WIN
    band = 3 * Q_BLOCK
    n_blk = n_lat // Q_BLOCK
    g = pl.program_id(1)
    kc_sc[...] = kc_ref[...].astype(BF16)
    vc_sc[...] = vc_ref[...].astype(BF16)
    qi = lax.broadcasted_iota(jnp.int32, (Q_BLOCK, band), 0)
    kj = lax.broadcasted_iota(jnp.int32, (Q_BLOCK, band), 1)
    for cls, off in enumerate((0, -Q_BLOCK, -2 * Q_BLOCK)):
        bias_sc[cls] = jnp.where(jnp.abs(kj + off - qi) <= WIN_RADIUS, 0.0, NEG_INF).astype(F32)

    def body(n, carry):
        start = pl.multiple_of(jnp.clip((n - 1) * Q_BLOCK, 0, n_lat - band), Q_BLOCK)
        cls = jnp.where(n == 0, 0, jnp.where(n == n_blk - 1, 2, 1))
        q0 = pl.multiple_of(n * Q_BLOCK, Q_BLOCK)
        qn = q_ref[pl.ds(q0, Q_BLOCK), :]
        q = jnp.concatenate([qn[:, h * HEAD_DIM:(h + 1) * HEAD_DIM] for h in range(grp)], axis=0)
        s_ctx_all = _qk_t(q, kc_sc[...])
        s_loc_all = _qk_t(q, k_ref[pl.ds(start, band), :])
        bias = bias_sc[cls]
        p_ctx, p_loc, dens = [], [], []
        for h in range(grp):
            s_ctx = s_ctx_all[h * Q_BLOCK:(h + 1) * Q_BLOCK]
            s_loc = s_loc_all[h * Q_BLOCK:(h + 1) * Q_BLOCK] + bias
            sk = sink_ref[l, g * grp + h]
            m = jnp.maximum(jnp.maximum(jnp.max(s_ctx, axis=-1, keepdims=True),
                                        jnp.max(s_loc, axis=-1, keepdims=True)), sk)
            pc = jnp.exp2(s_ctx - m)
            pw = jnp.exp2(s_loc - m)
            dens.append(jnp.sum(pc, axis=-1, keepdims=True) + jnp.sum(pw, axis=-1, keepdims=True)
                        + jnp.exp2(sk - m))
            p_ctx.append(pc.astype(BF16))
            p_loc.append(pw.astype(BF16))
        acc = (_mm(jnp.concatenate(p_ctx, axis=0), vc_sc[...])
               + _mm(jnp.concatenate(p_loc, axis=0), v_ref[pl.ds(start, band), :]))
        for h in range(grp):
            o = acc[h * Q_BLOCK:(h + 1) * Q_BLOCK] / dens[h]
            o_ref[pl.ds(q0, Q_BLOCK), h * HEAD_DIM:(h + 1) * HEAD_DIM] = o.astype(BF16)
        return carry

    lax.fori_loop(0, n_blk, body, 0, unroll=2)


def _win_attention(zq, cache_k, cache_v, sink2, *, l, n_b, n_lat):
    past = cache_k.shape[2]
    grp = N_HEADS // KV_WIN
    band = 3 * Q_BLOCK
    wq = grp * HEAD_DIM
    qc_blk, kc_blk, vc_blk = OFF_QC // wq, OFF_KC // HEAD_DIM, OFF_VC // HEAD_DIM
    kern = functools.partial(_win_kernel, l=l, past=past, n_lat=n_lat)
    nbytes = (4 * n_lat * wq * 2 + 4 * n_lat * HEAD_DIM * 2 + 4 * past * HEAD_DIM * 4
              + (past + band) * HEAD_DIM * 2 * 3 + 3 * Q_BLOCK * band * 4
              + 4 * grp * Q_BLOCK * (past + band) * 4)
    return pl.pallas_call(
        kern,
        grid=(n_b, KV_WIN),
        in_specs=[
            pl.BlockSpec((n_lat, wq), lambda b, g: (b, qc_blk + g)),
            pl.BlockSpec((n_lat, HEAD_DIM), lambda b, g: (b, kc_blk + g)),
            pl.BlockSpec((n_lat, HEAD_DIM), lambda b, g: (b, vc_blk + g)),
            pl.BlockSpec((None, None, past, HEAD_DIM), lambda b, g: (b, l, 0, g)),
            pl.BlockSpec((None, None, past, HEAD_DIM), lambda b, g: (b, l, 0, g)),
            pl.BlockSpec(memory_space=pltpu.SMEM),
        ],
        out_specs=pl.BlockSpec((n_lat, wq), lambda b, g: (b, g)),
        out_shape=jax.ShapeDtypeStruct((n_b * n_lat, N_HEADS * HEAD_DIM), BF16),
        scratch_shapes=[
            pltpu.VMEM((past, HEAD_DIM), BF16),
            pltpu.VMEM((past, HEAD_DIM), BF16),
            pltpu.VMEM((3, Q_BLOCK, band), F32),
        ],
        compiler_params=_params(("parallel", "parallel"), nbytes),
        name="attn_lat_window",
    )(zq, zq, zq, cache_k, cache_v, sink2)


def _merge_kernel(h_ref, oa_ref, ob_ref, oc_ref, wga_ref, wgb_ref, wgc_ref,
                  wba_ref, wbb_ref, wbc_ref, m_ref):
    h = h_ref[...]
    acc = None
    for o_ref, wg_ref, wb_ref in ((oa_ref, wga_ref, wba_ref), (ob_ref, wgb_ref, wbb_ref),
                                  (oc_ref, wgc_ref, wbc_ref)):
        gate = jax.nn.sigmoid(_mm(h, wg_ref[...].astype(BF16)))
        term = gate * _mm(o_ref[...], wb_ref[...].astype(BF16))
        acc = term if acc is None else acc + term
    m_ref[...] = acc.astype(BF16)


def _merge(h, outs, w_in, w_branch, *, l, tm):
    m, d = h.shape
    bw = w_branch.shape[2]
    tn = _divisor_tile(d, 256, 128)
    g0 = OFF_GATES // tn
    per = d // tn

    def o_spec(width, first):
        return pl.BlockSpec((tm, bw), lambda i, j: (i, first))

    in_specs = [pl.BlockSpec((tm, d), lambda i, j: (i, 0))]
    args = [h]
    for o, first in outs:
        in_specs.append(o_spec(bw, first))
        args.append(o)
    for k in range(3):
        in_specs.append(pl.BlockSpec((None, d, tn), lambda i, j, k=k: (l, 0, g0 + k * per + j)))
        args.append(w_in)
    for k in range(3):
        in_specs.append(pl.BlockSpec((None, None, bw, tn), lambda i, j, k=k: (l, k, 0, j)))
        args.append(w_branch)
    nbytes = (2 * tm * d * 2 + 6 * tm * bw * 2 + 6 * d * tn * 4 + 6 * bw * tn * 4
              + 3 * (d + bw) * tn * 2 + 6 * tm * tn * 4)
    return pl.pallas_call(
        _merge_kernel,
        grid=(m // tm, d // tn),
        in_specs=in_specs,
        out_specs=pl.BlockSpec((tm, tn), lambda i, j: (i, j)),
        out_shape=jax.ShapeDtypeStruct((m, d), BF16),
        compiler_params=_params(("parallel", "parallel"), nbytes),
        name="branch_merge",
    )(*args)


def _outproj_kernel(m_ref, w_ref, x_ref, gt_ref, o_ref, *, row_fn):
    r = row_fn(pl.program_id(0))
    y = _mm(m_ref[...], w_ref[...].astype(BF16))
    o_ref[...] = x_ref[...] + gt_ref[pl.ds(r, 1), :] * y


def _outproj(mm, w_o, x, mods, *, l, tm, row_fn):
    m, d = x.shape
    tn = _divisor_tile(d, 512, 128)
    kern = functools.partial(_outproj_kernel, row_fn=row_fn)
    nbytes = 2 * tm * d * 2 + 2 * d * tn * 4 + d * tn * 2 + 5 * tm * tn * 4
    return pl.pallas_call(
        kern,
        grid=(m // tm, d // tn),
        in_specs=[
            pl.BlockSpec((tm, d), lambda i, j: (i, 0)),
            pl.BlockSpec((None, d, tn), lambda i, j: (l, 0, j)),
            pl.BlockSpec((tm, tn), lambda i, j: (i, j)),
            pl.BlockSpec((None, 8, tn), lambda i, j: (l, 0, 5 * (d // tn) + j)),
        ],
        out_specs=pl.BlockSpec((tm, tn), lambda i, j: (i, j)),
        out_shape=jax.ShapeDtypeStruct((m, d), F32),
        compiler_params=_params(("parallel", "parallel"), nbytes),
        name="out_proj",
    )(mm, w_o, x, mods)


def _final_norm_kernel(x_ref, g_ref, o_ref):
    x = x_ref[...]
    ms = jnp.mean(x * x, axis=-1, keepdims=True)
    o_ref[...] = (x * lax.rsqrt(ms + EPS)) * g_ref[...]


def _final_norm(x, g_final):
    m, d = x.shape
    tm = _divisor_tile(m, 512, 8)
    return pl.pallas_call(
        _final_norm_kernel,
        grid=(m // tm,),
        in_specs=[pl.BlockSpec((tm, d), lambda i: (i, 0)), pl.BlockSpec((1, d), lambda i: (0, 0))],
        out_specs=pl.BlockSpec((tm, d), lambda i: (i, 0)),
        out_shape=jax.ShapeDtypeStruct((m, d), F32),
        compiler_params=_params(("parallel",), 4 * tm * d * 4),
        name="final_norm",
    )(x, g_final.reshape(1, d))


def _rope_tables(n_lat):
    half = HEAD_DIM // 2
    nf = half // 2
    t = jnp.arange(n_lat)
    row = (t // GRID_W).astype(F32)
    col = (t % GRID_W).astype(F32)
    inv = 1.0 / (ROPE_THETA ** (jnp.arange(nf, dtype=F32) / nf))
    ang_r = row[:, None] * inv[None, :]
    ang_c = col[:, None] * inv[None, :]
    cr, sr, cc, sc = jnp.cos(ang_r), jnp.sin(ang_r), jnp.cos(ang_c), jnp.sin(ang_c)
    zero = jnp.zeros_like(sr)
    cos = jnp.concatenate([cr, cr, cc, cc], axis=-1)
    sin_a = jnp.concatenate([-sr, zero, -sc, zero], axis=-1)
    sin_b = jnp.concatenate([zero, sr, zero, sc], axis=-1)
    return cos, sin_a, sin_b


def _trunk_layer(x, *, l, tm, row_fn, is_ctx, mods, g_norm, w_ffn1_gu, w_ffn1_down, w_in, g_q, g_k,
                 w_branch, w_o, w_ffn2_gu, w_ffn2_down, attend, rope_tabs):
    t = _ffn_gu(x, mods, g_norm, w_ffn1_gu, l=l, norm_idx=0, k_shift=0, k_scale=1, tm=tm,
                row_fn=row_fn)
    x = _ffn_down(t, w_ffn1_down, x, mods, l=l, k_gate=2, tm=tm, row_fn=row_fn)
    res = _inproj(x, mods, g_norm, g_q, g_k, w_in, rope_tabs, l=l, tm=tm, row_fn=row_fn,
                  is_ctx=is_ctx)
    zq, h2 = res[0], res[1]
    outs = attend(zq)
    mm = _merge(h2, outs, w_in, w_branch, l=l, tm=tm)
    x = _outproj(mm, w_o, x, mods, l=l, tm=tm, row_fn=row_fn)
    t = _ffn_gu(x, mods, g_norm, w_ffn2_gu, l=l, norm_idx=2, k_shift=6, k_scale=7, tm=tm,
                row_fn=row_fn)
    x = _ffn_down(t, w_ffn2_down, x, mods, l=l, k_gate=8, tm=tm, row_fn=row_fn)
    return x, (res[2] if is_ctx else None)


def kernel(x_prompt, x_sample, cache_k_gqa, cache_v_gqa, cache_k_nat, cache_v_nat, cache_k_win, cache_v_win, c, c_ctx, w_mod, b_mod, g_norm, w_ffn1_gu, w_ffn1_down, w_in, g_q, g_k, rpb, sink, w_branch, w_o, w_ffn2_gu, w_ffn2_down, g_final):
    batch, seq, d = x_prompt.shape
    n_b, n_lat, _ = x_sample.shape
    depth = w_mod.shape[0]
    past = cache_k_gqa.shape[2]
    rows = n_lat // GRID_W
    assert n_lat % GRID_W == 0 and rows % NAT_QROWS == 0 and rows >= NAT_KROWS + NAT_QROWS
    assert n_lat % Q_BLOCK == 0 and n_lat >= 3 * Q_BLOCK and n_b + 1 <= 8

    m_ctx = batch * seq
    tm_ctx = _divisor_tile(m_ctx, 1024, 16)
    tm_lat = _divisor_tile(n_lat, 1024, 16)
    per_lat = n_lat // tm_lat

    cond8 = jnp.zeros((8, d), F32).at[0].set(c_ctx).at[1:1 + n_b].set(c)
    mods = _modulation(cond8, w_mod, b_mod)

    sink2 = sink * LOG2E
    rope_tabs = _rope_tables(n_lat)
    ck_gqa = cache_k_gqa.reshape(n_b, depth, past, KV_GQA * HEAD_DIM)
    cv_gqa = cache_v_gqa.reshape(n_b, depth, past, KV_GQA * HEAD_DIM)
    ck_nat = cache_k_nat.reshape(n_b, depth, past, N_HEADS * HEAD_DIM)
    cv_nat = cache_v_nat.reshape(n_b, depth, past, N_HEADS * HEAD_DIM)
    ck_win = cache_k_win.reshape(n_b, depth, past, KV_WIN * HEAD_DIM)
    cv_win = cache_v_win.reshape(n_b, depth, past, KV_WIN * HEAD_DIM)

    weights = dict(g_norm=g_norm, w_ffn1_gu=w_ffn1_gu, w_ffn1_down=w_ffn1_down, w_in=w_in,
                   g_q=g_q, g_k=g_k, w_branch=w_branch, w_o=w_o, w_ffn2_gu=w_ffn2_gu,
                   w_ffn2_down=w_ffn2_down, mods=mods)

    y_p = x_prompt.reshape(m_ctx, d)
    y_s = x_sample.reshape(n_b * n_lat, d)
    kvn = []
    for l in range(depth):
        def attend_ctx(zq, l=l):
            o = _ctx_attention(zq, sink2, l=l, batch=batch, seq=seq)
            return [(o, 0), (o, 1), (o, 2)]

        y_p, kvn_l = _trunk_layer(y_p, l=l, tm=tm_ctx, row_fn=lambda i: 0, is_ctx=True,
                                  attend=attend_ctx, rope_tabs=None, **weights)
        kvn.append(kvn_l)

        nat_bias = _nat_bias_tables(rpb[l], rows)

        def attend_lat(zq, l=l, nat_bias=nat_bias):
            oa = _lat_gqa_attention(zq, ck_gqa, cv_gqa, l=l, n_b=n_b, n_lat=n_lat)
            ob = _nat_attention(zq, ck_nat, cv_nat, nat_bias, l=l, n_b=n_b, n_lat=n_lat)
            oc = _win_attention(zq, ck_win, cv_win, sink2, l=l, n_b=n_b, n_lat=n_lat)
            return [(oa, 0), (ob, 0), (oc, 0)]

        y_s, _ = _trunk_layer(y_s, l=l, tm=tm_lat, row_fn=lambda i: 1 + i // per_lat, is_ctx=False,
                              attend=attend_lat, rope_tabs=rope_tabs, **weights)

    y_prompt = _final_norm(y_p, g_final).reshape(batch, seq, d)
    y_sample = _final_norm(y_s, g_final).reshape(n_b, n_lat, d)

    kv = jnp.stack(kvn, axis=1)
    kv = kv.reshape(batch, seq, depth, KVN_W).transpose(0, 2, 1, 3)

    def part(off, heads):
        return kv[..., off:off + heads * HEAD_DIM].reshape(batch, depth, seq, heads, HEAD_DIM)

    return (y_prompt, y_sample,
            part(0, KV_GQA), part(256, KV_GQA),
            part(512, N_HEADS), part(1536, N_HEADS),
            part(2560, KV_WIN), part(2816, KV_WIN))
```

```python
import functools
import math

import numpy as np
import jax
import jax.numpy as jnp
from jax import lax
from jax.experimental import pallas as pl
from jax.experimental.pallas import tpu as pltpu

F32 = jnp.float32
BF16 = jnp.bfloat16

HEAD_DIM = 128
N_HEADS = 8
KV_GQA = 2
KV_WIN = 2
GRID_W = 64
NAT_ROWS = 8
NAT_COLS = 16
WIN_RADIUS = 128
Q_BLOCK = 128
ROPE_THETA = 10000.0
EPS = 1e-6
NEG_INF = -1e30
LOG2E = math.log2(math.e)
Q_SCALE = (HEAD_DIM ** -0.5) * LOG2E

OFF_QA, OFF_KA, OFF_VA = 0, 1024, 1280
OFF_QB, OFF_KB, OFF_VB = 1536, 2560, 3584
OFF_QC, OFF_KC, OFF_VC = 4608, 5632, 5888
OFF_GATES = 6144
QKV_W = 6144
COL_TILE = 512
KVN_W = 3072
KVN_PARTS = ((0, KV_GQA), (256, KV_GQA), (512, N_HEADS), (1536, N_HEADS),
             (2560, KV_WIN), (2816, KV_WIN))

V7X_VMEM_BYTES = 64 * 1024 * 1024
VMEM_CAP_BYTES = 60 * 1024 * 1024
NAT_QROWS = 4
NAT_KROWS = NAT_ROWS + NAT_QROWS


def _vmem_limit(nbytes):
    return int(min(VMEM_CAP_BYTES, max(32 * 1024 * 1024, nbytes * 3 // 2)))


def _params(sem, nbytes):
    return pltpu.CompilerParams(dimension_semantics=sem, vmem_limit_bytes=_vmem_limit(nbytes))


def _divisor_tile(n, cap, mult):
    if n <= cap:
        return n
    t = (cap // mult) * mult
    while t > mult and n % t:
        t -= mult
    assert n % t == 0, (n, cap, mult)
    return t


def _adaln(x, g, shift, scale):
    ms = jnp.mean(x * x, axis=-1, keepdims=True)
    y = x * lax.rsqrt(ms + EPS)
    return (y * g) * (1.0 + scale) + shift


def _headnorm(y, g):
    ms = jnp.mean(y * y, axis=-1, keepdims=True)
    return (y * lax.rsqrt(ms + EPS)) * g


def _silu(a):
    return a * jax.nn.sigmoid(a)


def _qk_t(q, k):
    return lax.dot_general(q, k, (((1,), (1,)), ((), ())), preferred_element_type=F32)


def _mm(a, b):
    return jnp.dot(a, b, preferred_element_type=F32)


def _mod_kernel(c_ref, w_ref, b_ref, o_ref):
    a = _silu(c_ref[...]).astype(BF16)
    o_ref[...] = _mm(a, w_ref[...].astype(BF16)) + b_ref[...]


def _modulation(cond8, w_mod, b_mod):
    depth, d, n = w_mod.shape
    tn = _divisor_tile(n, 1024, 128)
    nbytes = 2 * d * tn * 4 + d * tn * 2 + 4 * 8 * tn * 4 + 8 * d * 4
    return pl.pallas_call(
        _mod_kernel,
        grid=(depth, n // tn),
        in_specs=[
            pl.BlockSpec((8, d), lambda l, j: (0, 0)),
            pl.BlockSpec((None, d, tn), lambda l, j: (l, 0, j)),
            pl.BlockSpec((None, 1, tn), lambda l, j: (l, 0, j)),
        ],
        out_specs=pl.BlockSpec((None, 8, tn), lambda l, j: (l, 0, j)),
        out_shape=jax.ShapeDtypeStruct((depth, 8, n), F32),
        compiler_params=_params(("parallel", "parallel"), nbytes),
        name="modulation",
    )(cond8, w_mod, b_mod.reshape(depth, 1, n))


def _ffn_kernel(x_ref, g_ref, sh_ref, sc_ref, gt_ref, wgu_hbm, wd_hbm, o_ref,
                h_ref, wg_buf, wu_buf, wd_buf, sem, *, l, row_fn, norm_idx, tf, d_ff):
    n_chunk = d_ff // tf
    d = o_ref.shape[1]
    tn = _divisor_tile(d, 512, 128)

    def copies(c, slot):
        col = c * tf if isinstance(c, int) else pl.multiple_of(c * tf, tf)
        return (
            pltpu.make_async_copy(wgu_hbm.at[l, :, pl.ds(col, tf)], wg_buf.at[slot], sem.at[0, slot]),
            pltpu.make_async_copy(wgu_hbm.at[l, :, pl.ds(d_ff + col, tf)], wu_buf.at[slot],
                                  sem.at[1, slot]),
            pltpu.make_async_copy(wd_hbm.at[l, pl.ds(col, tf), :], wd_buf.at[slot], sem.at[2, slot]),
        )

    def start(c, slot):
        for cp in copies(c, slot):
            cp.start()

    def chunk(c, slot, first):
        for cp in copies(c, slot):
            cp.wait()
        h = h_ref[...]
        a = _mm(h, wg_buf[slot].astype(BF16))
        b = _mm(h, wu_buf[slot].astype(BF16))
        t = (_silu(a) * b).astype(BF16)
        for n in range(d // tn):
            y = _mm(t, wd_buf[slot, :, n * tn:(n + 1) * tn].astype(BF16))
            if first:
                o_ref[:, n * tn:(n + 1) * tn] = y
            else:
                o_ref[:, n * tn:(n + 1) * tn] += y

    start(0, 0)
    r = row_fn(pl.program_id(0))
    h_ref[...] = _adaln(x_ref[...], g_ref[norm_idx:norm_idx + 1, :],
                        sh_ref[pl.ds(r, 1), :], sc_ref[pl.ds(r, 1), :]).astype(BF16)
    start(1, 1)
    chunk(0, 0, True)

    def body(k, carry):
        c = 2 * k + 1
        start(c + 1, 0)
        chunk(c, 1, False)
        start(c + 2, 1)
        chunk(c + 1, 0, False)
        return carry

    lax.fori_loop(0, n_chunk // 2 - 1, body, 0)
    chunk(n_chunk - 1, 1, False)
    o_ref[...] = x_ref[...] + (0.5 * gt_ref[pl.ds(r, 1), :]) * o_ref[...]


def _ffn(x, mods, g_norm, w_gu, w_down, *, l, norm_idx, k_shift, k_scale, k_gate, tm, row_fn):
    m, d = x.shape
    d_ff = w_down.shape[1]
    tf = _divisor_tile(d_ff, 256, 128)
    assert (d_ff // tf) % 2 == 0 and d_ff // tf >= 2
    nbytes = (4 * tm * d * 4 + tm * d * 2 + 6 * d * tf * 4 + 3 * d * tf * 2
              + 3 * tm * tf * 4 + 2 * tm * 512 * 4)
    kern = functools.partial(_ffn_kernel, l=l, row_fn=row_fn, norm_idx=norm_idx, tf=tf, d_ff=d_ff)
    return pl.pallas_call(
        kern,
        grid=(m // tm,),
        in_specs=[
            pl.BlockSpec((tm, d), lambda i: (i, 0)),
            pl.BlockSpec((None, 3, d), lambda i: (l, 0, 0)),
            pl.BlockSpec((None, 8, d), lambda i: (l, 0, k_shift)),
            pl.BlockSpec((None, 8, d), lambda i: (l, 0, k_scale)),
            pl.BlockSpec((None, 8, d), lambda i: (l, 0, k_gate)),
            pl.BlockSpec(memory_space=pl.ANY),
            pl.BlockSpec(memory_space=pl.ANY),
        ],
        out_specs=pl.BlockSpec((tm, d), lambda i: (i, 0)),
        out_shape=jax.ShapeDtypeStruct((m, d), F32),
        scratch_shapes=[
            pltpu.VMEM((tm, d), BF16),
            pltpu.VMEM((2, d, tf), F32),
            pltpu.VMEM((2, d, tf), F32),
            pltpu.VMEM((2, tf, d), F32),
            pltpu.SemaphoreType.DMA((3, 2)),
        ],
        compiler_params=_params(("arbitrary",), nbytes),
        name="ffn",
    )(x, g_norm, mods, mods, mods, w_gu, w_down)


def _kvn_block(j):
    return jnp.where(j <= 2, 0, jnp.where(j >= 9, 5, jnp.clip(j - 4, 1, 4)))


def _inproj_kernel(*refs, is_ctx, row_fn, norm_idx):
    if is_ctx:
        (x_ref, g_ref, sh_ref, sc_ref, gq_ref, gk_ref, w_ref, zq_ref, h_ref, kvn_ref,
         acc_ref) = refs
    else:
        (x_ref, g_ref, sh_ref, sc_ref, gq_ref, gk_ref, cos_ref, sa_ref, sb_ref, w_ref,
         zq_ref, h_ref, acc_ref) = refs
    j = pl.program_id(1)

    def rope(y):
        if is_ctx:
            return y
        return (y * cos_ref[...] + pltpu.roll(y, 96, 1) * sa_ref[...]
                + pltpu.roll(y, 32, 1) * sb_ref[...])

    def cols(hh):
        return slice(hh * HEAD_DIM, (hh + 1) * HEAD_DIM)

    def project(h):
        return _mm(h, w_ref[...].astype(BF16))

    def query_norm_rope(acc):
        for hh in range(4):
            y = _headnorm(acc[:, cols(hh)], gq_ref[...])
            zq_ref[:, cols(hh)] = (rope(y) * Q_SCALE).astype(BF16)

    @pl.when(j == 0)
    def _():
        r = row_fn(pl.program_id(0))
        h = _adaln(x_ref[...], g_ref[norm_idx:norm_idx + 1, :],
                   sh_ref[pl.ds(r, 1), :], sc_ref[pl.ds(r, 1), :]).astype(BF16)
        h_ref[...] = h
        acc_ref[...] = project(h)

    @pl.when((j == 1) | (j == 2))
    def _():
        acc_ref[...] = project(h_ref[...])

    @pl.when(j < 2)
    def _():
        query_norm_rope(acc_ref[...])

    @pl.when(j == 2)
    def _():
        acc = acc_ref[...]
        for hh in range(2):
            y = _headnorm(acc[:, cols(hh)], gk_ref[...])
            if is_ctx:
                kvn_ref[:, cols(hh)] = y
            zq_ref[:, cols(hh)] = rope(y).astype(BF16)
        va = acc[:, 256:512]
        if is_ctx:
            kvn_ref[:, 256:512] = va
        zq_ref[:, 256:512] = va.astype(BF16)

    @pl.when((j == 3) | (j == 4))
    def _():
        zq_ref[...] = (project(h_ref[...]) * Q_SCALE).astype(BF16)

    @pl.when((j >= 5) & (j <= 8))
    def _():
        acc = project(h_ref[...])
        if is_ctx:
            kvn_ref[...] = acc
        zq_ref[...] = acc.astype(BF16)

    @pl.when((j == 9) | (j == 10))
    def _():
        acc = project(h_ref[...])
        for hh in range(4):
            zq_ref[:, cols(hh)] = (rope(acc[:, cols(hh)]) * Q_SCALE).astype(BF16)

    @pl.when(j == 11)
    def _():
        acc = project(h_ref[...])
        for hh in range(2):
            y = acc[:, cols(hh)]
            if is_ctx:
                kvn_ref[:, cols(hh)] = y
            zq_ref[:, cols(hh)] = rope(y).astype(BF16)
        vc = acc[:, 256:512]
        if is_ctx:
            kvn_ref[:, 256:512] = vc
        zq_ref[:, 256:512] = vc.astype(BF16)


def _inproj(x, mods, g_norm, g_q, g_k, w_in, rope_tabs, *, l, tm, row_fn, is_ctx):
    m, d = x.shape
    depth = w_in.shape[0]
    nj = QKV_W // COL_TILE
    in_specs = [
        pl.BlockSpec((tm, d), lambda i, j: (i, 0)),
        pl.BlockSpec((None, 3, d), lambda i, j: (l, 0, 0)),
        pl.BlockSpec((None, 8, d), lambda i, j: (l, 0, 3)),
        pl.BlockSpec((None, 8, d), lambda i, j: (l, 0, 4)),
        pl.BlockSpec((None, 1, HEAD_DIM), lambda i, j: (l, 0, 0)),
        pl.BlockSpec((None, 1, HEAD_DIM), lambda i, j: (l, 0, 0)),
    ]
    args = [x, g_norm, mods, mods, g_q.reshape(depth, 1, HEAD_DIM), g_k.reshape(depth, 1, HEAD_DIM)]
    if not is_ctx:
        n_pos = rope_tabs[0].shape[0]
        per = n_pos // tm
        for tab in rope_tabs:
            in_specs.append(pl.BlockSpec((tm, HEAD_DIM), lambda i, j: (i % per, 0)))
            args.append(tab)
    in_specs.append(pl.BlockSpec((None, d, COL_TILE), lambda i, j: (l, 0, j)))
    args.append(w_in)
    out_specs = [
        pl.BlockSpec((tm, COL_TILE), lambda i, j: (i, j)),
        pl.BlockSpec((tm, d), lambda i, j: (i, 0)),
    ]
    out_shape = [jax.ShapeDtypeStruct((m, QKV_W), BF16), jax.ShapeDtypeStruct((m, d), BF16)]
    if is_ctx:
        out_specs.append(pl.BlockSpec((tm, COL_TILE), lambda i, j: (i, _kvn_block(j))))
        out_shape.append(jax.ShapeDtypeStruct((m, KVN_W), F32))
    nbytes = (2 * tm * d * 4 + 2 * tm * d * 2 + 2 * d * COL_TILE * 4 + d * COL_TILE * 2
              + 3 * tm * COL_TILE * 4 + 2 * tm * COL_TILE * 2 + 6 * tm * HEAD_DIM * 4)
    kern = functools.partial(_inproj_kernel, is_ctx=is_ctx, row_fn=row_fn, norm_idx=1)
    return pl.pallas_call(
        kern,
        grid=(m // tm, nj),
        in_specs=in_specs,
        out_specs=out_specs,
        out_shape=out_shape,
        scratch_shapes=[pltpu.VMEM((tm, COL_TILE), F32)],
        compiler_params=_params(("parallel", "arbitrary"), nbytes),
        name="inproj_ctx" if is_ctx else "inproj_lat",
    )(*args)


def _ctx_attn_kernel(z_ref, sink_ref, o_ref, *, l, seq):
    def attend(q_off, n_q, k_off, v_off, sinks):
        q = jnp.concatenate(
            [z_ref[:, q_off + h * HEAD_DIM:q_off + (h + 1) * HEAD_DIM] for h in range(n_q)], axis=0)
        k = z_ref[:, k_off:k_off + HEAD_DIM]
        v = z_ref[:, v_off:v_off + HEAD_DIM]
        s = _qk_t(q, k)
        m = jnp.max(s, axis=-1, keepdims=True)
        if sinks is not None:
            sink_col = jnp.concatenate(
                [jnp.full((seq, 1), sk, F32) for sk in sinks], axis=0)
            m = jnp.maximum(m, sink_col)
        p = jnp.exp2(s - m)
        den = jnp.sum(p, axis=-1, keepdims=True)
        if sinks is not None:
            den = den + jnp.exp2(sink_col - m)
        return _mm(p.astype(BF16), v) / den

    def store(o, o_off, n_q):
        for h in range(n_q):
            o_ref[:, o_off + h * HEAD_DIM:o_off + (h + 1) * HEAD_DIM] = (
                o[h * seq:(h + 1) * seq].astype(BF16))

    grp = N_HEADS // KV_GQA
    for g in range(KV_GQA):
        o = attend(OFF_QA + g * grp * HEAD_DIM, grp, OFF_KA + g * HEAD_DIM,
                   OFF_VA + g * HEAD_DIM, None)
        store(o, g * grp * HEAD_DIM, grp)
    for h in range(N_HEADS):
        o = attend(OFF_QB + h * HEAD_DIM, 1, OFF_KB + h * HEAD_DIM, OFF_VB + h * HEAD_DIM, None)
        store(o, 1024 + h * HEAD_DIM, 1)
    grp = N_HEADS // KV_WIN
    for g in range(KV_WIN):
        sinks = [sink_ref[l, g * grp + h] for h in range(grp)]
        o = attend(OFF_QC + g * grp * HEAD_DIM, grp, OFF_KC + g * HEAD_DIM,
                   OFF_VC + g * HEAD_DIM, sinks)
        store(o, 2048 + g * grp * HEAD_DIM, grp)


def _ctx_attention(zq, sink2, *, l, batch, seq):
    kern = functools.partial(_ctx_attn_kernel, l=l, seq=seq)
    nbytes = 2 * seq * QKV_W * 2 + 2 * seq * 3072 * 2 + 8 * 4 * seq * seq * 4
    return pl.pallas_call(
        kern,
        grid=(batch,),
        in_specs=[
            pl.BlockSpec((seq, QKV_W), lambda b: (b, 0)),
            pl.BlockSpec(memory_space=pltpu.SMEM),
        ],
        out_specs=pl.BlockSpec((seq, 3072), lambda b: (b, 0)),
        out_shape=jax.ShapeDtypeStruct((batch * seq, 3072), BF16),
        compiler_params=_params(("parallel",), nbytes),
        name="attn_ctx",
    )(zq, sink2)


def _lat_gqa_kernel(q_ref, kl_ref, vl_ref, kc_ref, vc_ref, o_ref, k_sc, v_sc, s0_sc, s1_sc,
                    m0_sc, m1_sc, *, past, n_lat, tq, chunk):
    grp = N_HEADS // KV_GQA
    total = past + n_lat
    n_chunk = total // chunk
    nq = n_lat // tq
    k_sc[0:past, :] = kc_ref[...].astype(BF16)
    k_sc[past:total, :] = kl_ref[...]
    v_sc[0:past, 0:HEAD_DIM] = vc_ref[...].astype(BF16)
    v_sc[past:total, 0:HEAD_DIM] = vl_ref[...]
    v_sc[:, HEAD_DIM:2 * HEAD_DIM] = jnp.ones((total, HEAD_DIM), BF16)

    def row0(t):
        return t * tq if isinstance(t, int) else pl.multiple_of(t * tq, tq)

    def scores(t, s_sc, m_sc):
        qn = q_ref[pl.ds(row0(t), tq), :]
        q = jnp.concatenate([qn[:, h * HEAD_DIM:(h + 1) * HEAD_DIM] for h in range(grp)], axis=0)
        m = None
        for c in range(n_chunk):
            s = _qk_t(q, k_sc[c * chunk:(c + 1) * chunk, :])
            s_sc[:, c * chunk:(c + 1) * chunk] = s
            mc = jnp.max(s, axis=-1, keepdims=True)
            m = mc if m is None else jnp.maximum(m, mc)
        m_sc[...] = m

    def values(t, s_sc, m_sc):
        m = m_sc[...]
        acc = None
        for c in range(n_chunk):
            p = jnp.exp2(s_sc[:, c * chunk:(c + 1) * chunk] - m).astype(BF16)
            term = _mm(p, v_sc[c * chunk:(c + 1) * chunk, :])
            acc = term if acc is None else acc + term
        o = acc[:, 0:HEAD_DIM] / acc[:, HEAD_DIM:2 * HEAD_DIM]
        for h in range(grp):
            o_ref[pl.ds(row0(t), tq), h * HEAD_DIM:(h + 1) * HEAD_DIM] = (
                o[h * tq:(h + 1) * tq].astype(BF16))

    scores(0, s0_sc, m0_sc)

    def body(k, carry):
        t = 2 * k
        scores(t + 1, s1_sc, m1_sc)
        values(t, s0_sc, m0_sc)
        scores(t + 2, s0_sc, m0_sc)
        values(t + 1, s1_sc, m1_sc)
        return carry

    lax.fori_loop(0, nq // 2 - 1, body, 0)
    scores(nq - 1, s1_sc, m1_sc)
    values(nq - 2, s0_sc, m0_sc)
    values(nq - 1, s1_sc, m1_sc)


def _lat_gqa_attention(zq, cache_k, cache_v, *, l, n_b, n_lat):
    past = cache_k.shape[2]
    grp = N_HEADS // KV_GQA
    tq = _divisor_tile(n_lat, 128, 16)
    assert (n_lat // tq) % 2 == 0
    total = past + n_lat
    chunk = _divisor_tile(total, 1536, 128)
    wq = grp * HEAD_DIM
    ka_blk = OFF_KA // HEAD_DIM
    va_blk = OFF_VA // HEAD_DIM
    kern = functools.partial(_lat_gqa_kernel, past=past, n_lat=n_lat, tq=tq, chunk=chunk)
    nbytes = (4 * n_lat * wq * 2 + 4 * n_lat * HEAD_DIM * 2 + 4 * past * HEAD_DIM * 4
              + total * HEAD_DIM * 2 * 3 + 2 * grp * tq * total * 4 + 4 * grp * tq * chunk * 4)
    return pl.pallas_call(
        kern,
        grid=(n_b, KV_GQA),
        in_specs=[
            pl.BlockSpec((n_lat, wq), lambda b, g: (b, g)),
            pl.BlockSpec((n_lat, HEAD_DIM), lambda b, g: (b, ka_blk + g)),
            pl.BlockSpec((n_lat, HEAD_DIM), lambda b, g: (b, va_blk + g)),
            pl.BlockSpec((None, None, past, HEAD_DIM), lambda b, g: (b, l, 0, g)),
            pl.BlockSpec((None, None, past, HEAD_DIM), lambda b, g: (b, l, 0, g)),
        ],
        out_specs=pl.BlockSpec((n_lat, wq), lambda b, g: (b, g)),
        out_shape=jax.ShapeDtypeStruct((n_b * n_lat, N_HEADS * HEAD_DIM), BF16),
        scratch_shapes=[
            pltpu.VMEM((total, HEAD_DIM), BF16),
            pltpu.VMEM((total, 2 * HEAD_DIM), BF16),
            pltpu.VMEM((grp * tq, total), F32),
            pltpu.VMEM((grp * tq, total), F32),
            pltpu.VMEM((grp * tq, 1), F32),
            pltpu.VMEM((grp * tq, 1), F32),
        ],
        compiler_params=_params(("parallel", "parallel"), nbytes),
        name="attn_lat_global",
    )(zq, zq, zq, cache_k, cache_v)


def _nat_row_tables(rpb):
    w, wc = GRID_W, NAT_COLS
    c = np.arange(w)
    cs = np.clip(c - wc // 2, 0, w - wc)
    col_ok = (c[None, :] >= cs[:, None]) & (c[None, :] < cs[:, None] + wc)
    dc = np.clip(c[None, :] - c[:, None] + wc - 1, 0, 2 * wc - 2)
    t_tab = jnp.zeros(rpb.shape[:3] + (w, w), F32)
    for d in range(2 * wc - 1):
        t_tab = jnp.where(jnp.asarray(dc == d), rpb[..., d][..., None, None].astype(F32), t_tab)
    return jnp.where(jnp.asarray(col_ok), t_tab, NEG_INF) * LOG2E


def _nat_tile_plan(rows):
    wr = NAT_ROWS
    plan = []
    for r0 in (0, NAT_QROWS, rows - NAT_QROWS):
        ws = int(np.clip(r0 - wr // 2, 0, rows - NAT_KROWS))
        cls = []
        for a in range(NAT_QROWS):
            qr = r0 + a
            rs = int(np.clip(qr - wr // 2, 0, rows - wr))
            cls.append([(ws + b) - qr + wr - 1 if rs <= ws + b < rs + wr else None
                        for b in range(NAT_KROWS)])
        plan.append(cls)
    return plan


def _nat_kernel(q_ref, k_ref, v_ref, kc_ref, vc_ref, t_ref, o_ref, kc_sc, vc_sc, bias_ref,
                *, rows):
    tq = NAT_QROWS * GRID_W
    tk = NAT_KROWS * GRID_W
    n_tiles = rows // NAT_QROWS
    kc_sc[...] = kc_ref[...].astype(BF16)
    vc_sc[...] = vc_ref[...].astype(BF16)
    neg_blk = jnp.full((GRID_W, GRID_W), NEG_INF * LOG2E, F32)
    for cls, cls_plan in enumerate(_nat_tile_plan(rows)):
        for a, row_plan in enumerate(cls_plan):
            for b, dr in enumerate(row_plan):
                bias_ref[cls, a * GRID_W:(a + 1) * GRID_W, b * GRID_W:(b + 1) * GRID_W] = (
                    neg_blk if dr is None else t_ref[dr])

    def body(t, carry):
        r0 = t * NAT_QROWS
        ws = jnp.clip(r0 - NAT_ROWS // 2, 0, rows - NAT_KROWS)
        cls = jnp.where(t == 0, 0, jnp.where(t == n_tiles - 1, 2, 1))
        q = q_ref[pl.ds(pl.multiple_of(t * tq, tq), tq), :]
        k0 = pl.multiple_of(ws * GRID_W, GRID_W)
        s_loc = _qk_t(q, k_ref[pl.ds(k0, tk), :]) + bias_ref[cls]
        s_ctx = _qk_t(q, kc_sc[...])
        m = jnp.maximum(jnp.max(s_loc, axis=-1, keepdims=True),
                        jnp.max(s_ctx, axis=-1, keepdims=True))
        p_loc = jnp.exp2(s_loc - m)
        p_ctx = jnp.exp2(s_ctx - m)
        den = jnp.sum(p_loc, axis=-1, keepdims=True) + jnp.sum(p_ctx, axis=-1, keepdims=True)
        o = _mm(p_ctx.astype(BF16), vc_sc[...]) + _mm(p_loc.astype(BF16), v_ref[pl.ds(k0, tk), :])
        o_ref[pl.ds(pl.multiple_of(t * tq, tq), tq), :] = (o / den).astype(BF16)
        return carry

    lax.fori_loop(0, n_tiles, body, 0, unroll=2)


def _nat_attention(zq, cache_k, cache_v, row_tabs, *, l, n_b, n_lat):
    past = cache_k.shape[2]
    rows = n_lat // GRID_W
    tq = NAT_QROWS * GRID_W
    tk = NAT_KROWS * GRID_W
    n_dr = row_tabs.shape[2]
    qb, kb, vb = OFF_QB // HEAD_DIM, OFF_KB // HEAD_DIM, OFF_VB // HEAD_DIM
    kern = functools.partial(_nat_kernel, rows=rows)
    nbytes = (8 * n_lat * HEAD_DIM * 2 + 4 * past * HEAD_DIM * 4 + 2 * past * HEAD_DIM * 2
              + 2 * n_dr * GRID_W * 128 * 4 + 3 * tq * tk * 4 + 6 * tq * (tk + past) * 4)
    return pl.pallas_call(
        kern,
        grid=(n_b, N_HEADS),
        in_specs=[
            pl.BlockSpec((n_lat, HEAD_DIM), lambda b, h: (b, qb + h)),
            pl.BlockSpec((n_lat, HEAD_DIM), lambda b, h: (b, kb + h)),
            pl.BlockSpec((n_lat, HEAD_DIM), lambda b, h: (b, vb + h)),
            pl.BlockSpec((None, None, past, HEAD_DIM), lambda b, h: (b, l, 0, h)),
            pl.BlockSpec((None, None, past, HEAD_DIM), lambda b, h: (b, l, 0, h)),
            pl.BlockSpec((None, None, n_dr, GRID_W, GRID_W), lambda b, h: (l, h, 0, 0, 0)),
        ],
        out_specs=pl.BlockSpec((n_lat, HEAD_DIM), lambda b, h: (b, h)),
        out_shape=jax.ShapeDtypeStruct((n_b * n_lat, N_HEADS * HEAD_DIM), BF16),
        scratch_shapes=[pltpu.VMEM((past, HEAD_DIM), BF16), pltpu.VMEM((past, HEAD_DIM), BF16),
                        pltpu.VMEM((3, tq, tk), F32)],
        compiler_params=_params(("parallel", "parallel"), nbytes),
        name="attn_lat_nat",
    )(zq, zq, zq, cache_k, cache_v, row_tabs)


def _win_kernel(q_ref, k_ref, v_ref, kc_ref, vc_ref, sink_ref, o_ref, kc_sc, vc_sc, bias_sc,
                *, l, past, n_lat):
    grp = N_HEADS // KV_WIN
    band = 3 * Q_BLOCK
    n_blk = n_lat // Q_BLOCK
    g = pl.program_id(1)
    kc_sc[...] = kc_ref[...].astype(BF16)
    vc_sc[...] = vc_ref[...].astype(BF16)
    qi = lax.broadcasted_iota(jnp.int32, (Q_BLOCK, band), 0)
    kj = lax.broadcasted_iota(jnp.int32, (Q_BLOCK, band), 1)
    for cls, off in enumerate((0, -Q_BLOCK, -2 * Q_BLOCK)):
        bias_sc[cls] = jnp.where(jnp.abs(kj + off - qi) <= WIN_RADIUS, 0.0, NEG_INF).astype(F32)

    def body(n, carry):
        start = pl.multiple_of(jnp.clip((n - 1) * Q_BLOCK, 0, n_lat - band), Q_BLOCK)
        cls = jnp.where(n == 0, 0, jnp.where(n == n_blk - 1, 2, 1))
        q0 = pl.multiple_of(n * Q_BLOCK, Q_BLOCK)
        qn = q_ref[pl.ds(q0, Q_BLOCK), :]
        q = jnp.concatenate([qn[:, h * HEAD_DIM:(h + 1) * HEAD_DIM] for h in range(grp)], axis=0)
        s_ctx_all = _qk_t(q, kc_sc[...])
        s_loc_all = _qk_t(q, k_ref[pl.ds(start, band), :])
        bias = bias_sc[cls]
        p_ctx, p_loc, dens = [], [], []
        for h in range(grp):
            s_ctx = s_ctx_all[h * Q_BLOCK:(h + 1) * Q_BLOCK]
            s_loc = s_loc_all[h * Q_BLOCK:(h + 1) * Q_BLOCK] + bias
            sk = sink_ref[l, g * grp + h]
            m = jnp.maximum(jnp.maximum(jnp.max(s_ctx, axis=-1, keepdims=True),
                                        jnp.max(s_loc, axis=-1, keepdims=True)), sk)
            pc = jnp.exp2(s_ctx - m)
            pw = jnp.exp2(s_loc - m)
            dens.append(jnp.sum(pc, axis=-1, keepdims=True) + jnp.sum(pw, axis=-1, keepdims=True)
                        + jnp.exp2(sk - m))
            p_ctx.append(pc.astype(BF16))
            p_loc.append(pw.astype(BF16))
        acc = (_mm(jnp.concatenate(p_ctx, axis=0), vc_sc[...])
               + _mm(jnp.concatenate(p_loc, axis=0), v_ref[pl.ds(start, band), :]))
        for h in range(grp):
            o = acc[h * Q_BLOCK:(h + 1) * Q_BLOCK] / dens[h]
            o_ref[pl.ds(q0, Q_BLOCK), h * HEAD_DIM:(h + 1) * HEAD_DIM] = o.astype(BF16)
        return carry

    lax.fori_loop(0, n_blk, body, 0, unroll=2)


def _win_attention(zq, cache_k, cache_v, sink2, *, l, n_b, n_lat):
    past = cache_k.shape[2]
    grp = N_HEADS // KV_WIN
    band = 3 * Q_BLOCK
    wq = grp * HEAD_DIM
    qc_blk, kc_blk, vc_blk = OFF_QC // wq, OFF_KC // HEAD_DIM, OFF_VC // HEAD_DIM
    kern = functools.partial(_win_kernel, l=l, past=past, n_lat=n_lat)
    nbytes = (4 * n_lat * wq * 2 + 4 * n_lat * HEAD_DIM * 2 + 4 * past * HEAD_DIM * 4
              + (past + band) * HEAD_DIM * 2 * 3 + 3 * Q_BLOCK * band * 4
              + 4 * grp * Q_BLOCK * (past + band) * 4)
    return pl.pallas_call(
        kern,
        grid=(n_b, KV_WIN),
        in_specs=[
            pl.BlockSpec((n_lat, wq), lambda b, g: (b, qc_blk + g)),
            pl.BlockSpec((n_lat, HEAD_DIM), lambda b, g: (b, kc_blk + g)),
            pl.BlockSpec((n_lat, HEAD_DIM), lambda b, g: (b, vc_blk + g)),
            pl.BlockSpec((None, None, past, HEAD_DIM), lambda b, g: (b, l, 0, g)),
            pl.BlockSpec((None, None, past, HEAD_DIM), lambda b, g: (b, l, 0, g)),
            pl.BlockSpec(memory_space=pltpu.SMEM),
        ],
        out_specs=pl.BlockSpec((n_lat, wq), lambda b, g: (b, g)),
        out_shape=jax.ShapeDtypeStruct((n_b * n_lat, N_HEADS * HEAD_DIM), BF16),
        scratch_shapes=[
            pltpu.VMEM((past, HEAD_DIM), BF16),
            pltpu.VMEM((past, HEAD_DIM), BF16),
            pltpu.VMEM((3, Q_BLOCK, band), F32),
        ],
        compiler_params=_params(("parallel", "parallel"), nbytes),
        name="attn_lat_window",
    )(zq, zq, zq, cache_k, cache_v, sink2)


def _merge_kernel(h_ref, oa_ref, ob_ref, oc_ref, wga_ref, wgb_ref, wgc_ref,
                  wba_ref, wbb_ref, wbc_ref, m_ref):
    h = h_ref[...]
    acc = None
    for o_ref, wg_ref, wb_ref in ((oa_ref, wga_ref, wba_ref), (ob_ref, wgb_ref, wbb_ref),
                                  (oc_ref, wgc_ref, wbc_ref)):
        gate = jax.nn.sigmoid(_mm(h, wg_ref[...].astype(BF16)))
        term = gate * _mm(o_ref[...], wb_ref[...].astype(BF16))
        acc = term if acc is None else acc + term
    m_ref[...] = acc.astype(BF16)


def _merge(h, outs, w_in, w_branch, *, l, tm):
    m, d = h.shape
    bw = w_branch.shape[2]
    tn = _divisor_tile(d, 256, 128)
    g0 = OFF_GATES // tn
    per = d // tn

    def o_spec(width, first):
        return pl.BlockSpec((tm, bw), lambda i, j: (i, first))

    in_specs = [pl.BlockSpec((tm, d), lambda i, j: (i, 0))]
    args = [h]
    for o, first in outs:
        in_specs.append(o_spec(bw, first))
        args.append(o)
    for k in range(3):
        in_specs.append(pl.BlockSpec((None, d, tn), lambda i, j, k=k: (l, 0, g0 + k * per + j)))
        args.append(w_in)
    for k in range(3):
        in_specs.append(pl.BlockSpec((None, None, bw, tn), lambda i, j, k=k: (l, k, 0, j)))
        args.append(w_branch)
    nbytes = (2 * tm * d * 2 + 6 * tm * bw * 2 + 6 * d * tn * 4 + 6 * bw * tn * 4
              + 3 * (d + bw) * tn * 2 + 6 * tm * tn * 4)
    return pl.pallas_call(
        _merge_kernel,
        grid=(m // tm, d // tn),
        in_specs=in_specs,
        out_specs=pl.BlockSpec((tm, tn), lambda i, j: (i, j)),
        out_shape=jax.ShapeDtypeStruct((m, d), BF16),
        compiler_params=_params(("parallel", "parallel"), nbytes),
        name="branch_merge",
    )(*args)


def _outproj_kernel(m_ref, w_ref, x_ref, gt_ref, o_ref, *, row_fn):
    r = row_fn(pl.program_id(0))
    y = _mm(m_ref[...], w_ref[...].astype(BF16))
    o_ref[...] = x_ref[...] + gt_ref[pl.ds(r, 1), :] * y


def _outproj(mm, w_o, x, mods, *, l, tm, row_fn):
    m, d = x.shape
    tn = _divisor_tile(d, 512, 128)
    kern = functools.partial(_outproj_kernel, row_fn=row_fn)
    nbytes = 2 * tm * d * 2 + 2 * d * tn * 4 + d * tn * 2 + 5 * tm * tn * 4
    return pl.pallas_call(
        kern,
        grid=(m // tm, d // tn),
        in_specs=[
            pl.BlockSpec((tm, d), lambda i, j: (i, 0)),
            pl.BlockSpec((None, d, tn), lambda i, j: (l, 0, j)),
            pl.BlockSpec((tm, tn), lambda i, j: (i, j)),
            pl.BlockSpec((None, 8, tn), lambda i, j: (l, 0, 5 * (d // tn) + j)),
        ],
        out_specs=pl.BlockSpec((tm, tn), lambda i, j: (i, j)),
        out_shape=jax.ShapeDtypeStruct((m, d), F32),
        compiler_params=_params(("parallel", "parallel"), nbytes),
        name="out_proj",
    )(mm, w_o, x, mods)


def _final_norm_kernel(x_ref, g_ref, o_ref):
    x = x_ref[...]
    ms = jnp.mean(x * x, axis=-1, keepdims=True)
    o_ref[...] = (x * lax.rsqrt(ms + EPS)) * g_ref[...]


def _final_norm(x, g_final):
    m, d = x.shape
    tm = _divisor_tile(m, 512, 8)
    return pl.pallas_call(
        _final_norm_kernel,
        grid=(m // tm,),
        in_specs=[pl.BlockSpec((tm, d), lambda i: (i, 0)), pl.BlockSpec((1, d), lambda i: (0, 0))],
        out_specs=pl.BlockSpec((tm, d), lambda i: (i, 0)),
        out_shape=jax.ShapeDtypeStruct((m, d), F32),
        compiler_params=_params(("parallel",), 4 * tm * d * 4),
        name="final_norm",
    )(x, g_final.reshape(1, d))


def _pack_kv_kernel(*refs, seq):
    n_in = len(refs) - len(KVN_PARTS)
    for l in range(n_in):
        for out_ref, (off, heads) in zip(refs[n_in:], KVN_PARTS):
            for h in range(heads):
                c0 = off + h * HEAD_DIM
                out_ref[l, pl.ds(h, seq, stride=heads), :] = refs[l][:, c0:c0 + HEAD_DIM]


def _pack_context_kv(kvn, *, batch, seq):
    depth = len(kvn)
    kern = functools.partial(_pack_kv_kernel, seq=seq)
    nbytes = 2 * depth * seq * KVN_W * 4 * 2
    return pl.pallas_call(
        kern,
        grid=(batch,),
        in_specs=[pl.BlockSpec((seq, KVN_W), lambda b: (b, 0)) for _ in range(depth)],
        out_specs=[pl.BlockSpec((None, depth, seq * heads, HEAD_DIM), lambda b: (b, 0, 0, 0))
                   for _, heads in KVN_PARTS],
        out_shape=[jax.ShapeDtypeStruct((batch, depth, seq * heads, HEAD_DIM), F32)
                   for _, heads in KVN_PARTS],
        compiler_params=_params(("parallel",), nbytes),
        name="pack_context_kv",
    )(*kvn)


def _rope_tables(n_lat):
    half = HEAD_DIM // 2
    nf = half // 2
    t = jnp.arange(n_lat)
    row = (t // GRID_W).astype(F32)
    col = (t % GRID_W).astype(F32)
    inv = 1.0 / (ROPE_THETA ** (jnp.arange(nf, dtype=F32) / nf))
    ang_r = row[:, None] * inv[None, :]
    ang_c = col[:, None] * inv[None, :]
    cr, sr, cc, sc = jnp.cos(ang_r), jnp.sin(ang_r), jnp.cos(ang_c), jnp.sin(ang_c)
    zero = jnp.zeros_like(sr)
    cos = jnp.concatenate([cr, cr, cc, cc], axis=-1)
    sin_a = jnp.concatenate([-sr, zero, -sc, zero], axis=-1)
    sin_b = jnp.concatenate([zero, sr, zero, sc], axis=-1)
    return cos, sin_a, sin_b


def _trunk_layer(x, *, l, tm, row_fn, is_ctx, mods, g_norm, w_ffn1_gu, w_ffn1_down, w_in, g_q, g_k,
                 w_branch, w_o, w_ffn2_gu, w_ffn2_down, attend, rope_tabs):
    x = _ffn(x, mods, g_norm, w_ffn1_gu, w_ffn1_down, l=l, norm_idx=0, k_shift=0, k_scale=1,
             k_gate=2, tm=tm, row_fn=row_fn)
    res = _inproj(x, mods, g_norm, g_q, g_k, w_in, rope_tabs, l=l, tm=tm, row_fn=row_fn,
                  is_ctx=is_ctx)
    zq, h2 = res[0], res[1]
    outs = attend(zq)
    mm = _merge(h2, outs, w_in, w_branch, l=l, tm=tm)
    x = _outproj(mm, w_o, x, mods, l=l, tm=tm, row_fn=row_fn)
    x = _ffn(x, mods, g_norm, w_ffn2_gu, w_ffn2_down, l=l, norm_idx=2, k_shift=6, k_scale=7,
             k_gate=8, tm=tm, row_fn=row_fn)
    return x, (res[2] if is_ctx else None)


def kernel(x_prompt, x_sample, cache_k_gqa, cache_v_gqa, cache_k_nat, cache_v_nat, cache_k_win, cache_v_win, c, c_ctx, w_mod, b_mod, g_norm, w_ffn1_gu, w_ffn1_down, w_in, g_q, g_k, rpb, sink, w_branch, w_o, w_ffn2_gu, w_ffn2_down, g_final):
    batch, seq, d = x_prompt.shape
    n_b, n_lat, _ = x_sample.shape
    depth = w_mod.shape[0]
    past = cache_k_gqa.shape[2]
    rows = n_lat // GRID_W
    assert n_lat % GRID_W == 0 and rows % NAT_QROWS == 0 and rows >= NAT_KROWS + NAT_QROWS
    assert n_lat % Q_BLOCK == 0 and n_lat >= 3 * Q_BLOCK and n_b + 1 <= 8

    m_ctx = batch * seq
    tm_ctx = _divisor_tile(m_ctx, 1024, 16)
    tm_lat = _divisor_tile(n_lat, 1024, 16)
    per_lat = n_lat // tm_lat

    cond8 = jnp.zeros((8, d), F32).at[0].set(c_ctx).at[1:1 + n_b].set(c)
    mods = _modulation(cond8, w_mod, b_mod)

    sink2 = sink * LOG2E
    rope_tabs = _rope_tables(n_lat)
    nat_tabs = _nat_row_tables(rpb)
    ck_gqa = cache_k_gqa.reshape(n_b, depth, past, KV_GQA * HEAD_DIM)
    cv_gqa = cache_v_gqa.reshape(n_b, depth, past, KV_GQA * HEAD_DIM)
    ck_nat = cache_k_nat.reshape(n_b, depth, past, N_HEADS * HEAD_DIM)
    cv_nat = cache_v_nat.reshape(n_b, depth, past, N_HEADS * HEAD_DIM)
    ck_win = cache_k_win.reshape(n_b, depth, past, KV_WIN * HEAD_DIM)
    cv_win = cache_v_win.reshape(n_b, depth, past, KV_WIN * HEAD_DIM)

    weights = dict(g_norm=g_norm, w_ffn1_gu=w_ffn1_gu, w_ffn1_down=w_ffn1_down, w_in=w_in,
                   g_q=g_q, g_k=g_k, w_branch=w_branch, w_o=w_o, w_ffn2_gu=w_ffn2_gu,
                   w_ffn2_down=w_ffn2_down, mods=mods)

    y_p = x_prompt.reshape(m_ctx, d)
    y_s = x_sample.reshape(n_b * n_lat, d)
    kvn = []
    for l in range(depth):
        def attend_ctx(zq, l=l):
            o = _ctx_attention(zq, sink2, l=l, batch=batch, seq=seq)
            return [(o, 0), (o, 1), (o, 2)]

        y_p, kvn_l = _trunk_layer(y_p, l=l, tm=tm_ctx, row_fn=lambda i: 0, is_ctx=True,
                                  attend=attend_ctx, rope_tabs=None, **weights)
        kvn.append(kvn_l)

        def attend_lat(zq, l=l):
            oa = _lat_gqa_attention(zq, ck_gqa, cv_gqa, l=l, n_b=n_b, n_lat=n_lat)
            ob = _nat_attention(zq, ck_nat, cv_nat, nat_tabs, l=l, n_b=n_b, n_lat=n_lat)
            oc = _win_attention(zq, ck_win, cv_win, sink2, l=l, n_b=n_b, n_lat=n_lat)
            return [(oa, 0), (ob, 0), (oc, 0)]

        y_s, _ = _trunk_layer(y_s, l=l, tm=tm_lat, row_fn=lambda i: 1 + i // per_lat, is_ctx=False,
                              attend=attend_lat, rope_tabs=rope_tabs, **weights)

    y_prompt = _final_norm(y_p, g_final).reshape(batch, seq, d)
    y_sample = _final_norm(y_s, g_final).reshape(n_b, n_lat, d)

    new_kv = _pack_context_kv(kvn, batch=batch, seq=seq)
    return (y_prompt, y_sample) + tuple(
        a.reshape(batch, depth, seq, heads, HEAD_DIM) for a, (_, heads) in zip(new_kv, KVN_PARTS))
```

```python
import functools
import math

import numpy as np
import jax
import jax.numpy as jnp
from jax import lax
from jax.experimental import pallas as pl
from jax.experimental.pallas import tpu as pltpu

F32 = jnp.float32
BF16 = jnp.bfloat16

HEAD_DIM = 128
N_HEADS = 8
KV_GQA = 2
KV_WIN = 2
GRID_W = 64
NAT_ROWS = 8
NAT_COLS = 16
WIN_RADIUS = 128
Q_BLOCK = 128
ROPE_THETA = 10000.0
EPS = 1e-6
NEG_INF = -1e30
LOG2E = math.log2(math.e)
Q_SCALE = (HEAD_DIM ** -0.5) * LOG2E

OFF_QA, OFF_KA, OFF_VA = 0, 1024, 1280
OFF_QB, OFF_KB, OFF_VB = 1536, 2560, 3584
OFF_QC, OFF_KC, OFF_VC = 4608, 5632, 5888
OFF_GATES = 6144
QKV_W = 6144
COL_TILE = 512
KVN_W = 3072
KVN_PARTS = ((0, KV_GQA), (256, KV_GQA), (512, N_HEADS), (1536, N_HEADS),
             (2560, KV_WIN), (2816, KV_WIN))

V7X_VMEM_BYTES = 64 * 1024 * 1024
VMEM_CAP_BYTES = 60 * 1024 * 1024
NAT_QROWS = 4
NAT_KROWS = NAT_ROWS + NAT_QROWS


def _vmem_limit(nbytes):
    return int(min(VMEM_CAP_BYTES, max(32 * 1024 * 1024, nbytes * 3 // 2)))


def _params(sem, nbytes):
    return pltpu.CompilerParams(dimension_semantics=sem, vmem_limit_bytes=_vmem_limit(nbytes))


def _divisor_tile(n, cap, mult):
    if n <= cap:
        return n
    t = (cap // mult) * mult
    while t > mult and n % t:
        t -= mult
    assert n % t == 0, (n, cap, mult)
    return t


def _adaln(x, g, shift, scale):
    ms = jnp.mean(x * x, axis=-1, keepdims=True)
    y = x * lax.rsqrt(ms + EPS)
    return (y * g) * (1.0 + scale) + shift


def _headnorm(y, g):
    ms = jnp.mean(y * y, axis=-1, keepdims=True)
    return (y * lax.rsqrt(ms + EPS)) * g


def _silu(a):
    return a * jax.nn.sigmoid(a)


def _qk_t(q, k):
    return lax.dot_general(q, k, (((1,), (1,)), ((), ())), preferred_element_type=F32)


def _mm(a, b):
    return jnp.dot(a, b, preferred_element_type=F32)


def _mod_kernel(c_ref, w_ref, b_ref, o_ref):
    a = _silu(c_ref[...]).astype(BF16)
    o_ref[...] = _mm(a, w_ref[...].astype(BF16)) + b_ref[...]


def _modulation(cond8, w_mod, b_mod):
    depth, d, n = w_mod.shape
    tn = _divisor_tile(n, 1024, 128)
    nbytes = 2 * d * tn * 4 + d * tn * 2 + 4 * 8 * tn * 4 + 8 * d * 4
    return pl.pallas_call(
        _mod_kernel,
        grid=(depth, n // tn),
        in_specs=[
            pl.BlockSpec((8, d), lambda l, j: (0, 0)),
            pl.BlockSpec((None, d, tn), lambda l, j: (l, 0, j)),
            pl.BlockSpec((None, 1, tn), lambda l, j: (l, 0, j)),
        ],
        out_specs=pl.BlockSpec((None, 8, tn), lambda l, j: (l, 0, j)),
        out_shape=jax.ShapeDtypeStruct((depth, 8, n), F32),
        compiler_params=_params(("parallel", "parallel"), nbytes),
        name="modulation",
    )(cond8, w_mod, b_mod.reshape(depth, 1, n))


FFN_SLOTS = 3


def _ffn_kernel(*refs, l, row_fn, norm_idx, tf, d_ff, tm, final_norm):
    if final_norm:
        (x_hbm, g_ref, sh_ref, sc_ref, gt_ref, wgu_hbm, wd_hbm, gf_ref, o_ref,
         x_buf, h_ref, wg_buf, wu_buf, wd_buf, wsem, xsem) = refs
    else:
        (x_hbm, g_ref, sh_ref, sc_ref, gt_ref, wgu_hbm, wd_hbm, o_ref,
         x_buf, h_ref, wg_buf, wu_buf, wd_buf, wsem, xsem) = refs
    n_chunk = d_ff // tf
    d = o_ref.shape[1]
    tn = _divisor_tile(d, 512, 128)
    i = pl.program_id(0)
    n_tiles = pl.num_programs(0)
    lookahead = FFN_SLOTS - 1

    def weight_copies(c, slot):
        col = pl.multiple_of(c * tf, tf)
        return (
            pltpu.make_async_copy(wgu_hbm.at[l, :, pl.ds(col, tf)], wg_buf.at[slot], wsem.at[0, slot]),
            pltpu.make_async_copy(wgu_hbm.at[l, :, pl.ds(d_ff + col, tf)], wu_buf.at[slot],
                                  wsem.at[1, slot]),
            pltpu.make_async_copy(wd_hbm.at[l, pl.ds(col, tf), :], wd_buf.at[slot], wsem.at[2, slot]),
        )

    def rows_copy(tile):
        return pltpu.make_async_copy(x_hbm.at[pl.ds(pl.multiple_of(tile * tm, tm), tm), :],
                                     x_buf, xsem.at[0])

    def start_weights(g):
        for cp in weight_copies(lax.rem(g, n_chunk), lax.rem(g, FFN_SLOTS)):
            cp.start()

    def wait_weights(g):
        for cp in weight_copies(lax.rem(g, n_chunk), lax.rem(g, FFN_SLOTS)):
            cp.wait()

    @pl.when(i == 0)
    def _():
        rows_copy(0).start()
        for g in range(lookahead):
            start_weights(jnp.int32(g))

    g0 = i * n_chunk
    rows_copy(i).wait()
    r = row_fn(i)
    x = x_buf[...]
    h_ref[...] = _adaln(x, g_ref[norm_idx:norm_idx + 1, :],
                        sh_ref[pl.ds(r, 1), :], sc_ref[pl.ds(r, 1), :]).astype(BF16)
    o_ref[...] = x

    @pl.when(i + 1 < n_tiles)
    def _():
        rows_copy(i + 1).start()

    gate = 0.5 * gt_ref[pl.ds(r, 1), :]

    def body(c, carry):
        g = g0 + c
        start_weights(g + lookahead)
        wait_weights(g)
        slot = lax.rem(g, FFN_SLOTS)
        h = h_ref[...]
        a = _mm(h, wg_buf[slot].astype(BF16))
        b = _mm(h, wu_buf[slot].astype(BF16))
        t = (_silu(a) * b).astype(BF16)
        for n in range(d // tn):
            cs = slice(n * tn, (n + 1) * tn)
            o_ref[:, cs] += gate[:, cs] * _mm(t, wd_buf[slot, :, cs].astype(BF16))
        return carry

    lax.fori_loop(0, n_chunk, body, 0, unroll=2)

    @pl.when(i + 1 == n_tiles)
    def _():
        for k in range(lookahead):
            wait_weights(g0 + n_chunk + k)

    if final_norm:
        y = o_ref[...]
        ms = jnp.mean(y * y, axis=-1, keepdims=True)
        o_ref[...] = (y * lax.rsqrt(ms + EPS)) * gf_ref[...]


def _ffn(x, mods, g_norm, w_gu, w_down, *, l, norm_idx, k_shift, k_scale, k_gate, tm, row_fn,
         g_final=None):
    m, d = x.shape
    d_ff = w_down.shape[1]
    tf = _divisor_tile(d_ff, 256, 128)
    final_norm = g_final is not None
    nbytes = (3 * tm * d * 4 + tm * d * 2 + 3 * FFN_SLOTS * d * tf * 4 + 3 * d * tf * 2
              + 3 * tm * tf * 4 + 2 * tm * 512 * 4)
    kern = functools.partial(_ffn_kernel, l=l, row_fn=row_fn, norm_idx=norm_idx, tf=tf, d_ff=d_ff,
                             tm=tm, final_norm=final_norm)
    in_specs = [
        pl.BlockSpec(memory_space=pl.ANY),
        pl.BlockSpec((None, 3, d), lambda i: (l, 0, 0)),
        pl.BlockSpec((None, 8, d), lambda i: (l, 0, k_shift)),
        pl.BlockSpec((None, 8, d), lambda i: (l, 0, k_scale)),
        pl.BlockSpec((None, 8, d), lambda i: (l, 0, k_gate)),
        pl.BlockSpec(memory_space=pl.ANY),
        pl.BlockSpec(memory_space=pl.ANY),
    ]
    args = [x, g_norm, mods, mods, mods, w_gu, w_down]
    if final_norm:
        in_specs.append(pl.BlockSpec((1, d), lambda i: (0, 0)))
        args.append(g_final.reshape(1, d))
    return pl.pallas_call(
        kern,
        grid=(m // tm,),
        in_specs=in_specs,
        out_specs=pl.BlockSpec((tm, d), lambda i: (i, 0)),
        out_shape=jax.ShapeDtypeStruct((m, d), F32),
        scratch_shapes=[
            pltpu.VMEM((tm, d), F32),
            pltpu.VMEM((tm, d), BF16),
            pltpu.VMEM((FFN_SLOTS, d, tf), F32),
            pltpu.VMEM((FFN_SLOTS, d, tf), F32),
            pltpu.VMEM((FFN_SLOTS, tf, d), F32),
            pltpu.SemaphoreType.DMA((3, FFN_SLOTS)),
            pltpu.SemaphoreType.DMA((1,)),
        ],
        compiler_params=_params(("arbitrary",), nbytes),
        name="ffn_final" if final_norm else "ffn",
    )(*args)


def _kvn_block(j):
    return jnp.where(j <= 2, 0, jnp.where(j >= 9, 5, jnp.clip(j - 4, 1, 4)))


def _inproj_kernel(*refs, is_ctx, row_fn, norm_idx):
    if is_ctx:
        (x_ref, g_ref, sh_ref, sc_ref, gq_ref, gk_ref, w_ref, zq_ref, h_ref, kvn_ref,
         acc_ref) = refs
    else:
        (x_ref, g_ref, sh_ref, sc_ref, gq_ref, gk_ref, cos_ref, sa_ref, sb_ref, w_ref,
         zq_ref, h_ref, acc_ref) = refs
    j = pl.program_id(1)

    def rope(y):
        if is_ctx:
            return y
        return (y * cos_ref[...] + pltpu.roll(y, 96, 1) * sa_ref[...]
                + pltpu.roll(y, 32, 1) * sb_ref[...])

    def cols(hh):
        return slice(hh * HEAD_DIM, (hh + 1) * HEAD_DIM)

    def project(h):
        return _mm(h, w_ref[...].astype(BF16))

    def query_norm_rope(acc):
        for hh in range(4):
            y = _headnorm(acc[:, cols(hh)], gq_ref[...])
            zq_ref[:, cols(hh)] = (rope(y) * Q_SCALE).astype(BF16)

    @pl.when(j == 0)
    def _():
        r = row_fn(pl.program_id(0))
        h = _adaln(x_ref[...], g_ref[norm_idx:norm_idx + 1, :],
                   sh_ref[pl.ds(r, 1), :], sc_ref[pl.ds(r, 1), :]).astype(BF16)
        h_ref[...] = h
        acc_ref[...] = project(h)

    @pl.when((j == 1) | (j == 2))
    def _():
        acc_ref[...] = project(h_ref[...])

    @pl.when(j < 2)
    def _():
        query_norm_rope(acc_ref[...])

    @pl.when(j == 2)
    def _():
        acc = acc_ref[...]
        for hh in range(2):
            y = _headnorm(acc[:, cols(hh)], gk_ref[...])
            if is_ctx:
                kvn_ref[:, cols(hh)] = y
            zq_ref[:, cols(hh)] = rope(y).astype(BF16)
        va = acc[:, 256:512]
        if is_ctx:
            kvn_ref[:, 256:512] = va
        zq_ref[:, 256:512] = va.astype(BF16)

    @pl.when((j == 3) | (j == 4))
    def _():
        zq_ref[...] = (project(h_ref[...]) * Q_SCALE).astype(BF16)

    @pl.when((j >= 5) & (j <= 8))
    def _():
        acc = project(h_ref[...])
        if is_ctx:
            kvn_ref[...] = acc
        zq_ref[...] = acc.astype(BF16)

    @pl.when((j == 9) | (j == 10))
    def _():
        acc = project(h_ref[...])
        for hh in range(4):
            zq_ref[:, cols(hh)] = (rope(acc[:, cols(hh)]) * Q_SCALE).astype(BF16)

    @pl.when(j == 11)
    def _():
        acc = project(h_ref[...])
        for hh in range(2):
            y = acc[:, cols(hh)]
            if is_ctx:
                kvn_ref[:, cols(hh)] = y
            zq_ref[:, cols(hh)] = rope(y).astype(BF16)
        vc = acc[:, 256:512]
        if is_ctx:
            kvn_ref[:, 256:512] = vc
        zq_ref[:, 256:512] = vc.astype(BF16)


def _inproj(x, mods, g_norm, g_q, g_k, w_in, rope_tabs, *, l, tm, row_fn, is_ctx):
    m, d = x.shape
    depth = w_in.shape[0]
    nj = QKV_W // COL_TILE
    in_specs = [
        pl.BlockSpec((tm, d), lambda i, j: (i, 0)),
        pl.BlockSpec((None, 3, d), lambda i, j: (l, 0, 0)),
        pl.BlockSpec((None, 8, d), lambda i, j: (l, 0, 3)),
        pl.BlockSpec((None, 8, d), lambda i, j: (l, 0, 4)),
        pl.BlockSpec((None, 1, HEAD_DIM), lambda i, j: (l, 0, 0)),
        pl.BlockSpec((None, 1, HEAD_DIM), lambda i, j: (l, 0, 0)),
    ]
    args = [x, g_norm, mods, mods, g_q.reshape(depth, 1, HEAD_DIM), g_k.reshape(depth, 1, HEAD_DIM)]
    if not is_ctx:
        n_pos = rope_tabs[0].shape[0]
        per = n_pos // tm
        for tab in rope_tabs:
            in_specs.append(pl.BlockSpec((tm, HEAD_DIM), lambda i, j: (i % per, 0)))
            args.append(tab)
    in_specs.append(pl.BlockSpec((None, d, COL_TILE), lambda i, j: (l, 0, j)))
    args.append(w_in)
    out_specs = [
        pl.BlockSpec((tm, COL_TILE), lambda i, j: (i, j)),
        pl.BlockSpec((tm, d), lambda i, j: (i, 0)),
    ]
    out_shape = [jax.ShapeDtypeStruct((m, QKV_W), BF16), jax.ShapeDtypeStruct((m, d), BF16)]
    if is_ctx:
        out_specs.append(pl.BlockSpec((tm, COL_TILE), lambda i, j: (i, _kvn_block(j))))
        out_shape.append(jax.ShapeDtypeStruct((m, KVN_W), F32))
    nbytes = (2 * tm * d * 4 + 2 * tm * d * 2 + 2 * d * COL_TILE * 4 + d * COL_TILE * 2
              + 3 * tm * COL_TILE * 4 + 2 * tm * COL_TILE * 2 + 6 * tm * HEAD_DIM * 4)
    kern = functools.partial(_inproj_kernel, is_ctx=is_ctx, row_fn=row_fn, norm_idx=1)
    return pl.pallas_call(
        kern,
        grid=(m // tm, nj),
        in_specs=in_specs,
        out_specs=out_specs,
        out_shape=out_shape,
        scratch_shapes=[pltpu.VMEM((tm, COL_TILE), F32)],
        compiler_params=_params(("parallel", "arbitrary"), nbytes),
        name="inproj_ctx" if is_ctx else "inproj_lat",
    )(*args)


def _ctx_attn_kernel(z_ref, sink_ref, o_ref, *, l, seq):
    def attend(q_off, n_q, k_off, v_off, sinks):
        q = jnp.concatenate(
            [z_ref[:, q_off + h * HEAD_DIM:q_off + (h + 1) * HEAD_DIM] for h in range(n_q)], axis=0)
        k = z_ref[:, k_off:k_off + HEAD_DIM]
        v = z_ref[:, v_off:v_off + HEAD_DIM]
        s = _qk_t(q, k)
        m = jnp.max(s, axis=-1, keepdims=True)
        if sinks is not None:
            sink_col = jnp.concatenate(
                [jnp.full((seq, 1), sk, F32) for sk in sinks], axis=0)
            m = jnp.maximum(m, sink_col)
        p = jnp.exp2(s - m).astype(BF16)
        acc = _mm(p, jnp.concatenate([v, jnp.ones_like(v)], axis=1))
        den = acc[:, HEAD_DIM:2 * HEAD_DIM]
        if sinks is not None:
            den = den + jnp.exp2(sink_col - m)
        return acc[:, 0:HEAD_DIM] / den

    def store(o, o_off, n_q):
        for h in range(n_q):
            o_ref[:, o_off + h * HEAD_DIM:o_off + (h + 1) * HEAD_DIM] = (
                o[h * seq:(h + 1) * seq].astype(BF16))

    grp = N_HEADS // KV_GQA
    for g in range(KV_GQA):
        o = attend(OFF_QA + g * grp * HEAD_DIM, grp, OFF_KA + g * HEAD_DIM,
                   OFF_VA + g * HEAD_DIM, None)
        store(o, g * grp * HEAD_DIM, grp)
    for h in range(N_HEADS):
        o = attend(OFF_QB + h * HEAD_DIM, 1, OFF_KB + h * HEAD_DIM, OFF_VB + h * HEAD_DIM, None)
        store(o, 1024 + h * HEAD_DIM, 1)
    grp = N_HEADS // KV_WIN
    for g in range(KV_WIN):
        sinks = [sink_ref[l, g * grp + h] for h in range(grp)]
        o = attend(OFF_QC + g * grp * HEAD_DIM, grp, OFF_KC + g * HEAD_DIM,
                   OFF_VC + g * HEAD_DIM, sinks)
        store(o, 2048 + g * grp * HEAD_DIM, grp)


def _ctx_attention(zq, sink2, *, l, batch, seq):
    kern = functools.partial(_ctx_attn_kernel, l=l, seq=seq)
    nbytes = 2 * seq * QKV_W * 2 + 2 * seq * 3072 * 2 + 8 * 4 * seq * seq * 4
    return pl.pallas_call(
        kern,
        grid=(batch,),
        in_specs=[
            pl.BlockSpec((seq, QKV_W), lambda b: (b, 0)),
            pl.BlockSpec(memory_space=pltpu.SMEM),
        ],
        out_specs=pl.BlockSpec((seq, 3072), lambda b: (b, 0)),
        out_shape=jax.ShapeDtypeStruct((batch * seq, 3072), BF16),
        compiler_params=_params(("parallel",), nbytes),
        name="attn_ctx",
    )(zq, sink2)


def _lat_gqa_kernel(q_ref, kl_ref, vl_ref, kc_ref, vc_ref, o_ref, k_sc, v_sc, s0_sc, s1_sc,
                    m0_sc, m1_sc, *, past, n_lat, tq, chunk):
    grp = N_HEADS // KV_GQA
    total = past + n_lat
    n_chunk = total // chunk
    nq = n_lat // tq
    k_sc[0:past, :] = kc_ref[...].astype(BF16)
    k_sc[past:total, :] = kl_ref[...]
    v_sc[0:past, 0:HEAD_DIM] = vc_ref[...].astype(BF16)
    v_sc[past:total, 0:HEAD_DIM] = vl_ref[...]
    v_sc[:, HEAD_DIM:2 * HEAD_DIM] = jnp.ones((total, HEAD_DIM), BF16)

    def row0(t):
        return t * tq if isinstance(t, int) else pl.multiple_of(t * tq, tq)

    def scores(t, s_sc, m_sc):
        qn = q_ref[pl.ds(row0(t), tq), :]
        q = jnp.concatenate([qn[:, h * HEAD_DIM:(h + 1) * HEAD_DIM] for h in range(grp)], axis=0)
        m = None
        for c in range(n_chunk):
            s = _qk_t(q, k_sc[c * chunk:(c + 1) * chunk, :])
            s_sc[:, c * chunk:(c + 1) * chunk] = s
            mc = jnp.max(s, axis=-1, keepdims=True)
            m = mc if m is None else jnp.maximum(m, mc)
        m_sc[...] = m

    def values(t, s_sc, m_sc):
        m = m_sc[...]
        acc = None
        for c in range(n_chunk):
            p = jnp.exp2(s_sc[:, c * chunk:(c + 1) * chunk] - m).astype(BF16)
            term = _mm(p, v_sc[c * chunk:(c + 1) * chunk, :])
            acc = term if acc is None else acc + term
        o = acc[:, 0:HEAD_DIM] / acc[:, HEAD_DIM:2 * HEAD_DIM]
        for h in range(grp):
            o_ref[pl.ds(row0(t), tq), h * HEAD_DIM:(h + 1) * HEAD_DIM] = (
                o[h * tq:(h + 1) * tq].astype(BF16))

    scores(0, s0_sc, m0_sc)

    def body(k, carry):
        t = 2 * k
        scores(t + 1, s1_sc, m1_sc)
        values(t, s0_sc, m0_sc)
        scores(t + 2, s0_sc, m0_sc)
        values(t + 1, s1_sc, m1_sc)
        return carry

    lax.fori_loop(0, nq // 2 - 1, body, 0)
    scores(nq - 1, s1_sc, m1_sc)
    values(nq - 2, s0_sc, m0_sc)
    values(nq - 1, s1_sc, m1_sc)


def _lat_gqa_attention(zq, cache_k, cache_v, *, l, n_b, n_lat):
    past = cache_k.shape[2]
    grp = N_HEADS // KV_GQA
    tq = _divisor_tile(n_lat, 128, 16)
    assert (n_lat // tq) % 2 == 0
    total = past + n_lat
    chunk = _divisor_tile(total, 1536, 128)
    wq = grp * HEAD_DIM
    ka_blk = OFF_KA // HEAD_DIM
    va_blk = OFF_VA // HEAD_DIM
    kern = functools.partial(_lat_gqa_kernel, past=past, n_lat=n_lat, tq=tq, chunk=chunk)
    nbytes = (4 * n_lat * wq * 2 + 4 * n_lat * HEAD_DIM * 2 + 4 * past * HEAD_DIM * 4
              + total * HEAD_DIM * 2 * 3 + 2 * grp * tq * total * 4 + 4 * grp * tq * chunk * 4)
    return pl.pallas_call(
        kern,
        grid=(n_b, KV_GQA),
        in_specs=[
            pl.BlockSpec((n_lat, wq), lambda b, g: (b, g)),
            pl.BlockSpec((n_lat, HEAD_DIM), lambda b, g: (b, ka_blk + g)),
            pl.BlockSpec((n_lat, HEAD_DIM), lambda b, g: (b, va_blk + g)),
            pl.BlockSpec((None, None, past, HEAD_DIM), lambda b, g: (b, l, 0, g)),
            pl.BlockSpec((None, None, past, HEAD_DIM), lambda b, g: (b, l, 0, g)),
        ],
        out_specs=pl.BlockSpec((n_lat, wq), lambda b, g: (b, g)),
        out_shape=jax.ShapeDtypeStruct((n_b * n_lat, N_HEADS * HEAD_DIM), BF16),
        scratch_shapes=[
            pltpu.VMEM((total, HEAD_DIM), BF16),
            pltpu.VMEM((total, 2 * HEAD_DIM), BF16),
            pltpu.VMEM((grp * tq, total), F32),
            pltpu.VMEM((grp * tq, total), F32),
            pltpu.VMEM((grp * tq, 1), F32),
            pltpu.VMEM((grp * tq, 1), F32),
        ],
        compiler_params=_params(("parallel", "parallel"), nbytes),
        name="attn_lat_global",
    )(zq, zq, zq, cache_k, cache_v)


def _nat_row_tables(rpb):
    w, wc = GRID_W, NAT_COLS
    c = np.arange(w)
    cs = np.clip(c - wc // 2, 0, w - wc)
    col_ok = (c[None, :] >= cs[:, None]) & (c[None, :] < cs[:, None] + wc)
    dc = np.clip(c[None, :] - c[:, None] + wc - 1, 0, 2 * wc - 2)
    t_tab = jnp.zeros(rpb.shape[:3] + (w, w), F32)
    for d in range(2 * wc - 1):
        t_tab = jnp.where(jnp.asarray(dc == d), rpb[..., d][..., None, None].astype(F32), t_tab)
    return jnp.where(jnp.asarray(col_ok), t_tab, NEG_INF) * LOG2E


def _nat_tile_plan(rows):
    wr = NAT_ROWS
    plan = []
    for r0 in (0, NAT_QROWS, rows - NAT_QROWS):
        ws = int(np.clip(r0 - wr // 2, 0, rows - NAT_KROWS))
        cls = []
        for a in range(NAT_QROWS):
            qr = r0 + a
            rs = int(np.clip(qr - wr // 2, 0, rows - wr))
            cls.append([(ws + b) - qr + wr - 1 if rs <= ws + b < rs + wr else None
                        for b in range(NAT_KROWS)])
        plan.append(cls)
    return plan


def _nat_kernel(q_ref, k_ref, v_ref, kc_ref, vc_ref, t_ref, o_ref, kc_sc, vc_sc, bias_ref,
                *, rows):
    tq = NAT_QROWS * GRID_W
    tk = NAT_KROWS * GRID_W
    n_tiles = rows // NAT_QROWS
    kc_sc[...] = kc_ref[...].astype(BF16)
    vc_sc[...] = vc_ref[...].astype(BF16)
    neg_blk = jnp.full((GRID_W, GRID_W), NEG_INF * LOG2E, F32)
    for cls, cls_plan in enumerate(_nat_tile_plan(rows)):
        for a, row_plan in enumerate(cls_plan):
            for b, dr in enumerate(row_plan):
                bias_ref[cls, a * GRID_W:(a + 1) * GRID_W, b * GRID_W:(b + 1) * GRID_W] = (
                    neg_blk if dr is None else t_ref[dr])

    def body(t, carry):
        r0 = t * NAT_QROWS
        ws = jnp.clip(r0 - NAT_ROWS // 2, 0, rows - NAT_KROWS)
        cls = jnp.where(t == 0, 0, jnp.where(t == n_tiles - 1, 2, 1))
        q = q_ref[pl.ds(pl.multiple_of(t * tq, tq), tq), :]
        k0 = pl.multiple_of(ws * GRID_W, GRID_W)
        s_loc = _qk_t(q, k_ref[pl.ds(k0, tk), :]) + bias_ref[cls]
        s_ctx = _qk_t(q, kc_sc[...])
        m = jnp.maximum(jnp.max(s_loc, axis=-1, keepdims=True),
                        jnp.max(s_ctx, axis=-1, keepdims=True))
        p_loc = jnp.exp2(s_loc - m)
        p_ctx = jnp.exp2(s_ctx - m)
        den = jnp.sum(p_loc, axis=-1, keepdims=True) + jnp.sum(p_ctx, axis=-1, keepdims=True)
        o = _mm(p_ctx.astype(BF16), vc_sc[...]) + _mm(p_loc.astype(BF16), v_ref[pl.ds(k0, tk), :])
        o_ref[pl.ds(pl.multiple_of(t * tq, tq), tq), :] = (o / den).astype(BF16)
        return carry

    lax.fori_loop(0, n_tiles, body, 0, unroll=2)


def _nat_attention(zq, cache_k, cache_v, row_tabs, *, l, n_b, n_lat):
    past = cache_k.shape[2]
    rows = n_lat // GRID_W
    tq = NAT_QROWS * GRID_W
    tk = NAT_KROWS * GRID_W
    n_dr = row_tabs.shape[2]
    qb, kb, vb = OFF_QB // HEAD_DIM, OFF_KB // HEAD_DIM, OFF_VB // HEAD_DIM
    kern = functools.partial(_nat_kernel, rows=rows)
    nbytes = (8 * n_lat * HEAD_DIM * 2 + 4 * past * HEAD_DIM * 4 + 2 * past * HEAD_DIM * 2
              + 2 * n_dr * GRID_W * 128 * 4 + 3 * tq * tk * 4 + 6 * tq * (tk + past) * 4)
    return pl.pallas_call(
        kern,
        grid=(n_b, N_HEADS),
        in_specs=[
            pl.BlockSpec((n_lat, HEAD_DIM), lambda b, h: (b, qb + h)),
            pl.BlockSpec((n_lat, HEAD_DIM), lambda b, h: (b, kb + h)),
            pl.BlockSpec((n_lat, HEAD_DIM), lambda b, h: (b, vb + h)),
            pl.BlockSpec((None, None, past, HEAD_DIM), lambda b, h: (b, l, 0, h)),
            pl.BlockSpec((None, None, past, HEAD_DIM), lambda b, h: (b, l, 0, h)),
            pl.BlockSpec((None, None, n_dr, GRID_W, GRID_W), lambda b, h: (l, h, 0, 0, 0)),
        ],
        out_specs=pl.BlockSpec((n_lat, HEAD_DIM), lambda b, h: (b, h)),
        out_shape=jax.ShapeDtypeStruct((n_b * n_lat, N_HEADS * HEAD_DIM), BF16),
        scratch_shapes=[pltpu.VMEM((past, HEAD_DIM), BF16), pltpu.VMEM((past, HEAD_DIM), BF16),
                        pltpu.VMEM((3, tq, tk), F32)],
        compiler_params=_params(("parallel", "parallel"), nbytes),
        name="attn_lat_nat",
    )(zq, zq, zq, cache_k, cache_v, row_tabs)


def _win_kernel(q_ref, k_ref, v_ref, kc_ref, vc_ref, sink_ref, o_ref, kc_sc, vc_sc, bias_sc,
                *, l, past, n_lat):
    grp = N_HEADS // KV_WIN
    band = 3 * Q_BLOCK
    n_blk = n_lat // Q_BLOCK
    g = pl.program_id(1)
    kc_sc[...] = kc_ref[...].astype(BF16)
    vc_sc[...] = vc_ref[...].astype(BF16)
    qi = lax.broadcasted_iota(jnp.int32, (Q_BLOCK, band), 0)
    kj = lax.broadcasted_iota(jnp.int32, (Q_BLOCK, band), 1)
    for cls, off in enumerate((0, -Q_BLOCK, -2 * Q_BLOCK)):
        bias_sc[cls] = jnp.where(jnp.abs(kj + off - qi) <= WIN_RADIUS, 0.0, NEG_INF).astype(F32)

    def body(n, carry):
        start = pl.multiple_of(jnp.clip((n - 1) * Q_BLOCK, 0, n_lat - band), Q_BLOCK)
        cls = jnp.where(n == 0, 0, jnp.where(n == n_blk - 1, 2, 1))
        q0 = pl.multiple_of(n * Q_BLOCK, Q_BLOCK)
        qn = q_ref[pl.ds(q0, Q_BLOCK), :]
        q = jnp.concatenate([qn[:, h * HEAD_DIM:(h + 1) * HEAD_DIM] for h in range(grp)], axis=0)
        s_ctx_all = _qk_t(q, kc_sc[...])
        s_loc_all = _qk_t(q, k_ref[pl.ds(start, band), :])
        bias = bias_sc[cls]
        p_ctx, p_loc, dens = [], [], []
        for h in range(grp):
            s_ctx = s_ctx_all[h * Q_BLOCK:(h + 1) * Q_BLOCK]
            s_loc = s_loc_all[h * Q_BLOCK:(h + 1) * Q_BLOCK] + bias
            sk = sink_ref[l, g * grp + h]
            m = jnp.maximum(jnp.maximum(jnp.max(s_ctx, axis=-1, keepdims=True),
                                        jnp.max(s_loc, axis=-1, keepdims=True)), sk)
            pc = jnp.exp2(s_ctx - m)
            pw = jnp.exp2(s_loc - m)
            dens.append(jnp.sum(pc, axis=-1, keepdims=True) + jnp.sum(pw, axis=-1, keepdims=True)
                        + jnp.exp2(sk - m))
            p_ctx.append(pc.astype(BF16))
            p_loc.append(pw.astype(BF16))
        acc = (_mm(jnp.concatenate(p_ctx, axis=0), vc_sc[...])
               + _mm(jnp.concatenate(p_loc, axis=0), v_ref[pl.ds(start, band), :]))
        for h in range(grp):
            o = acc[h * Q_BLOCK:(h + 1) * Q_BLOCK] / dens[h]
            o_ref[pl.ds(q0, Q_BLOCK), h * HEAD_DIM:(h + 1) * HEAD_DIM] = o.astype(BF16)
        return carry

    lax.fori_loop(0, n_blk, body, 0, unroll=2)


def _win_attention(zq, cache_k, cache_v, sink2, *, l, n_b, n_lat):
    past = cache_k.shape[2]
    grp = N_HEADS // KV_WIN
    band = 3 * Q_BLOCK
    wq = grp * HEAD_DIM
    qc_blk, kc_blk, vc_blk = OFF_QC // wq, OFF_KC // HEAD_DIM, OFF_VC // HEAD_DIM
    kern = functools.partial(_win_kernel, l=l, past=past, n_lat=n_lat)
    nbytes = (4 * n_lat * wq * 2 + 4 * n_lat * HEAD_DIM * 2 + 4 * past * HEAD_DIM * 4
              + (past + band) * HEAD_DIM * 2 * 3 + 3 * Q_BLOCK * band * 4
              + 4 * grp * Q_BLOCK * (past + band) * 4)
    return pl.pallas_call(
        kern,
        grid=(n_b, KV_WIN),
        in_specs=[
            pl.BlockSpec((n_lat, wq), lambda b, g: (b, qc_blk + g)),
            pl.BlockSpec((n_lat, HEAD_DIM), lambda b, g: (b, kc_blk + g)),
            pl.BlockSpec((n_lat, HEAD_DIM), lambda b, g: (b, vc_blk + g)),
            pl.BlockSpec((None, None, past, HEAD_DIM), lambda b, g: (b, l, 0, g)),
            pl.BlockSpec((None, None, past, HEAD_DIM), lambda b, g: (b, l, 0, g)),
            pl.BlockSpec(memory_space=pltpu.SMEM),
        ],
        out_specs=pl.BlockSpec((n_lat, wq), lambda b, g: (b, g)),
        out_shape=jax.ShapeDtypeStruct((n_b * n_lat, N_HEADS * HEAD_DIM), BF16),
        scratch_shapes=[
            pltpu.VMEM((past, HEAD_DIM), BF16),
            pltpu.VMEM((past, HEAD_DIM), BF16),
            pltpu.VMEM((3, Q_BLOCK, band), F32),
        ],
        compiler_params=_params(("parallel", "parallel"), nbytes),
        name="attn_lat_window",
    )(zq, zq, zq, cache_k, cache_v, sink2)


def _merge_kernel(h_ref, oa_ref, ob_ref, oc_ref, wga_ref, wgb_ref, wgc_ref,
                  wba_ref, wbb_ref, wbc_ref, m_ref):
    h = h_ref[...]
    acc = None
    for o_ref, wg_ref, wb_ref in ((oa_ref, wga_ref, wba_ref), (ob_ref, wgb_ref, wbb_ref),
                                  (oc_ref, wgc_ref, wbc_ref)):
        gate = jax.nn.sigmoid(_mm(h, wg_ref[...].astype(BF16)))
        term = gate * _mm(o_ref[...], wb_ref[...].astype(BF16))
        acc = term if acc is None else acc + term
    m_ref[...] = acc.astype(BF16)


def _merge(h, outs, w_in, w_branch, *, l, tm):
    m, d = h.shape
    bw = w_branch.shape[2]
    tn = _divisor_tile(d, 256, 128)
    g0 = OFF_GATES // tn
    per = d // tn

    def o_spec(width, first):
        return pl.BlockSpec((tm, bw), lambda i, j: (i, first))

    in_specs = [pl.BlockSpec((tm, d), lambda i, j: (i, 0))]
    args = [h]
    for o, first in outs:
        in_specs.append(o_spec(bw, first))
        args.append(o)
    for k in range(3):
        in_specs.append(pl.BlockSpec((None, d, tn), lambda i, j, k=k: (l, 0, g0 + k * per + j)))
        args.append(w_in)
    for k in range(3):
        in_specs.append(pl.BlockSpec((None, None, bw, tn), lambda i, j, k=k: (l, k, 0, j)))
        args.append(w_branch)
    nbytes = (2 * tm * d * 2 + 6 * tm * bw * 2 + 6 * d * tn * 4 + 6 * bw * tn * 4
              + 3 * (d + bw) * tn * 2 + 6 * tm * tn * 4)
    return pl.pallas_call(
        _merge_kernel,
        grid=(m // tm, d // tn),
        in_specs=in_specs,
        out_specs=pl.BlockSpec((tm, tn), lambda i, j: (i, j)),
        out_shape=jax.ShapeDtypeStruct((m, d), BF16),
        compiler_params=_params(("parallel", "parallel"), nbytes),
        name="branch_merge",
    )(*args)


def _outproj_kernel(m_ref, w_ref, x_ref, gt_ref, o_ref, *, row_fn):
    r = row_fn(pl.program_id(0))
    y = _mm(m_ref[...], w_ref[...].astype(BF16))
    o_ref[...] = x_ref[...] + gt_ref[pl.ds(r, 1), :] * y


def _outproj(mm, w_o, x, mods, *, l, tm, row_fn):
    m, d = x.shape
    tn = _divisor_tile(d, 512, 128)
    kern = functools.partial(_outproj_kernel, row_fn=row_fn)
    nbytes = 2 * tm * d * 2 + 2 * d * tn * 4 + d * tn * 2 + 5 * tm * tn * 4
    return pl.pallas_call(
        kern,
        grid=(m // tm, d // tn),
        in_specs=[
            pl.BlockSpec((tm, d), lambda i, j: (i, 0)),
            pl.BlockSpec((None, d, tn), lambda i, j: (l, 0, j)),
            pl.BlockSpec((tm, tn), lambda i, j: (i, j)),
            pl.BlockSpec((None, 8, tn), lambda i, j: (l, 0, 5 * (d // tn) + j)),
        ],
        out_specs=pl.BlockSpec((tm, tn), lambda i, j: (i, j)),
        out_shape=jax.ShapeDtypeStruct((m, d), F32),
        compiler_params=_params(("parallel", "parallel"), nbytes),
        name="out_proj",
    )(mm, w_o, x, mods)


def _pack_kv_kernel(*refs, seq):
    n_in = len(refs) - len(KVN_PARTS)
    for l in range(n_in):
        for out_ref, (off, heads) in zip(refs[n_in:], KVN_PARTS):
            for h in range(heads):
                c0 = off + h * HEAD_DIM
                out_ref[l, pl.ds(h, seq, stride=heads), :] = refs[l][:, c0:c0 + HEAD_DIM]


def _pack_context_kv(kvn, *, batch, seq):
    depth = len(kvn)
    kern = functools.partial(_pack_kv_kernel, seq=seq)
    nbytes = 2 * depth * seq * KVN_W * 4 * 2
    return pl.pallas_call(
        kern,
        grid=(batch,),
        in_specs=[pl.BlockSpec((seq, KVN_W), lambda b: (b, 0)) for _ in range(depth)],
        out_specs=[pl.BlockSpec((None, depth, seq * heads, HEAD_DIM), lambda b: (b, 0, 0, 0))
                   for _, heads in KVN_PARTS],
        out_shape=[jax.ShapeDtypeStruct((batch, depth, seq * heads, HEAD_DIM), F32)
                   for _, heads in KVN_PARTS],
        compiler_params=_params(("parallel",), nbytes),
        name="pack_context_kv",
    )(*kvn)


def _rope_tables(n_lat):
    half = HEAD_DIM // 2
    nf = half // 2
    t = jnp.arange(n_lat)
    row = (t // GRID_W).astype(F32)
    col = (t % GRID_W).astype(F32)
    inv = 1.0 / (ROPE_THETA ** (jnp.arange(nf, dtype=F32) / nf))
    ang_r = row[:, None] * inv[None, :]
    ang_c = col[:, None] * inv[None, :]
    cr, sr, cc, sc = jnp.cos(ang_r), jnp.sin(ang_r), jnp.cos(ang_c), jnp.sin(ang_c)
    zero = jnp.zeros_like(sr)
    cos = jnp.concatenate([cr, cr, cc, cc], axis=-1)
    sin_a = jnp.concatenate([-sr, zero, -sc, zero], axis=-1)
    sin_b = jnp.concatenate([zero, sr, zero, sc], axis=-1)
    return cos, sin_a, sin_b


def _trunk_layer(x, *, l, tm, row_fn, is_ctx, mods, g_norm, w_ffn1_gu, w_ffn1_down, w_in, g_q, g_k,
                 w_branch, w_o, w_ffn2_gu, w_ffn2_down, attend, rope_tabs, g_final):
    x = _ffn(x, mods, g_norm, w_ffn1_gu, w_ffn1_down, l=l, norm_idx=0, k_shift=0, k_scale=1,
             k_gate=2, tm=tm, row_fn=row_fn)
    res = _inproj(x, mods, g_norm, g_q, g_k, w_in, rope_tabs, l=l, tm=tm, row_fn=row_fn,
                  is_ctx=is_ctx)
    zq, h2 = res[0], res[1]
    outs = attend(zq)
    mm = _merge(h2, outs, w_in, w_branch, l=l, tm=tm)
    x = _outproj(mm, w_o, x, mods, l=l, tm=tm, row_fn=row_fn)
    x = _ffn(x, mods, g_norm, w_ffn2_gu, w_ffn2_down, l=l, norm_idx=2, k_shift=6, k_scale=7,
             k_gate=8, tm=tm, row_fn=row_fn, g_final=g_final)
    return x, (res[2] if is_ctx else None)


def kernel(x_prompt, x_sample, cache_k_gqa, cache_v_gqa, cache_k_nat, cache_v_nat, cache_k_win, cache_v_win, c, c_ctx, w_mod, b_mod, g_norm, w_ffn1_gu, w_ffn1_down, w_in, g_q, g_k, rpb, sink, w_branch, w_o, w_ffn2_gu, w_ffn2_down, g_final):
    batch, seq, d = x_prompt.shape
    n_b, n_lat, _ = x_sample.shape
    depth = w_mod.shape[0]
    past = cache_k_gqa.shape[2]
    rows = n_lat // GRID_W
    assert n_lat % GRID_W == 0 and rows % NAT_QROWS == 0 and rows >= NAT_KROWS + NAT_QROWS
    assert n_lat % Q_BLOCK == 0 and n_lat >= 3 * Q_BLOCK and n_b + 1 <= 8

    m_ctx = batch * seq
    tm_ctx = _divisor_tile(m_ctx, 1024, 16)
    tm_lat = _divisor_tile(n_lat, 1024, 16)
    per_lat = n_lat // tm_lat

    cond8 = jnp.zeros((8, d), F32).at[0].set(c_ctx).at[1:1 + n_b].set(c)
    mods = _modulation(cond8, w_mod, b_mod)

    sink2 = sink * LOG2E
    rope_tabs = _rope_tables(n_lat)
    nat_tabs = _nat_row_tables(rpb)
    ck_gqa = cache_k_gqa.reshape(n_b, depth, past, KV_GQA * HEAD_DIM)
    cv_gqa = cache_v_gqa.reshape(n_b, depth, past, KV_GQA * HEAD_DIM)
    ck_nat = cache_k_nat.reshape(n_b, depth, past, N_HEADS * HEAD_DIM)
    cv_nat = cache_v_nat.reshape(n_b, depth, past, N_HEADS * HEAD_DIM)
    ck_win = cache_k_win.reshape(n_b, depth, past, KV_WIN * HEAD_DIM)
    cv_win = cache_v_win.reshape(n_b, depth, past, KV_WIN * HEAD_DIM)

    weights = dict(g_norm=g_norm, w_ffn1_gu=w_ffn1_gu, w_ffn1_down=w_ffn1_down, w_in=w_in,
                   g_q=g_q, g_k=g_k, w_branch=w_branch, w_o=w_o, w_ffn2_gu=w_ffn2_gu,
                   w_ffn2_down=w_ffn2_down, mods=mods)

    y_p = x_prompt.reshape(m_ctx, d)
    y_s = x_sample.reshape(n_b * n_lat, d)
    kvn = []
    for l in range(depth):
        def attend_ctx(zq, l=l):
            o = _ctx_attention(zq, sink2, l=l, batch=batch, seq=seq)
            return [(o, 0), (o, 1), (o, 2)]

        gf = g_final if l == depth - 1 else None
        y_p, kvn_l = _trunk_layer(y_p, l=l, tm=tm_ctx, row_fn=lambda i: 0, is_ctx=True,
                                  attend=attend_ctx, rope_tabs=None, g_final=gf, **weights)
        kvn.append(kvn_l)

        def attend_lat(zq, l=l):
            oa = _lat_gqa_attention(zq, ck_gqa, cv_gqa, l=l, n_b=n_b, n_lat=n_lat)
            ob = _nat_attention(zq, ck_nat, cv_nat, nat_tabs, l=l, n_b=n_b, n_lat=n_lat)
            oc = _win_attention(zq, ck_win, cv_win, sink2, l=l, n_b=n_b, n_lat=n_lat)
            return [(oa, 0), (ob, 0), (oc, 0)]

        y_s, _ = _trunk_layer(y_s, l=l, tm=tm_lat, row_fn=lambda i: 1 + i // per_lat, is_ctx=False,
                              attend=attend_lat, rope_tabs=rope_tabs, g_final=gf, **weights)

    y_prompt = y_p.reshape(batch, seq, d)
    y_sample = y_s.reshape(n_b, n_lat, d)

    new_kv = _pack_context_kv(kvn, batch=batch, seq=seq)
    return (y_prompt, y_sample) + tuple(
        a.reshape(batch, depth, seq, heads, HEAD_DIM) for a, (_, heads) in zip(new_kv, KVN_PARTS))
```

```python
import functools
import math

import numpy as np
import jax
import jax.numpy as jnp
from jax import lax
from jax.experimental import pallas as pl
from jax.experimental.pallas import tpu as pltpu

F32 = jnp.float32
BF16 = jnp.bfloat16

HEAD_DIM = 128
N_HEADS = 8
KV_GQA = 2
KV_WIN = 2
GRID_W = 64
NAT_ROWS = 8
NAT_COLS = 16
WIN_RADIUS = 128
Q_BLOCK = 128
ROPE_THETA = 10000.0
EPS = 1e-6
NEG_INF = -1e30
LOG2E = math.log2(math.e)
Q_SCALE = (HEAD_DIM ** -0.5) * LOG2E

OFF_QA, OFF_KA, OFF_VA = 0, 1024, 1280
OFF_QB, OFF_KB, OFF_VB = 1536, 2560, 3584
OFF_QC, OFF_KC, OFF_VC = 4608, 5632, 5888
OFF_GATES = 6144
QKV_W = 6144
COL_TILE = 512
KVN_W = 3072
KVN_PARTS = ((0, KV_GQA), (256, KV_GQA), (512, N_HEADS), (1536, N_HEADS),
             (2560, KV_WIN), (2816, KV_WIN))

V7X_VMEM_BYTES = 64 * 1024 * 1024
VMEM_CAP_BYTES = 60 * 1024 * 1024
NAT_QROWS = 4
NAT_KROWS = NAT_ROWS + NAT_QROWS


def _vmem_limit(nbytes):
    return int(min(VMEM_CAP_BYTES, max(32 * 1024 * 1024, nbytes * 3 // 2)))


def _params(sem, nbytes):
    return pltpu.CompilerParams(dimension_semantics=sem, vmem_limit_bytes=_vmem_limit(nbytes))


def _divisor_tile(n, cap, mult):
    if n <= cap:
        return n
    t = (cap // mult) * mult
    while t > mult and n % t:
        t -= mult
    assert n % t == 0, (n, cap, mult)
    return t


def _adaln(x, g, shift, scale):
    ms = jnp.mean(x * x, axis=-1, keepdims=True)
    y = x * lax.rsqrt(ms + EPS)
    return (y * g) * (1.0 + scale) + shift


def _headnorm(y, g):
    ms = jnp.mean(y * y, axis=-1, keepdims=True)
    return (y * lax.rsqrt(ms + EPS)) * g


def _silu(a):
    return a * jax.nn.sigmoid(a)


def _qk_t(q, k):
    return lax.dot_general(q, k, (((1,), (1,)), ((), ())), preferred_element_type=F32)


def _mm(a, b):
    return jnp.dot(a, b, preferred_element_type=F32)


def _mod_kernel(c_ref, w_ref, b_ref, o_ref):
    a = _silu(c_ref[...]).astype(BF16)
    o_ref[...] = _mm(a, w_ref[...].astype(BF16)) + b_ref[...]


def _modulation(cond8, w_mod, b_mod):
    depth, d, n = w_mod.shape
    tn = _divisor_tile(n, 1024, 128)
    nbytes = 2 * d * tn * 4 + d * tn * 2 + 4 * 8 * tn * 4 + 8 * d * 4
    return pl.pallas_call(
        _mod_kernel,
        grid=(depth, n // tn),
        in_specs=[
            pl.BlockSpec((8, d), lambda l, j: (0, 0)),
            pl.BlockSpec((None, d, tn), lambda l, j: (l, 0, j)),
            pl.BlockSpec((None, 1, tn), lambda l, j: (l, 0, j)),
        ],
        out_specs=pl.BlockSpec((None, 8, tn), lambda l, j: (l, 0, j)),
        out_shape=jax.ShapeDtypeStruct((depth, 8, n), F32),
        compiler_params=_params(("parallel", "parallel"), nbytes),
        name="modulation",
    )(cond8, w_mod, b_mod.reshape(depth, 1, n))


FFN_SLOTS = 3


def _ffn_kernel(*refs, l, row_fn, norm_idx, tf, d_ff, tm, final_norm):
    if final_norm:
        (x_hbm, g_ref, sh_ref, sc_ref, gt_ref, wgu_hbm, wd_hbm, gf_ref, o_ref,
         x_buf, h_ref, wg_buf, wu_buf, wd_buf, wsem, xsem) = refs
    else:
        (x_hbm, g_ref, sh_ref, sc_ref, gt_ref, wgu_hbm, wd_hbm, o_ref,
         x_buf, h_ref, wg_buf, wu_buf, wd_buf, wsem, xsem) = refs
    n_chunk = d_ff // tf
    d = o_ref.shape[1]
    tn = _divisor_tile(d, 512, 128)
    i = pl.program_id(0)
    n_tiles = pl.num_programs(0)
    lookahead = FFN_SLOTS - 1

    def weight_copies(c, slot):
        col = pl.multiple_of(c * tf, tf)
        return (
            pltpu.make_async_copy(wgu_hbm.at[l, :, pl.ds(col, tf)], wg_buf.at[slot], wsem.at[0, slot]),
            pltpu.make_async_copy(wgu_hbm.at[l, :, pl.ds(d_ff + col, tf)], wu_buf.at[slot],
                                  wsem.at[1, slot]),
            pltpu.make_async_copy(wd_hbm.at[l, pl.ds(col, tf), :], wd_buf.at[slot], wsem.at[2, slot]),
        )

    def rows_copy(tile):
        return pltpu.make_async_copy(x_hbm.at[pl.ds(pl.multiple_of(tile * tm, tm), tm), :],
                                     x_buf, xsem.at[0])

    def start_weights(g):
        for cp in weight_copies(lax.rem(g, n_chunk), lax.rem(g, FFN_SLOTS)):
            cp.start()

    def wait_weights(g):
        for cp in weight_copies(lax.rem(g, n_chunk), lax.rem(g, FFN_SLOTS)):
            cp.wait()

    @pl.when(i == 0)
    def _():
        rows_copy(0).start()
        for g in range(lookahead):
            start_weights(jnp.int32(g))

    g0 = i * n_chunk
    r = row_fn(i)
    gate = 0.5 * gt_ref[pl.ds(r, 1), :]

    def chunk(g, h):
        slot = lax.rem(g, FFN_SLOTS)
        a = _mm(h, wg_buf[slot].astype(BF16))
        b = _mm(h, wu_buf[slot].astype(BF16))
        t = (_silu(a) * b).astype(BF16)
        for n in range(d // tn):
            cs = slice(n * tn, (n + 1) * tn)
            o_ref[:, cs] += gate[:, cs] * _mm(t, wd_buf[slot, :, cs].astype(BF16))

    rows_copy(i).wait()
    start_weights(g0 + lookahead)
    wait_weights(g0)
    x = x_buf[...]
    h = _adaln(x, g_ref[norm_idx:norm_idx + 1, :],
               sh_ref[pl.ds(r, 1), :], sc_ref[pl.ds(r, 1), :]).astype(BF16)
    h_ref[...] = h
    o_ref[...] = x
    chunk(g0, h)

    @pl.when(i + 1 < n_tiles)
    def _():
        rows_copy(i + 1).start()

    def body(c, carry):
        g = g0 + c
        start_weights(g + lookahead)
        wait_weights(g)
        chunk(g, h_ref[...])
        return carry

    lax.fori_loop(1, n_chunk, body, 0, unroll=2)

    @pl.when(i + 1 == n_tiles)
    def _():
        for k in range(lookahead):
            wait_weights(g0 + n_chunk + k)

    if final_norm:
        y = o_ref[...]
        ms = jnp.mean(y * y, axis=-1, keepdims=True)
        o_ref[...] = (y * lax.rsqrt(ms + EPS)) * gf_ref[...]


def _ffn(x, mods, g_norm, w_gu, w_down, *, l, norm_idx, k_shift, k_scale, k_gate, tm, row_fn,
         g_final=None):
    m, d = x.shape
    d_ff = w_down.shape[1]
    tf = _divisor_tile(d_ff, 256, 128)
    final_norm = g_final is not None
    nbytes = (3 * tm * d * 4 + tm * d * 2 + 3 * FFN_SLOTS * d * tf * 4 + 3 * d * tf * 2
              + 3 * tm * tf * 4 + 2 * tm * 512 * 4)
    kern = functools.partial(_ffn_kernel, l=l, row_fn=row_fn, norm_idx=norm_idx, tf=tf, d_ff=d_ff,
                             tm=tm, final_norm=final_norm)
    in_specs = [
        pl.BlockSpec(memory_space=pl.ANY),
        pl.BlockSpec((None, 3, d), lambda i: (l, 0, 0)),
        pl.BlockSpec((None, 8, d), lambda i: (l, 0, k_shift)),
        pl.BlockSpec((None, 8, d), lambda i: (l, 0, k_scale)),
        pl.BlockSpec((None, 8, d), lambda i: (l, 0, k_gate)),
        pl.BlockSpec(memory_space=pl.ANY),
        pl.BlockSpec(memory_space=pl.ANY),
    ]
    args = [x, g_norm, mods, mods, mods, w_gu, w_down]
    if final_norm:
        in_specs.append(pl.BlockSpec((1, d), lambda i: (0, 0)))
        args.append(g_final.reshape(1, d))
    return pl.pallas_call(
        kern,
        grid=(m // tm,),
        in_specs=in_specs,
        out_specs=pl.BlockSpec((tm, d), lambda i: (i, 0)),
        out_shape=jax.ShapeDtypeStruct((m, d), F32),
        scratch_shapes=[
            pltpu.VMEM((tm, d), F32),
            pltpu.VMEM((tm, d), BF16),
            pltpu.VMEM((FFN_SLOTS, d, tf), F32),
            pltpu.VMEM((FFN_SLOTS, d, tf), F32),
            pltpu.VMEM((FFN_SLOTS, tf, d), F32),
            pltpu.SemaphoreType.DMA((3, FFN_SLOTS)),
            pltpu.SemaphoreType.DMA((1,)),
        ],
        compiler_params=_params(("arbitrary",), nbytes),
        name="ffn_final" if final_norm else "ffn",
    )(*args)


def _kvn_block(j):
    return jnp.where(j <= 2, 0, jnp.where(j >= 9, 5, jnp.clip(j - 4, 1, 4)))


def _inproj_kernel(*refs, is_ctx, row_fn, norm_idx):
    if is_ctx:
        (x_ref, g_ref, sh_ref, sc_ref, gq_ref, gk_ref, w_ref, zq_ref, h_ref, kvn_ref,
         acc_ref) = refs
    else:
        (x_ref, g_ref, sh_ref, sc_ref, gq_ref, gk_ref, cos_ref, sa_ref, sb_ref, w_ref,
         zq_ref, h_ref, acc_ref) = refs
    j = pl.program_id(1)

    def rope(y):
        if is_ctx:
            return y
        return (y * cos_ref[...] + pltpu.roll(y, 96, 1) * sa_ref[...]
                + pltpu.roll(y, 32, 1) * sb_ref[...])

    def cols(hh):
        return slice(hh * HEAD_DIM, (hh + 1) * HEAD_DIM)

    def project(h):
        return _mm(h, w_ref[...].astype(BF16))

    def query_norm_rope(acc):
        for hh in range(4):
            y = _headnorm(acc[:, cols(hh)], gq_ref[...])
            zq_ref[:, cols(hh)] = (rope(y) * Q_SCALE).astype(BF16)

    @pl.when(j == 0)
    def _():
        r = row_fn(pl.program_id(0))
        h = _adaln(x_ref[...], g_ref[norm_idx:norm_idx + 1, :],
                   sh_ref[pl.ds(r, 1), :], sc_ref[pl.ds(r, 1), :]).astype(BF16)
        h_ref[...] = h
        acc_ref[...] = project(h)

    @pl.when((j == 1) | (j == 2))
    def _():
        acc_ref[...] = project(h_ref[...])

    @pl.when(j < 2)
    def _():
        query_norm_rope(acc_ref[...])

    @pl.when(j == 2)
    def _():
        acc = acc_ref[...]
        for hh in range(2):
            y = _headnorm(acc[:, cols(hh)], gk_ref[...])
            if is_ctx:
                kvn_ref[:, cols(hh)] = y
            zq_ref[:, cols(hh)] = rope(y).astype(BF16)
        va = acc[:, 256:512]
        if is_ctx:
            kvn_ref[:, 256:512] = va
        zq_ref[:, 256:512] = va.astype(BF16)

    @pl.when((j == 3) | (j == 4))
    def _():
        zq_ref[...] = (project(h_ref[...]) * Q_SCALE).astype(BF16)

    @pl.when((j >= 5) & (j <= 8))
    def _():
        acc = project(h_ref[...])
        if is_ctx:
            kvn_ref[...] = acc
        zq_ref[...] = acc.astype(BF16)

    @pl.when((j == 9) | (j == 10))
    def _():
        acc = project(h_ref[...])
        for hh in range(4):
            zq_ref[:, cols(hh)] = (rope(acc[:, cols(hh)]) * Q_SCALE).astype(BF16)

    @pl.when(j == 11)
    def _():
        acc = project(h_ref[...])
        for hh in range(2):
            y = acc[:, cols(hh)]
            if is_ctx:
                kvn_ref[:, cols(hh)] = y
            zq_ref[:, cols(hh)] = rope(y).astype(BF16)
        vc = acc[:, 256:512]
        if is_ctx:
            kvn_ref[:, 256:512] = vc
        zq_ref[:, 256:512] = vc.astype(BF16)


def _inproj(x, mods, g_norm, g_q, g_k, w_in, rope_tabs, *, l, tm, row_fn, is_ctx):
    m, d = x.shape
    depth = w_in.shape[0]
    nj = QKV_W // COL_TILE
    in_specs = [
        pl.BlockSpec((tm, d), lambda i, j: (i, 0)),
        pl.BlockSpec((None, 3, d), lambda i, j: (l, 0, 0)),
        pl.BlockSpec((None, 8, d), lambda i, j: (l, 0, 3)),
        pl.BlockSpec((None, 8, d), lambda i, j: (l, 0, 4)),
        pl.BlockSpec((None, 1, HEAD_DIM), lambda i, j: (l, 0, 0)),
        pl.BlockSpec((None, 1, HEAD_DIM), lambda i, j: (l, 0, 0)),
    ]
    args = [x, g_norm, mods, mods, g_q.reshape(depth, 1, HEAD_DIM), g_k.reshape(depth, 1, HEAD_DIM)]
    if not is_ctx:
        n_pos = rope_tabs[0].shape[0]
        per = n_pos // tm
        for tab in rope_tabs:
            in_specs.append(pl.BlockSpec((tm, HEAD_DIM), lambda i, j: (i % per, 0)))
            args.append(tab)
    in_specs.append(pl.BlockSpec((None, d, COL_TILE), lambda i, j: (l, 0, j)))
    args.append(w_in)
    out_specs = [
        pl.BlockSpec((tm, COL_TILE), lambda i, j: (i, j)),
        pl.BlockSpec((tm, d), lambda i, j: (i, 0)),
    ]
    out_shape = [jax.ShapeDtypeStruct((m, QKV_W), BF16), jax.ShapeDtypeStruct((m, d), BF16)]
    if is_ctx:
        out_specs.append(pl.BlockSpec((tm, COL_TILE), lambda i, j: (i, _kvn_block(j))))
        out_shape.append(jax.ShapeDtypeStruct((m, KVN_W), F32))
    nbytes = (2 * tm * d * 4 + 2 * tm * d * 2 + 2 * d * COL_TILE * 4 + d * COL_TILE * 2
              + 3 * tm * COL_TILE * 4 + 2 * tm * COL_TILE * 2 + 6 * tm * HEAD_DIM * 4)
    kern = functools.partial(_inproj_kernel, is_ctx=is_ctx, row_fn=row_fn, norm_idx=1)
    return pl.pallas_call(
        kern,
        grid=(m // tm, nj),
        in_specs=in_specs,
        out_specs=out_specs,
        out_shape=out_shape,
        scratch_shapes=[pltpu.VMEM((tm, COL_TILE), F32)],
        compiler_params=_params(("parallel", "arbitrary"), nbytes),
        name="inproj_ctx" if is_ctx else "inproj_lat",
    )(*args)


def _ctx_attn_kernel(z_ref, sink_ref, o_ref, *, l, seq):
    def attend(q_off, n_q, k_off, v_off, sinks):
        q = jnp.concatenate(
            [z_ref[:, q_off + h * HEAD_DIM:q_off + (h + 1) * HEAD_DIM] for h in range(n_q)], axis=0)
        k = z_ref[:, k_off:k_off + HEAD_DIM]
        v = z_ref[:, v_off:v_off + HEAD_DIM]
        s = _qk_t(q, k)
        m = jnp.max(s, axis=-1, keepdims=True)
        if sinks is not None:
            sink_col = jnp.concatenate(
                [jnp.full((seq, 1), sk, F32) for sk in sinks], axis=0)
            m = jnp.maximum(m, sink_col)
        p = jnp.exp2(s - m)
        den = jnp.sum(p, axis=-1, keepdims=True)
        if sinks is not None:
            den = den + jnp.exp2(sink_col - m)
        return _mm(p.astype(BF16), v) / den

    def store(o, o_off, n_q):
        for h in range(n_q):
            o_ref[:, o_off + h * HEAD_DIM:o_off + (h + 1) * HEAD_DIM] = (
                o[h * seq:(h + 1) * seq].astype(BF16))

    grp = N_HEADS // KV_GQA
    for g in range(KV_GQA):
        o = attend(OFF_QA + g * grp * HEAD_DIM, grp, OFF_KA + g * HEAD_DIM,
                   OFF_VA + g * HEAD_DIM, None)
        store(o, g * grp * HEAD_DIM, grp)
    for h in range(N_HEADS):
        o = attend(OFF_QB + h * HEAD_DIM, 1, OFF_KB + h * HEAD_DIM, OFF_VB + h * HEAD_DIM, None)
        store(o, 1024 + h * HEAD_DIM, 1)
    grp = N_HEADS // KV_WIN
    for g in range(KV_WIN):
        sinks = [sink_ref[l, g * grp + h] for h in range(grp)]
        o = attend(OFF_QC + g * grp * HEAD_DIM, grp, OFF_KC + g * HEAD_DIM,
                   OFF_VC + g * HEAD_DIM, sinks)
        store(o, 2048 + g * grp * HEAD_DIM, grp)


def _ctx_attention(zq, sink2, *, l, batch, seq):
    kern = functools.partial(_ctx_attn_kernel, l=l, seq=seq)
    nbytes = 2 * seq * QKV_W * 2 + 2 * seq * 3072 * 2 + 8 * 4 * seq * seq * 4
    return pl.pallas_call(
        kern,
        grid=(batch,),
        in_specs=[
            pl.BlockSpec((seq, QKV_W), lambda b: (b, 0)),
            pl.BlockSpec(memory_space=pltpu.SMEM),
        ],
        out_specs=pl.BlockSpec((seq, 3072), lambda b: (b, 0)),
        out_shape=jax.ShapeDtypeStruct((batch * seq, 3072), BF16),
        compiler_params=_params(("parallel",), nbytes),
        name="attn_ctx",
    )(zq, sink2)


def _lat_gqa_kernel(q_ref, kl_ref, vl_ref, kc_ref, vc_ref, o_ref, k_sc, v_sc, s0_sc, s1_sc,
                    m0_sc, m1_sc, *, past, n_lat, tq, chunk):
    grp = N_HEADS // KV_GQA
    total = past + n_lat
    n_chunk = total // chunk
    nq = n_lat // tq
    k_sc[0:past, :] = kc_ref[...].astype(BF16)
    k_sc[past:total, :] = kl_ref[...]
    v_sc[0:past, 0:HEAD_DIM] = vc_ref[...].astype(BF16)
    v_sc[past:total, 0:HEAD_DIM] = vl_ref[...]
    v_sc[:, HEAD_DIM:2 * HEAD_DIM] = jnp.ones((total, HEAD_DIM), BF16)

    def row0(t):
        return t * tq if isinstance(t, int) else pl.multiple_of(t * tq, tq)

    def scores(t, s_sc, m_sc):
        qn = q_ref[pl.ds(row0(t), tq), :]
        q = jnp.concatenate([qn[:, h * HEAD_DIM:(h + 1) * HEAD_DIM] for h in range(grp)], axis=0)
        m = None
        for c in range(n_chunk):
            s = _qk_t(q, k_sc[c * chunk:(c + 1) * chunk, :])
            s_sc[:, c * chunk:(c + 1) * chunk] = s
            mc = jnp.max(s, axis=-1, keepdims=True)
            m = mc if m is None else jnp.maximum(m, mc)
        m_sc[...] = m

    def values(t, s_sc, m_sc):
        m = m_sc[...]
        acc = None
        for c in range(n_chunk):
            p = jnp.exp2(s_sc[:, c * chunk:(c + 1) * chunk] - m).astype(BF16)
            term = _mm(p, v_sc[c * chunk:(c + 1) * chunk, :])
            acc = term if acc is None else acc + term
        o = acc[:, 0:HEAD_DIM] / acc[:, HEAD_DIM:2 * HEAD_DIM]
        for h in range(grp):
            o_ref[pl.ds(row0(t), tq), h * HEAD_DIM:(h + 1) * HEAD_DIM] = (
                o[h * tq:(h + 1) * tq].astype(BF16))

    scores(0, s0_sc, m0_sc)

    def body(k, carry):
        t = 2 * k
        scores(t + 1, s1_sc, m1_sc)
        values(t, s0_sc, m0_sc)
        scores(t + 2, s0_sc, m0_sc)
        values(t + 1, s1_sc, m1_sc)
        return carry

    lax.fori_loop(0, nq // 2 - 1, body, 0)
    scores(nq - 1, s1_sc, m1_sc)
    values(nq - 2, s0_sc, m0_sc)
    values(nq - 1, s1_sc, m1_sc)


def _lat_gqa_attention(zq, cache_k, cache_v, *, l, n_b, n_lat):
    past = cache_k.shape[2]
    grp = N_HEADS // KV_GQA
    tq = _divisor_tile(n_lat, 128, 16)
    assert (n_lat // tq) % 2 == 0
    total = past + n_lat
    chunk = _divisor_tile(total, 1536, 128)
    wq = grp * HEAD_DIM
    ka_blk = OFF_KA // HEAD_DIM
    va_blk = OFF_VA // HEAD_DIM
    kern = functools.partial(_lat_gqa_kernel, past=past, n_lat=n_lat, tq=tq, chunk=chunk)
    nbytes = (4 * n_lat * wq * 2 + 4 * n_lat * HEAD_DIM * 2 + 4 * past * HEAD_DIM * 4
              + total * HEAD_DIM * 2 * 3 + 2 * grp * tq * total * 4 + 4 * grp * tq * chunk * 4)
    return pl.pallas_call(
        kern,
        grid=(n_b, KV_GQA),
        in_specs=[
            pl.BlockSpec((n_lat, wq), lambda b, g: (b, g)),
            pl.BlockSpec((n_lat, HEAD_DIM), lambda b, g: (b, ka_blk + g)),
            pl.BlockSpec((n_lat, HEAD_DIM), lambda b, g: (b, va_blk + g)),
            pl.BlockSpec((None, None, past, HEAD_DIM), lambda b, g: (b, l, 0, g)),
            pl.BlockSpec((None, None, past, HEAD_DIM), lambda b, g: (b, l, 0, g)),
        ],
        out_specs=pl.BlockSpec((n_lat, wq), lambda b, g: (b, g)),
        out_shape=jax.ShapeDtypeStruct((n_b * n_lat, N_HEADS * HEAD_DIM), BF16),
        scratch_shapes=[
            pltpu.VMEM((total, HEAD_DIM), BF16),
            pltpu.VMEM((total, 2 * HEAD_DIM), BF16),
            pltpu.VMEM((grp * tq, total), F32),
            pltpu.VMEM((grp * tq, total), F32),
            pltpu.VMEM((grp * tq, 1), F32),
            pltpu.VMEM((grp * tq, 1), F32),
        ],
        compiler_params=_params(("parallel", "parallel"), nbytes),
        name="attn_lat_global",
    )(zq, zq, zq, cache_k, cache_v)


def _nat_row_tables(rpb):
    w, wc = GRID_W, NAT_COLS
    c = np.arange(w)
    cs = np.clip(c - wc // 2, 0, w - wc)
    col_ok = (c[None, :] >= cs[:, None]) & (c[None, :] < cs[:, None] + wc)
    dc = np.clip(c[None, :] - c[:, None] + wc - 1, 0, 2 * wc - 2)
    t_tab = jnp.zeros(rpb.shape[:3] + (w, w), F32)
    for d in range(2 * wc - 1):
        t_tab = jnp.where(jnp.asarray(dc == d), rpb[..., d][..., None, None].astype(F32), t_tab)
    return jnp.where(jnp.asarray(col_ok), t_tab, NEG_INF) * LOG2E


def _nat_tile_plan(rows):
    wr = NAT_ROWS
    plan = []
    for r0 in (0, NAT_QROWS, rows - NAT_QROWS):
        ws = int(np.clip(r0 - wr // 2, 0, rows - NAT_KROWS))
        cls = []
        for a in range(NAT_QROWS):
            qr = r0 + a
            rs = int(np.clip(qr - wr // 2, 0, rows - wr))
            cls.append([(ws + b) - qr + wr - 1 if rs <= ws + b < rs + wr else None
                        for b in range(NAT_KROWS)])
        plan.append(cls)
    return plan


def _nat_kernel(q_ref, k_ref, v_ref, kc_ref, vc_ref, t_ref, o_ref, kc_sc, vc_sc, bias_ref,
                s0_sc, s1_sc, m0_sc, m1_sc, *, rows):
    tq = NAT_QROWS * GRID_W
    tk = NAT_KROWS * GRID_W
    n_tiles = rows // NAT_QROWS
    past = kc_ref.shape[0]
    kc_sc[...] = kc_ref[...].astype(BF16)
    vc_sc[...] = vc_ref[...].astype(BF16)
    neg_blk = jnp.full((GRID_W, GRID_W), NEG_INF * LOG2E, F32)
    for cls, cls_plan in enumerate(_nat_tile_plan(rows)):
        for a, row_plan in enumerate(cls_plan):
            for b, dr in enumerate(row_plan):
                bias_ref[cls, a * GRID_W:(a + 1) * GRID_W, b * GRID_W:(b + 1) * GRID_W] = (
                    neg_blk if dr is None else t_ref[dr])

    def window(t):
        if isinstance(t, int):
            ws = min(max(t * NAT_QROWS - NAT_ROWS // 2, 0), rows - NAT_KROWS)
            cls = 0 if t == 0 else (2 if t == n_tiles - 1 else 1)
            return t * tq, ws * GRID_W, cls
        ws = jnp.clip(t * NAT_QROWS - NAT_ROWS // 2, 0, rows - NAT_KROWS)
        cls = jnp.where(t == 0, 0, jnp.where(t == n_tiles - 1, 2, 1))
        return pl.multiple_of(t * tq, tq), pl.multiple_of(ws * GRID_W, GRID_W), cls

    def scores(t, s_sc, m_sc):
        q0, k0, cls = window(t)
        q = q_ref[pl.ds(q0, tq), :]
        s_loc = _qk_t(q, k_ref[pl.ds(k0, tk), :]) + bias_ref[cls]
        s_ctx = _qk_t(q, kc_sc[...])
        s_sc[:, 0:tk] = s_loc
        s_sc[:, tk:tk + past] = s_ctx
        m_sc[...] = jnp.maximum(jnp.max(s_loc, axis=-1, keepdims=True),
                                jnp.max(s_ctx, axis=-1, keepdims=True))

    def values(t, s_sc, m_sc):
        q0, k0, _ = window(t)
        m = m_sc[...]
        p_loc = jnp.exp2(s_sc[:, 0:tk] - m)
        p_ctx = jnp.exp2(s_sc[:, tk:tk + past] - m)
        den = jnp.sum(p_loc, axis=-1, keepdims=True) + jnp.sum(p_ctx, axis=-1, keepdims=True)
        o = _mm(p_ctx.astype(BF16), vc_sc[...]) + _mm(p_loc.astype(BF16), v_ref[pl.ds(k0, tk), :])
        o_ref[pl.ds(q0, tq), :] = (o / den).astype(BF16)

    scores(0, s0_sc, m0_sc)

    def body(k, carry):
        t = 2 * k
        scores(t + 1, s1_sc, m1_sc)
        values(t, s0_sc, m0_sc)
        scores(t + 2, s0_sc, m0_sc)
        values(t + 1, s1_sc, m1_sc)
        return carry

    lax.fori_loop(0, n_tiles // 2 - 1, body, 0)
    scores(n_tiles - 1, s1_sc, m1_sc)
    values(n_tiles - 2, s0_sc, m0_sc)
    values(n_tiles - 1, s1_sc, m1_sc)


def _nat_attention(zq, cache_k, cache_v, row_tabs, *, l, n_b, n_lat):
    past = cache_k.shape[2]
    rows = n_lat // GRID_W
    tq = NAT_QROWS * GRID_W
    tk = NAT_KROWS * GRID_W
    n_dr = row_tabs.shape[2]
    qb, kb, vb = OFF_QB // HEAD_DIM, OFF_KB // HEAD_DIM, OFF_VB // HEAD_DIM
    kern = functools.partial(_nat_kernel, rows=rows)
    nbytes = (8 * n_lat * HEAD_DIM * 2 + 4 * past * HEAD_DIM * 4 + 2 * past * HEAD_DIM * 2
              + 2 * n_dr * GRID_W * 128 * 4 + 3 * tq * tk * 4 + 6 * tq * (tk + past) * 4)
    return pl.pallas_call(
        kern,
        grid=(n_b, N_HEADS),
        in_specs=[
            pl.BlockSpec((n_lat, HEAD_DIM), lambda b, h: (b, qb + h)),
            pl.BlockSpec((n_lat, HEAD_DIM), lambda b, h: (b, kb + h)),
            pl.BlockSpec((n_lat, HEAD_DIM), lambda b, h: (b, vb + h)),
            pl.BlockSpec((None, None, past, HEAD_DIM), lambda b, h: (b, l, 0, h)),
            pl.BlockSpec((None, None, past, HEAD_DIM), lambda b, h: (b, l, 0, h)),
            pl.BlockSpec((None, None, n_dr, GRID_W, GRID_W), lambda b, h: (l, h, 0, 0, 0)),
        ],
        out_specs=pl.BlockSpec((n_lat, HEAD_DIM), lambda b, h: (b, h)),
        out_shape=jax.ShapeDtypeStruct((n_b * n_lat, N_HEADS * HEAD_DIM), BF16),
        scratch_shapes=[pltpu.VMEM((past, HEAD_DIM), BF16), pltpu.VMEM((past, HEAD_DIM), BF16),
                        pltpu.VMEM((3, tq, tk), F32),
                        pltpu.VMEM((tq, tk + past), F32), pltpu.VMEM((tq, tk + past), F32),
                        pltpu.VMEM((tq, 1), F32), pltpu.VMEM((tq, 1), F32)],
        compiler_params=_params(("parallel", "parallel"), nbytes),
        name="attn_lat_nat",
    )(zq, zq, zq, cache_k, cache_v, row_tabs)


def _win_kernel(q_ref, k_ref, v_ref, kc_ref, vc_ref, sink_ref, o_ref, kc_sc, vc_sc, bias_sc,
                s0_sc, s1_sc, m0_sc, m1_sc, *, l, past, n_lat):
    grp = N_HEADS // KV_WIN
    band = 3 * Q_BLOCK
    n_blk = n_lat // Q_BLOCK
    g = pl.program_id(1)
    kc_sc[...] = kc_ref[...].astype(BF16)
    vc_sc[...] = vc_ref[...].astype(BF16)
    qi = lax.broadcasted_iota(jnp.int32, (Q_BLOCK, band), 0)
    kj = lax.broadcasted_iota(jnp.int32, (Q_BLOCK, band), 1)
    for cls, off in enumerate((0, -Q_BLOCK, -2 * Q_BLOCK)):
        bias_sc[cls] = jnp.where(jnp.abs(kj + off - qi) <= WIN_RADIUS, 0.0, NEG_INF).astype(F32)

    def window(n):
        if isinstance(n, int):
            start = min(max((n - 1) * Q_BLOCK, 0), n_lat - band)
            return n * Q_BLOCK, start, 0 if n == 0 else (2 if n == n_blk - 1 else 1)
        start = pl.multiple_of(jnp.clip((n - 1) * Q_BLOCK, 0, n_lat - band), Q_BLOCK)
        cls = jnp.where(n == 0, 0, jnp.where(n == n_blk - 1, 2, 1))
        return pl.multiple_of(n * Q_BLOCK, Q_BLOCK), start, cls

    def head_rows(h):
        return slice(h * Q_BLOCK, (h + 1) * Q_BLOCK)

    def scores(n, s_sc, m_sc):
        q0, start, cls = window(n)
        qn = q_ref[pl.ds(q0, Q_BLOCK), :]
        q = jnp.concatenate([qn[:, h * HEAD_DIM:(h + 1) * HEAD_DIM] for h in range(grp)], axis=0)
        s_ctx = _qk_t(q, kc_sc[...])
        s_loc = _qk_t(q, k_ref[pl.ds(start, band), :])
        bias = bias_sc[cls]
        s_sc[:, 0:past] = s_ctx
        for h in range(grp):
            sl = s_loc[head_rows(h)] + bias
            s_sc[head_rows(h), past:past + band] = sl
            sk = sink_ref[l, g * grp + h]
            m_sc[head_rows(h), :] = jnp.maximum(
                jnp.maximum(jnp.max(s_ctx[head_rows(h)], axis=-1, keepdims=True),
                            jnp.max(sl, axis=-1, keepdims=True)), sk)

    def values(n, s_sc, m_sc):
        q0, start, _ = window(n)
        m = m_sc[...]
        pc = jnp.exp2(s_sc[:, 0:past] - m)
        pw = jnp.exp2(s_sc[:, past:past + band] - m)
        den = jnp.sum(pc, axis=-1, keepdims=True) + jnp.sum(pw, axis=-1, keepdims=True)
        acc = (_mm(pc.astype(BF16), vc_sc[...])
               + _mm(pw.astype(BF16), v_ref[pl.ds(start, band), :]))
        for h in range(grp):
            sk = sink_ref[l, g * grp + h]
            dh = den[head_rows(h)] + jnp.exp2(sk - m[head_rows(h)])
            o_ref[pl.ds(q0, Q_BLOCK), h * HEAD_DIM:(h + 1) * HEAD_DIM] = (
                acc[head_rows(h)] / dh).astype(BF16)

    scores(0, s0_sc, m0_sc)

    def body(k, carry):
        n = 2 * k
        scores(n + 1, s1_sc, m1_sc)
        values(n, s0_sc, m0_sc)
        scores(n + 2, s0_sc, m0_sc)
        values(n + 1, s1_sc, m1_sc)
        return carry

    lax.fori_loop(0, n_blk // 2 - 1, body, 0)
    scores(n_blk - 1, s1_sc, m1_sc)
    values(n_blk - 2, s0_sc, m0_sc)
    values(n_blk - 1, s1_sc, m1_sc)


def _win_attention(zq, cache_k, cache_v, sink2, *, l, n_b, n_lat):
    past = cache_k.shape[2]
    grp = N_HEADS // KV_WIN
    band = 3 * Q_BLOCK
    wq = grp * HEAD_DIM
    qc_blk, kc_blk, vc_blk = OFF_QC // wq, OFF_KC // HEAD_DIM, OFF_VC // HEAD_DIM
    kern = functools.partial(_win_kernel, l=l, past=past, n_lat=n_lat)
    nbytes = (4 * n_lat * wq * 2 + 4 * n_lat * HEAD_DIM * 2 + 4 * past * HEAD_DIM * 4
              + (past + band) * HEAD_DIM * 2 * 3 + 3 * Q_BLOCK * band * 4
              + 4 * grp * Q_BLOCK * (past + band) * 4)
    return pl.pallas_call(
        kern,
        grid=(n_b, KV_WIN),
        in_specs=[
            pl.BlockSpec((n_lat, wq), lambda b, g: (b, qc_blk + g)),
            pl.BlockSpec((n_lat, HEAD_DIM), lambda b, g: (b, kc_blk + g)),
            pl.BlockSpec((n_lat, HEAD_DIM), lambda b, g: (b, vc_blk + g)),
            pl.BlockSpec((None, None, past, HEAD_DIM), lambda b, g: (b, l, 0, g)),
            pl.BlockSpec((None, None, past, HEAD_DIM), lambda b, g: (b, l, 0, g)),
            pl.BlockSpec(memory_space=pltpu.SMEM),
        ],
        out_specs=pl.BlockSpec((n_lat, wq), lambda b, g: (b, g)),
        out_shape=jax.ShapeDtypeStruct((n_b * n_lat, N_HEADS * HEAD_DIM), BF16),
        scratch_shapes=[
            pltpu.VMEM((past, HEAD_DIM), BF16),
            pltpu.VMEM((past, HEAD_DIM), BF16),
            pltpu.VMEM((3, Q_BLOCK, band), F32),
            pltpu.VMEM((grp * Q_BLOCK, past + band), F32),
            pltpu.VMEM((grp * Q_BLOCK, past + band), F32),
            pltpu.VMEM((grp * Q_BLOCK, 1), F32),
            pltpu.VMEM((grp * Q_BLOCK, 1), F32),
        ],
        compiler_params=_params(("parallel", "parallel"), nbytes),
        name="attn_lat_window",
    )(zq, zq, zq, cache_k, cache_v, sink2)


def _merge_kernel(h_ref, oa_ref, ob_ref, oc_ref, wga_ref, wgb_ref, wgc_ref,
                  wba_ref, wbb_ref, wbc_ref, m_ref):
    h = h_ref[...]
    acc = None
    for o_ref, wg_ref, wb_ref in ((oa_ref, wga_ref, wba_ref), (ob_ref, wgb_ref, wbb_ref),
                                  (oc_ref, wgc_ref, wbc_ref)):
        gate = jax.nn.sigmoid(_mm(h, wg_ref[...].astype(BF16)))
        term = gate * _mm(o_ref[...], wb_ref[...].astype(BF16))
        acc = term if acc is None else acc + term
    m_ref[...] = acc.astype(BF16)


def _merge(h, outs, w_in, w_branch, *, l, tm):
    m, d = h.shape
    bw = w_branch.shape[2]
    tn = _divisor_tile(d, 256, 128)
    g0 = OFF_GATES // tn
    per = d // tn

    def o_spec(width, first):
        return pl.BlockSpec((tm, bw), lambda i, j: (i, first))

    in_specs = [pl.BlockSpec((tm, d), lambda i, j: (i, 0))]
    args = [h]
    for o, first in outs:
        in_specs.append(o_spec(bw, first))
        args.append(o)
    for k in range(3):
        in_specs.append(pl.BlockSpec((None, d, tn), lambda i, j, k=k: (l, 0, g0 + k * per + j)))
        args.append(w_in)
    for k in range(3):
        in_specs.append(pl.BlockSpec((None, None, bw, tn), lambda i, j, k=k: (l, k, 0, j)))
        args.append(w_branch)
    nbytes = (2 * tm * d * 2 + 6 * tm * bw * 2 + 6 * d * tn * 4 + 6 * bw * tn * 4
              + 3 * (d + bw) * tn * 2 + 6 * tm * tn * 4)
    return pl.pallas_call(
        _merge_kernel,
        grid=(m // tm, d // tn),
        in_specs=in_specs,
        out_specs=pl.BlockSpec((tm, tn), lambda i, j: (i, j)),
        out_shape=jax.ShapeDtypeStruct((m, d), BF16),
        compiler_params=_params(("parallel", "parallel"), nbytes),
        name="branch_merge",
    )(*args)


def _outproj_kernel(m_ref, w_ref, x_ref, gt_ref, o_ref, *, row_fn):
    r = row_fn(pl.program_id(0))
    y = _mm(m_ref[...], w_ref[...].astype(BF16))
    o_ref[...] = x_ref[...] + gt_ref[pl.ds(r, 1), :] * y


def _outproj(mm, w_o, x, mods, *, l, tm, row_fn):
    m, d = x.shape
    tn = _divisor_tile(d, 512, 128)
    kern = functools.partial(_outproj_kernel, row_fn=row_fn)
    nbytes = 2 * tm * d * 2 + 2 * d * tn * 4 + d * tn * 2 + 5 * tm * tn * 4
    return pl.pallas_call(
        kern,
        grid=(m // tm, d // tn),
        in_specs=[
            pl.BlockSpec((tm, d), lambda i, j: (i, 0)),
            pl.BlockSpec((None, d, tn), lambda i, j: (l, 0, j)),
            pl.BlockSpec((tm, tn), lambda i, j: (i, j)),
            pl.BlockSpec((None, 8, tn), lambda i, j: (l, 0, 5 * (d // tn) + j)),
        ],
        out_specs=pl.BlockSpec((tm, tn), lambda i, j: (i, j)),
        out_shape=jax.ShapeDtypeStruct((m, d), F32),
        compiler_params=_params(("parallel", "parallel"), nbytes),
        name="out_proj",
    )(mm, w_o, x, mods)


def _pack_kv_kernel(*refs, seq):
    n_in = len(refs) - len(KVN_PARTS)
    for l in range(n_in):
        for out_ref, (off, heads) in zip(refs[n_in:], KVN_PARTS):
            for h in range(heads):
                c0 = off + h * HEAD_DIM
                out_ref[l, pl.ds(h, seq, stride=heads), :] = refs[l][:, c0:c0 + HEAD_DIM]


def _pack_context_kv(kvn, *, batch, seq):
    depth = len(kvn)
    kern = functools.partial(_pack_kv_kernel, seq=seq)
    nbytes = 2 * depth * seq * KVN_W * 4 * 2
    return pl.pallas_call(
        kern,
        grid=(batch,),
        in_specs=[pl.BlockSpec((seq, KVN_W), lambda b: (b, 0)) for _ in range(depth)],
        out_specs=[pl.BlockSpec((None, depth, seq * heads, HEAD_DIM), lambda b: (b, 0, 0, 0))
                   for _, heads in KVN_PARTS],
        out_shape=[jax.ShapeDtypeStruct((batch, depth, seq * heads, HEAD_DIM), F32)
                   for _, heads in KVN_PARTS],
        compiler_params=_params(("parallel",), nbytes),
        name="pack_context_kv",
    )(*kvn)


def _rope_tables(n_lat):
    half = HEAD_DIM // 2
    nf = half // 2
    t = jnp.arange(n_lat)
    row = (t // GRID_W).astype(F32)
    col = (t % GRID_W).astype(F32)
    inv = 1.0 / (ROPE_THETA ** (jnp.arange(nf, dtype=F32) / nf))
    ang_r = row[:, None] * inv[None, :]
    ang_c = col[:, None] * inv[None, :]
    cr, sr, cc, sc = jnp.cos(ang_r), jnp.sin(ang_r), jnp.cos(ang_c), jnp.sin(ang_c)
    zero = jnp.zeros_like(sr)
    cos = jnp.concatenate([cr, cr, cc, cc], axis=-1)
    sin_a = jnp.concatenate([-sr, zero, -sc, zero], axis=-1)
    sin_b = jnp.concatenate([zero, sr, zero, sc], axis=-1)
    return cos, sin_a, sin_b


def _trunk_layer(x, *, l, tm, row_fn, is_ctx, mods, g_norm, w_ffn1_gu, w_ffn1_down, w_in, g_q, g_k,
                 w_branch, w_o, w_ffn2_gu, w_ffn2_down, attend, rope_tabs, g_final):
    x = _ffn(x, mods, g_norm, w_ffn1_gu, w_ffn1_down, l=l, norm_idx=0, k_shift=0, k_scale=1,
             k_gate=2, tm=tm, row_fn=row_fn)
    res = _inproj(x, mods, g_norm, g_q, g_k, w_in, rope_tabs, l=l, tm=tm, row_fn=row_fn,
                  is_ctx=is_ctx)
    zq, h2 = res[0], res[1]
    outs = attend(zq)
    mm = _merge(h2, outs, w_in, w_branch, l=l, tm=tm)
    x = _outproj(mm, w_o, x, mods, l=l, tm=tm, row_fn=row_fn)
    x = _ffn(x, mods, g_norm, w_ffn2_gu, w_ffn2_down, l=l, norm_idx=2, k_shift=6, k_scale=7,
             k_gate=8, tm=tm, row_fn=row_fn, g_final=g_final)
    return x, (res[2] if is_ctx else None)


def kernel(x_prompt, x_sample, cache_k_gqa, cache_v_gqa, cache_k_nat, cache_v_nat, cache_k_win, cache_v_win, c, c_ctx, w_mod, b_mod, g_norm, w_ffn1_gu, w_ffn1_down, w_in, g_q, g_k, rpb, sink, w_branch, w_o, w_ffn2_gu, w_ffn2_down, g_final):
    batch, seq, d = x_prompt.shape
    n_b, n_lat, _ = x_sample.shape
    depth = w_mod.shape[0]
    past = cache_k_gqa.shape[2]
    rows = n_lat // GRID_W
    assert n_lat % GRID_W == 0 and rows % (2 * NAT_QROWS) == 0 and rows >= NAT_KROWS + NAT_QROWS
    assert n_lat % Q_BLOCK == 0 and n_lat >= 3 * Q_BLOCK and n_b + 1 <= 8

    m_ctx = batch * seq
    tm_ctx = _divisor_tile(m_ctx, 1024, 16)
    tm_lat = _divisor_tile(n_lat, 1024, 16)
    per_lat = n_lat // tm_lat

    cond8 = jnp.zeros((8, d), F32).at[0].set(c_ctx).at[1:1 + n_b].set(c)
    mods = _modulation(cond8, w_mod, b_mod)

    sink2 = sink * LOG2E
    rope_tabs = _rope_tables(n_lat)
    nat_tabs = _nat_row_tables(rpb)
    ck_gqa = cache_k_gqa.reshape(n_b, depth, past, KV_GQA * HEAD_DIM)
    cv_gqa = cache_v_gqa.reshape(n_b, depth, past, KV_GQA * HEAD_DIM)
    ck_nat = cache_k_nat.reshape(n_b, depth, past, N_HEADS * HEAD_DIM)
    cv_nat = cache_v_nat.reshape(n_b, depth, past, N_HEADS * HEAD_DIM)
    ck_win = cache_k_win.reshape(n_b, depth, past, KV_WIN * HEAD_DIM)
    cv_win = cache_v_win.reshape(n_b, depth, past, KV_WIN * HEAD_DIM)

    weights = dict(g_norm=g_norm, w_ffn1_gu=w_ffn1_gu, w_ffn1_down=w_ffn1_down, w_in=w_in,
                   g_q=g_q, g_k=g_k, w_branch=w_branch, w_o=w_o, w_ffn2_gu=w_ffn2_gu,
                   w_ffn2_down=w_ffn2_down, mods=mods)

    y_p = x_prompt.reshape(m_ctx, d)
    y_s = x_sample.reshape(n_b * n_lat, d)
    kvn = []
    for l in range(depth):
        def attend_ctx(zq, l=l):
            o = _ctx_attention(zq, sink2, l=l, batch=batch, seq=seq)
            return [(o, 0), (o, 1), (o, 2)]

        gf = g_final if l == depth - 1 else None
        y_p, kvn_l = _trunk_layer(y_p, l=l, tm=tm_ctx, row_fn=lambda i: 0, is_ctx=True,
                                  attend=attend_ctx, rope_tabs=None, g_final=gf, **weights)
        kvn.append(kvn_l)

        def attend_lat(zq, l=l):
            oa = _lat_gqa_attention(zq, ck_gqa, cv_gqa, l=l, n_b=n_b, n_lat=n_lat)
            ob = _nat_attention(zq, ck_nat, cv_nat, nat_tabs, l=l, n_b=n_b, n_lat=n_lat)
            oc = _win_attention(zq, ck_win, cv_win, sink2, l=l, n_b=n_b, n_lat=n_lat)
            return [(oa, 0), (ob, 0), (oc, 0)]

        y_s, _ = _trunk_layer(y_s, l=l, tm=tm_lat, row_fn=lambda i: 1 + i // per_lat, is_ctx=False,
                              attend=attend_lat, rope_tabs=rope_tabs, g_final=gf, **weights)

    y_prompt = y_p.reshape(batch, seq, d)
    y_sample = y_s.reshape(n_b, n_lat, d)

    new_kv = _pack_context_kv(kvn, batch=batch, seq=seq)
    return (y_prompt, y_sample) + tuple(
        a.reshape(batch, depth, seq, heads, HEAD_DIM) for a, (_, heads) in zip(new_kv, KVN_PARTS))
```

```python
import functools
import math

import numpy as np
import jax
import jax.numpy as jnp
from jax import lax
from jax.experimental import pallas as pl
from jax.experimental.pallas import tpu as pltpu

F32 = jnp.float32
BF16 = jnp.bfloat16

HEAD_DIM = 128
N_HEADS = 8
KV_GQA = 2
KV_WIN = 2
GRID_W = 64
NAT_ROWS = 8
NAT_COLS = 16
WIN_RADIUS = 128
Q_BLOCK = 128
ROPE_THETA = 10000.0
EPS = 1e-6
NEG_INF = -1e30
LOG2E = math.log2(math.e)
Q_SCALE = (HEAD_DIM ** -0.5) * LOG2E

OFF_QA, OFF_KA, OFF_VA = 0, 1024, 1280
OFF_QB, OFF_KB, OFF_VB = 1536, 2560, 3584
OFF_QC, OFF_KC, OFF_VC = 4608, 5632, 5888
OFF_GATES = 6144
QKV_W = 6144
COL_TILE = 512
KVN_W = 3072
KVN_PARTS = ((0, KV_GQA), (256, KV_GQA), (512, N_HEADS), (1536, N_HEADS),
             (2560, KV_WIN), (2816, KV_WIN))

V7X_VMEM_BYTES = 64 * 1024 * 1024
VMEM_CAP_BYTES = 60 * 1024 * 1024
NAT_QROWS = 4
NAT_KROWS = NAT_ROWS + NAT_QROWS


def _vmem_limit(nbytes):
    return int(min(VMEM_CAP_BYTES, max(32 * 1024 * 1024, nbytes * 3 // 2)))


def _params(sem, nbytes):
    return pltpu.CompilerParams(dimension_semantics=sem, vmem_limit_bytes=_vmem_limit(nbytes))


def _divisor_tile(n, cap, mult):
    if n <= cap:
        return n
    t = (cap // mult) * mult
    while t > mult and n % t:
        t -= mult
    assert n % t == 0, (n, cap, mult)
    return t


def _adaln(x, g, shift, scale):
    ms = jnp.mean(x * x, axis=-1, keepdims=True)
    y = x * lax.rsqrt(ms + EPS)
    return (y * g) * (1.0 + scale) + shift


def _headnorm(y, g):
    ms = jnp.mean(y * y, axis=-1, keepdims=True)
    return (y * lax.rsqrt(ms + EPS)) * g


def _silu(a):
    return a * jax.nn.sigmoid(a)


def _qk_t(q, k):
    return lax.dot_general(q, k, (((1,), (1,)), ((), ())), preferred_element_type=F32)


def _mm(a, b):
    return jnp.dot(a, b, preferred_element_type=F32)


def _mod_kernel(c_ref, w_ref, b_ref, o_ref):
    a = _silu(c_ref[...]).astype(BF16)
    o_ref[...] = _mm(a, w_ref[...].astype(BF16)) + b_ref[...]


def _modulation(cond8, w_mod, b_mod):
    depth, d, n = w_mod.shape
    tn = _divisor_tile(n, 1024, 128)
    nbytes = 2 * d * tn * 4 + d * tn * 2 + 4 * 8 * tn * 4 + 8 * d * 4
    return pl.pallas_call(
        _mod_kernel,
        grid=(depth, n // tn),
        in_specs=[
            pl.BlockSpec((8, d), lambda l, j: (0, 0)),
            pl.BlockSpec((None, d, tn), lambda l, j: (l, 0, j)),
            pl.BlockSpec((None, 1, tn), lambda l, j: (l, 0, j)),
        ],
        out_specs=pl.BlockSpec((None, 8, tn), lambda l, j: (l, 0, j)),
        out_shape=jax.ShapeDtypeStruct((depth, 8, n), F32),
        compiler_params=_params(("parallel", "parallel"), nbytes),
        name="modulation",
    )(cond8, w_mod, b_mod.reshape(depth, 1, n))


FFN_SLOTS = 2


def _ffn_kernel(*refs, l, row_fn, norm_idx, tf, d_ff, tm, final_norm, mix):
    x_hbm, g_ref, sh_ref, sc_ref, gt_ref, wgu_hbm, wd_hbm = refs[:7]
    pos = 7
    if mix:
        m_hbm, wo_hbm, gm_ref = refs[pos:pos + 3]
        pos += 3
    if final_norm:
        gf_ref = refs[pos]
        pos += 1
    o_hbm = refs[pos]
    x_buf, o_ref, h_ref, wg_buf, wu_buf, wd_buf, wsem, xsem, osem = refs[pos + 1:pos + 10]
    if mix:
        m_buf, msem = refs[pos + 10:pos + 12]
    n_chunk = d_ff // tf
    d = o_ref.shape[1]
    tn = _divisor_tile(d, 512, 128)
    two = min(tf, d)
    n_o = d // two if mix else 0
    per_tile = n_o + n_chunk
    i = pl.program_id(0)
    n_tiles = pl.num_programs(0)
    g0 = i * per_tile

    def slot_of(g):
        return lax.rem(g, FFN_SLOTS)

    def ffn_copies(c, slot):
        col = c * tf if isinstance(c, int) else pl.multiple_of(c * tf, tf)
        return (
            pltpu.make_async_copy(wgu_hbm.at[l, :, pl.ds(col, tf)], wg_buf.at[slot], wsem.at[0, slot]),
            pltpu.make_async_copy(wgu_hbm.at[l, :, pl.ds(d_ff + col, tf)], wu_buf.at[slot],
                                  wsem.at[1, slot]),
            pltpu.make_async_copy(wd_hbm.at[l, pl.ds(col, tf), :], wd_buf.at[slot], wsem.at[2, slot]),
        )

    def wo_copies(k, slot):
        return (pltpu.make_async_copy(wo_hbm.at[l, :, pl.ds(k * two, two)],
                                      wg_buf.at[slot, :, pl.ds(0, two)], wsem.at[0, slot]),)

    def first_item_copies(slot):
        return wo_copies(0, slot) if mix else ffn_copies(0, slot)

    def rows_copy(tile):
        return pltpu.make_async_copy(x_hbm.at[pl.ds(pl.multiple_of(tile * tm, tm), tm), :],
                                     x_buf, xsem.at[0])

    def mix_copy(tile):
        return pltpu.make_async_copy(m_hbm.at[pl.ds(pl.multiple_of(tile * tm, tm), tm), :],
                                     m_buf, msem.at[0])

    def out_copy(tile):
        return pltpu.make_async_copy(
            o_ref, o_hbm.at[pl.ds(pl.multiple_of(tile * tm, tm), tm), :], osem.at[0])

    def start(copies):
        for cp in copies:
            cp.start()

    def wait(copies):
        for cp in copies:
            cp.wait()

    @pl.when(i == 0)
    def _():
        rows_copy(0).start()
        if mix:
            mix_copy(0).start()
        start(first_item_copies(slot_of(g0)))

    r = row_fn(i)
    gate = 0.5 * gt_ref[pl.ds(r, 1), :]
    rows_copy(i).wait()

    if mix:
        mix_copy(i).wait()
        gate_mix = gm_ref[pl.ds(r, 1), :]
        for k in range(n_o):
            g = g0 + k
            start(wo_copies(k + 1, slot_of(g + 1)) if k + 1 < n_o else ffn_copies(0, slot_of(g + 1)))
            wait(wo_copies(k, slot_of(g)))
            cs = slice(k * two, (k + 1) * two)
            w = wg_buf[slot_of(g), :, 0:two].astype(BF16)
            x_buf[:, cs] = x_buf[:, cs] + gate_mix[:, cs] * _mm(m_buf[...], w)

        @pl.when(i + 1 < n_tiles)
        def _():
            mix_copy(i + 1).start()

    gf = g0 + n_o

    def hidden(g, h):
        slot = slot_of(g)
        a = _mm(h, wg_buf[slot].astype(BF16))
        b = _mm(h, wu_buf[slot].astype(BF16))
        return (_silu(a) * b).astype(BF16)

    def project(g, t, first):
        slot = slot_of(g)
        for n in range(d // tn):
            cs = slice(n * tn, (n + 1) * tn)
            y = gate[:, cs] * _mm(t, wd_buf[slot, :, cs].astype(BF16))
            if first:
                o_ref[:, cs] = x_buf[:, cs] + y
            else:
                o_ref[:, cs] += y

    start(ffn_copies(1, slot_of(gf + 1)))
    wait(ffn_copies(0, slot_of(gf)))
    h = _adaln(x_buf[...], g_ref[norm_idx:norm_idx + 1, :],
               sh_ref[pl.ds(r, 1), :], sc_ref[pl.ds(r, 1), :]).astype(BF16)
    h_ref[...] = h
    t0 = hidden(gf, h)

    @pl.when(i > 0)
    def _():
        out_copy(i - 1).wait()

    project(gf, t0, True)

    @pl.when(i + 1 < n_tiles)
    def _():
        rows_copy(i + 1).start()

    def body(c, carry):
        g = gf + c
        start(ffn_copies(c + 1, slot_of(g + 1)))
        wait(ffn_copies(c, slot_of(g)))
        project(g, hidden(g, h_ref[...]), False)
        return carry

    lax.fori_loop(1, n_chunk - 1, body, 0, unroll=2)

    g = gf + n_chunk - 1
    start(first_item_copies(slot_of(g + 1)))
    wait(ffn_copies(n_chunk - 1, slot_of(g)))
    project(g, hidden(g, h_ref[...]), False)

    if final_norm:
        y = o_ref[...]
        ms = jnp.mean(y * y, axis=-1, keepdims=True)
        o_ref[...] = (y * lax.rsqrt(ms + EPS)) * gf_ref[...]
    out_copy(i).start()

    @pl.when(i + 1 == n_tiles)
    def _():
        out_copy(i).wait()
        wait(first_item_copies(slot_of(g0 + per_tile)))


def _ffn(x, mods, g_norm, w_gu, w_down, *, l, norm_idx, k_shift, k_scale, k_gate, tm, row_fn,
         g_final=None, mix=None):
    m, d = x.shape
    d_ff = w_down.shape[1]
    tf = _divisor_tile(d_ff, 512, 128)
    assert d_ff // tf >= 2
    final_norm = g_final is not None
    nbytes = (2 * tm * d * 4 + 2 * tm * d * 2 + 3 * FFN_SLOTS * d * tf * 4 + 3 * d * tf * 2
              + 3 * tm * tf * 4 + 2 * tm * 512 * 4)
    kern = functools.partial(_ffn_kernel, l=l, row_fn=row_fn, norm_idx=norm_idx, tf=tf, d_ff=d_ff,
                             tm=tm, final_norm=final_norm, mix=mix is not None)
    in_specs = [
        pl.BlockSpec(memory_space=pl.ANY),
        pl.BlockSpec((None, 3, d), lambda i: (l, 0, 0)),
        pl.BlockSpec((None, 8, d), lambda i: (l, 0, k_shift)),
        pl.BlockSpec((None, 8, d), lambda i: (l, 0, k_scale)),
        pl.BlockSpec((None, 8, d), lambda i: (l, 0, k_gate)),
        pl.BlockSpec(memory_space=pl.ANY),
        pl.BlockSpec(memory_space=pl.ANY),
    ]
    args = [x, g_norm, mods, mods, mods, w_gu, w_down]
    scratch = [
        pltpu.VMEM((tm, d), F32),
        pltpu.VMEM((tm, d), F32),
        pltpu.VMEM((tm, d), BF16),
        pltpu.VMEM((FFN_SLOTS, d, tf), F32),
        pltpu.VMEM((FFN_SLOTS, d, tf), F32),
        pltpu.VMEM((FFN_SLOTS, tf, d), F32),
        pltpu.SemaphoreType.DMA((3, FFN_SLOTS)),
        pltpu.SemaphoreType.DMA((1,)),
        pltpu.SemaphoreType.DMA((1,)),
    ]
    if mix is not None:
        mm, w_o, k_gate_mix = mix
        in_specs += [pl.BlockSpec(memory_space=pl.ANY), pl.BlockSpec(memory_space=pl.ANY),
                     pl.BlockSpec((None, 8, d), lambda i: (l, 0, k_gate_mix))]
        args += [mm, w_o, mods]
        scratch += [pltpu.VMEM((tm, d), BF16), pltpu.SemaphoreType.DMA((1,))]
    if final_norm:
        in_specs.append(pl.BlockSpec((1, d), lambda i: (0, 0)))
        args.append(g_final.reshape(1, d))
    name = "ffn" + ("_mix" if mix is not None else "") + ("_final" if final_norm else "")
    return pl.pallas_call(
        kern,
        grid=(m // tm,),
        in_specs=in_specs,
        out_specs=pl.BlockSpec(memory_space=pl.ANY),
        out_shape=jax.ShapeDtypeStruct((m, d), F32),
        scratch_shapes=scratch,
        compiler_params=_params(("arbitrary",), nbytes),
        name=name,
    )(*args)


def _kvn_block(j):
    return jnp.where(j <= 2, 0, jnp.where(j >= 9, 5, jnp.clip(j - 4, 1, 4)))


def _inproj_kernel(*refs, is_ctx, row_fn, norm_idx):
    if is_ctx:
        (x_ref, g_ref, sh_ref, sc_ref, gq_ref, gk_ref, w_ref, zq_ref, h_ref, kvn_ref,
         acc_ref) = refs
    else:
        (x_ref, g_ref, sh_ref, sc_ref, gq_ref, gk_ref, cos_ref, sa_ref, sb_ref, w_ref,
         zq_ref, h_ref, acc_ref) = refs
    j = pl.program_id(1)

    def rope(y):
        if is_ctx:
            return y
        return (y * cos_ref[...] + pltpu.roll(y, 96, 1) * sa_ref[...]
                + pltpu.roll(y, 32, 1) * sb_ref[...])

    def cols(hh):
        return slice(hh * HEAD_DIM, (hh + 1) * HEAD_DIM)

    def project(h):
        return _mm(h, w_ref[...].astype(BF16))

    def query_norm_rope(acc):
        for hh in range(4):
            y = _headnorm(acc[:, cols(hh)], gq_ref[...])
            zq_ref[:, cols(hh)] = (rope(y) * Q_SCALE).astype(BF16)

    @pl.when(j == 0)
    def _():
        r = row_fn(pl.program_id(0))
        h = _adaln(x_ref[...], g_ref[norm_idx:norm_idx + 1, :],
                   sh_ref[pl.ds(r, 1), :], sc_ref[pl.ds(r, 1), :]).astype(BF16)
        h_ref[...] = h
        acc_ref[...] = project(h)

    @pl.when((j == 1) | (j == 2))
    def _():
        acc_ref[...] = project(h_ref[...])

    @pl.when(j < 2)
    def _():
        query_norm_rope(acc_ref[...])

    @pl.when(j == 2)
    def _():
        acc = acc_ref[...]
        for hh in range(2):
            y = _headnorm(acc[:, cols(hh)], gk_ref[...])
            if is_ctx:
                kvn_ref[:, cols(hh)] = y
            zq_ref[:, cols(hh)] = rope(y).astype(BF16)
        va = acc[:, 256:512]
        if is_ctx:
            kvn_ref[:, 256:512] = va
        zq_ref[:, 256:512] = va.astype(BF16)

    @pl.when((j == 3) | (j == 4))
    def _():
        zq_ref[...] = (project(h_ref[...]) * Q_SCALE).astype(BF16)

    @pl.when((j >= 5) & (j <= 8))
    def _():
        acc = project(h_ref[...])
        if is_ctx:
            kvn_ref[...] = acc
        zq_ref[...] = acc.astype(BF16)

    @pl.when((j == 9) | (j == 10))
    def _():
        acc = project(h_ref[...])
        for hh in range(4):
            zq_ref[:, cols(hh)] = (rope(acc[:, cols(hh)]) * Q_SCALE).astype(BF16)

    @pl.when(j == 11)
    def _():
        acc = project(h_ref[...])
        for hh in range(2):
            y = acc[:, cols(hh)]
            if is_ctx:
                kvn_ref[:, cols(hh)] = y
            zq_ref[:, cols(hh)] = rope(y).astype(BF16)
        vc = acc[:, 256:512]
        if is_ctx:
            kvn_ref[:, 256:512] = vc
        zq_ref[:, 256:512] = vc.astype(BF16)


def _inproj(x, mods, g_norm, g_q, g_k, w_in, rope_tabs, *, l, tm, row_fn, is_ctx):
    m, d = x.shape
    depth = w_in.shape[0]
    nj = QKV_W // COL_TILE
    in_specs = [
        pl.BlockSpec((tm, d), lambda i, j: (i, 0)),
        pl.BlockSpec((None, 3, d), lambda i, j: (l, 0, 0)),
        pl.BlockSpec((None, 8, d), lambda i, j: (l, 0, 3)),
        pl.BlockSpec((None, 8, d), lambda i, j: (l, 0, 4)),
        pl.BlockSpec((None, 1, HEAD_DIM), lambda i, j: (l, 0, 0)),
        pl.BlockSpec((None, 1, HEAD_DIM), lambda i, j: (l, 0, 0)),
    ]
    args = [x, g_norm, mods, mods, g_q.reshape(depth, 1, HEAD_DIM), g_k.reshape(depth, 1, HEAD_DIM)]
    if not is_ctx:
        n_pos = rope_tabs[0].shape[0]
        per = n_pos // tm
        for tab in rope_tabs:
            in_specs.append(pl.BlockSpec((tm, HEAD_DIM), lambda i, j: (i % per, 0)))
            args.append(tab)
    in_specs.append(pl.BlockSpec((None, d, COL_TILE), lambda i, j: (l, 0, j)))
    args.append(w_in)
    out_specs = [
        pl.BlockSpec((tm, COL_TILE), lambda i, j: (i, j)),
        pl.BlockSpec((tm, d), lambda i, j: (i, 0)),
    ]
    out_shape = [jax.ShapeDtypeStruct((m, QKV_W), BF16), jax.ShapeDtypeStruct((m, d), BF16)]
    if is_ctx:
        out_specs.append(pl.BlockSpec((tm, COL_TILE), lambda i, j: (i, _kvn_block(j))))
        out_shape.append(jax.ShapeDtypeStruct((m, KVN_W), F32))
    nbytes = (2 * tm * d * 4 + 2 * tm * d * 2 + 2 * d * COL_TILE * 4 + d * COL_TILE * 2
              + 3 * tm * COL_TILE * 4 + 2 * tm * COL_TILE * 2 + 6 * tm * HEAD_DIM * 4)
    kern = functools.partial(_inproj_kernel, is_ctx=is_ctx, row_fn=row_fn, norm_idx=1)
    return pl.pallas_call(
        kern,
        grid=(m // tm, nj),
        in_specs=in_specs,
        out_specs=out_specs,
        out_shape=out_shape,
        scratch_shapes=[pltpu.VMEM((tm, COL_TILE), F32)],
        compiler_params=_params(("parallel", "arbitrary"), nbytes),
        name="inproj_ctx" if is_ctx else "inproj_lat",
    )(*args)


def _ctx_attn_kernel(z_ref, sink_ref, o_ref, *, l, seq):
    def attend(q_off, n_q, k_off, v_off, sinks):
        q = jnp.concatenate(
            [z_ref[:, q_off + h * HEAD_DIM:q_off + (h + 1) * HEAD_DIM] for h in range(n_q)], axis=0)
        k = z_ref[:, k_off:k_off + HEAD_DIM]
        v = z_ref[:, v_off:v_off + HEAD_DIM]
        s = _qk_t(q, k)
        m = jnp.max(s, axis=-1, keepdims=True)
        if sinks is not None:
            sink_col = jnp.concatenate(
                [jnp.full((seq, 1), sk, F32) for sk in sinks], axis=0)
            m = jnp.maximum(m, sink_col)
        p = jnp.exp2(s - m)
        den = jnp.sum(p, axis=-1, keepdims=True)
        if sinks is not None:
            den = den + jnp.exp2(sink_col - m)
        return _mm(p.astype(BF16), v) / den

    def store(o, o_off, n_q):
        for h in range(n_q):
            o_ref[:, o_off + h * HEAD_DIM:o_off + (h + 1) * HEAD_DIM] = (
                o[h * seq:(h + 1) * seq].astype(BF16))

    grp = N_HEADS // KV_GQA
    for g in range(KV_GQA):
        o = attend(OFF_QA + g * grp * HEAD_DIM, grp, OFF_KA + g * HEAD_DIM,
                   OFF_VA + g * HEAD_DIM, None)
        store(o, g * grp * HEAD_DIM, grp)
    for h in range(N_HEADS):
        o = attend(OFF_QB + h * HEAD_DIM, 1, OFF_KB + h * HEAD_DIM, OFF_VB + h * HEAD_DIM, None)
        store(o, 1024 + h * HEAD_DIM, 1)
    grp = N_HEADS // KV_WIN
    for g in range(KV_WIN):
        sinks = [sink_ref[l, g * grp + h] for h in range(grp)]
        o = attend(OFF_QC + g * grp * HEAD_DIM, grp, OFF_KC + g * HEAD_DIM,
                   OFF_VC + g * HEAD_DIM, sinks)
        store(o, 2048 + g * grp * HEAD_DIM, grp)


def _ctx_attention(zq, sink2, *, l, batch, seq):
    kern = functools.partial(_ctx_attn_kernel, l=l, seq=seq)
    nbytes = 2 * seq * QKV_W * 2 + 2 * seq * 3072 * 2 + 8 * 4 * seq * seq * 4
    return pl.pallas_call(
        kern,
        grid=(batch,),
        in_specs=[
            pl.BlockSpec((seq, QKV_W), lambda b: (b, 0)),
            pl.BlockSpec(memory_space=pltpu.SMEM),
        ],
        out_specs=pl.BlockSpec((seq, 3072), lambda b: (b, 0)),
        out_shape=jax.ShapeDtypeStruct((batch * seq, 3072), BF16),
        compiler_params=_params(("parallel",), nbytes),
        name="attn_ctx",
    )(zq, sink2)


def _lat_gqa_kernel(q_ref, kl_ref, vl_ref, kc_ref, vc_ref, o_ref, k_sc, v_sc, s0_sc, s1_sc,
                    m0_sc, m1_sc, *, past, n_lat, tq, chunk):
    grp = N_HEADS // KV_GQA
    total = past + n_lat
    n_chunk = total // chunk
    nq = n_lat // tq
    k_sc[0:past, :] = kc_ref[...].astype(BF16)
    k_sc[past:total, :] = kl_ref[...]
    v_sc[0:past, 0:HEAD_DIM] = vc_ref[...].astype(BF16)
    v_sc[past:total, 0:HEAD_DIM] = vl_ref[...]
    v_sc[:, HEAD_DIM:2 * HEAD_DIM] = jnp.ones((total, HEAD_DIM), BF16)

    def row0(t):
        return t * tq if isinstance(t, int) else pl.multiple_of(t * tq, tq)

    def scores(t, s_sc, m_sc):
        qn = q_ref[pl.ds(row0(t), tq), :]
        q = jnp.concatenate([qn[:, h * HEAD_DIM:(h + 1) * HEAD_DIM] for h in range(grp)], axis=0)
        m = None
        for c in range(n_chunk):
            s = _qk_t(q, k_sc[c * chunk:(c + 1) * chunk, :])
            s_sc[:, c * chunk:(c + 1) * chunk] = s
            mc = jnp.max(s, axis=-1, keepdims=True)
            m = mc if m is None else jnp.maximum(m, mc)
        m_sc[...] = m

    def values(t, s_sc, m_sc):
        m = m_sc[...]
        acc = None
        for c in range(n_chunk):
            p = jnp.exp2(s_sc[:, c * chunk:(c + 1) * chunk] - m).astype(BF16)
            term = _mm(p, v_sc[c * chunk:(c + 1) * chunk, :])
            acc = term if acc is None else acc + term
        o = acc[:, 0:HEAD_DIM] / acc[:, HEAD_DIM:2 * HEAD_DIM]
        for h in range(grp):
            o_ref[pl.ds(row0(t), tq), h * HEAD_DIM:(h + 1) * HEAD_DIM] = (
                o[h * tq:(h + 1) * tq].astype(BF16))

    scores(0, s0_sc, m0_sc)

    def body(k, carry):
        t = 2 * k
        scores(t + 1, s1_sc, m1_sc)
        values(t, s0_sc, m0_sc)
        scores(t + 2, s0_sc, m0_sc)
        values(t + 1, s1_sc, m1_sc)
        return carry

    lax.fori_loop(0, nq // 2 - 1, body, 0)
    scores(nq - 1, s1_sc, m1_sc)
    values(nq - 2, s0_sc, m0_sc)
    values(nq - 1, s1_sc, m1_sc)


def _lat_gqa_attention(zq, cache_k, cache_v, *, l, n_b, n_lat):
    past = cache_k.shape[2]
    grp = N_HEADS // KV_GQA
    tq = _divisor_tile(n_lat, 128, 16)
    assert (n_lat // tq) % 2 == 0
    total = past + n_lat
    chunk = _divisor_tile(total, 1536, 128)
    wq = grp * HEAD_DIM
    ka_blk = OFF_KA // HEAD_DIM
    va_blk = OFF_VA // HEAD_DIM
    kern = functools.partial(_lat_gqa_kernel, past=past, n_lat=n_lat, tq=tq, chunk=chunk)
    nbytes = (4 * n_lat * wq * 2 + 4 * n_lat * HEAD_DIM * 2 + 4 * past * HEAD_DIM * 4
              + total * HEAD_DIM * 2 * 3 + 2 * grp * tq * total * 4 + 4 * grp * tq * chunk * 4)
    return pl.pallas_call(
        kern,
        grid=(n_b, KV_GQA),
        in_specs=[
            pl.BlockSpec((n_lat, wq), lambda b, g: (b, g)),
            pl.BlockSpec((n_lat, HEAD_DIM), lambda b, g: (b, ka_blk + g)),
            pl.BlockSpec((n_lat, HEAD_DIM), lambda b, g: (b, va_blk + g)),
            pl.BlockSpec((None, None, past, HEAD_DIM), lambda b, g: (b, l, 0, g)),
            pl.BlockSpec((None, None, past, HEAD_DIM), lambda b, g: (b, l, 0, g)),
        ],
        out_specs=pl.BlockSpec((n_lat, wq), lambda b, g: (b, g)),
        out_shape=jax.ShapeDtypeStruct((n_b * n_lat, N_HEADS * HEAD_DIM), BF16),
        scratch_shapes=[
            pltpu.VMEM((total, HEAD_DIM), BF16),
            pltpu.VMEM((total, 2 * HEAD_DIM), BF16),
            pltpu.VMEM((grp * tq, total), F32),
            pltpu.VMEM((grp * tq, total), F32),
            pltpu.VMEM((grp * tq, 1), F32),
            pltpu.VMEM((grp * tq, 1), F32),
        ],
        compiler_params=_params(("parallel", "parallel"), nbytes),
        name="attn_lat_global",
    )(zq, zq, zq, cache_k, cache_v)


def _nat_row_tables(rpb):
    w, wc = GRID_W, NAT_COLS
    c = np.arange(w)
    cs = np.clip(c - wc // 2, 0, w - wc)
    col_ok = (c[None, :] >= cs[:, None]) & (c[None, :] < cs[:, None] + wc)
    dc = np.clip(c[None, :] - c[:, None] + wc - 1, 0, 2 * wc - 2)
    t_tab = jnp.zeros(rpb.shape[:3] + (w, w), F32)
    for d in range(2 * wc - 1):
        t_tab = jnp.where(jnp.asarray(dc == d), rpb[..., d][..., None, None].astype(F32), t_tab)
    return jnp.where(jnp.asarray(col_ok), t_tab, NEG_INF) * LOG2E


def _nat_tile_plan(rows):
    wr = NAT_ROWS
    plan = []
    for r0 in (0, NAT_QROWS, rows - NAT_QROWS):
        ws = int(np.clip(r0 - wr // 2, 0, rows - NAT_KROWS))
        cls = []
        for a in range(NAT_QROWS):
            qr = r0 + a
            rs = int(np.clip(qr - wr // 2, 0, rows - wr))
            cls.append([(ws + b) - qr + wr - 1 if rs <= ws + b < rs + wr else None
                        for b in range(NAT_KROWS)])
        plan.append(cls)
    return plan


def _nat_kernel(q_ref, k_ref, v_ref, kc_ref, vc_ref, t_ref, o_ref, kc_sc, vc_sc, bias_ref,
                s0_sc, s1_sc, m0_sc, m1_sc, *, rows):
    tq = NAT_QROWS * GRID_W
    tk = NAT_KROWS * GRID_W
    n_tiles = rows // NAT_QROWS
    past = kc_ref.shape[0]
    kc_sc[...] = kc_ref[...].astype(BF16)
    vc_sc[...] = vc_ref[...].astype(BF16)
    neg_blk = jnp.full((GRID_W, GRID_W), NEG_INF * LOG2E, F32)
    for cls, cls_plan in enumerate(_nat_tile_plan(rows)):
        for a, row_plan in enumerate(cls_plan):
            for b, dr in enumerate(row_plan):
                bias_ref[cls, a * GRID_W:(a + 1) * GRID_W, b * GRID_W:(b + 1) * GRID_W] = (
                    neg_blk if dr is None else t_ref[dr])

    def window(t):
        if isinstance(t, int):
            ws = min(max(t * NAT_QROWS - NAT_ROWS // 2, 0), rows - NAT_KROWS)
            cls = 0 if t == 0 else (2 if t == n_tiles - 1 else 1)
            return t * tq, ws * GRID_W, cls
        ws = jnp.clip(t * NAT_QROWS - NAT_ROWS // 2, 0, rows - NAT_KROWS)
        cls = jnp.where(t == 0, 0, jnp.where(t == n_tiles - 1, 2, 1))
        return pl.multiple_of(t * tq, tq), pl.multiple_of(ws * GRID_W, GRID_W), cls

    def scores(t, s_sc, m_sc):
        q0, k0, cls = window(t)
        q = q_ref[pl.ds(q0, tq), :]
        s_loc = _qk_t(q, k_ref[pl.ds(k0, tk), :]) + bias_ref[cls]
        s_ctx = _qk_t(q, kc_sc[...])
        s_sc[:, 0:tk] = s_loc
        s_sc[:, tk:tk + past] = s_ctx
        m_sc[...] = jnp.maximum(jnp.max(s_loc, axis=-1, keepdims=True),
                                jnp.max(s_ctx, axis=-1, keepdims=True))

    def values(t, s_sc, m_sc):
        q0, k0, _ = window(t)
        m = m_sc[...]
        p_loc = jnp.exp2(s_sc[:, 0:tk] - m)
        p_ctx = jnp.exp2(s_sc[:, tk:tk + past] - m)
        den = jnp.sum(p_loc, axis=-1, keepdims=True) + jnp.sum(p_ctx, axis=-1, keepdims=True)
        o = _mm(p_ctx.astype(BF16), vc_sc[...]) + _mm(p_loc.astype(BF16), v_ref[pl.ds(k0, tk), :])
        o_ref[pl.ds(q0, tq), :] = (o / den).astype(BF16)

    scores(0, s0_sc, m0_sc)

    def body(k, carry):
        t = 2 * k
        scores(t + 1, s1_sc, m1_sc)
        values(t, s0_sc, m0_sc)
        scores(t + 2, s0_sc, m0_sc)
        values(t + 1, s1_sc, m1_sc)
        return carry

    lax.fori_loop(0, n_tiles // 2 - 1, body, 0)
    scores(n_tiles - 1, s1_sc, m1_sc)
    values(n_tiles - 2, s0_sc, m0_sc)
    values(n_tiles - 1, s1_sc, m1_sc)


def _nat_attention(zq, cache_k, cache_v, row_tabs, *, l, n_b, n_lat):
    past = cache_k.shape[2]
    rows = n_lat // GRID_W
    tq = NAT_QROWS * GRID_W
    tk = NAT_KROWS * GRID_W
    n_dr = row_tabs.shape[2]
    qb, kb, vb = OFF_QB // HEAD_DIM, OFF_KB // HEAD_DIM, OFF_VB // HEAD_DIM
    kern = functools.partial(_nat_kernel, rows=rows)
    nbytes = (8 * n_lat * HEAD_DIM * 2 + 4 * past * HEAD_DIM * 4 + 2 * past * HEAD_DIM * 2
              + 2 * n_dr * GRID_W * 128 * 4 + 3 * tq * tk * 4 + 6 * tq * (tk + past) * 4)
    return pl.pallas_call(
        kern,
        grid=(n_b, N_HEADS),
        in_specs=[
            pl.BlockSpec((n_lat, HEAD_DIM), lambda b, h: (b, qb + h)),
            pl.BlockSpec((n_lat, HEAD_DIM), lambda b, h: (b, kb + h)),
            pl.BlockSpec((n_lat, HEAD_DIM), lambda b, h: (b, vb + h)),
            pl.BlockSpec((None, None, past, HEAD_DIM), lambda b, h: (b, l, 0, h)),
            pl.BlockSpec((None, None, past, HEAD_DIM), lambda b, h: (b, l, 0, h)),
            pl.BlockSpec((None, None, n_dr, GRID_W, GRID_W), lambda b, h: (l, h, 0, 0, 0)),
        ],
        out_specs=pl.BlockSpec((n_lat, HEAD_DIM), lambda b, h: (b, h)),
        out_shape=jax.ShapeDtypeStruct((n_b * n_lat, N_HEADS * HEAD_DIM), BF16),
        scratch_shapes=[pltpu.VMEM((past, HEAD_DIM), BF16), pltpu.VMEM((past, HEAD_DIM), BF16),
                        pltpu.VMEM((3, tq, tk), F32),
                        pltpu.VMEM((tq, tk + past), F32), pltpu.VMEM((tq, tk + past), F32),
                        pltpu.VMEM((tq, 1), F32), pltpu.VMEM((tq, 1), F32)],
        compiler_params=_params(("parallel", "parallel"), nbytes),
        name="attn_lat_nat",
    )(zq, zq, zq, cache_k, cache_v, row_tabs)


def _win_kernel(q_ref, k_ref, v_ref, kc_ref, vc_ref, sink_ref, o_ref, kc_sc, vc_sc, bias_sc,
                s0_sc, s1_sc, m0_sc, m1_sc, *, l, past, n_lat):
    grp = N_HEADS // KV_WIN
    band = 3 * Q_BLOCK
    n_blk = n_lat // Q_BLOCK
    g = pl.program_id(1)
    kc_sc[...] = kc_ref[...].astype(BF16)
    vc_sc[...] = vc_ref[...].astype(BF16)
    qi = lax.broadcasted_iota(jnp.int32, (Q_BLOCK, band), 0)
    kj = lax.broadcasted_iota(jnp.int32, (Q_BLOCK, band), 1)
    for cls, off in enumerate((0, -Q_BLOCK, -2 * Q_BLOCK)):
        bias_sc[cls] = jnp.where(jnp.abs(kj + off - qi) <= WIN_RADIUS, 0.0, NEG_INF).astype(F32)

    def window(n):
        if isinstance(n, int):
            start = min(max((n - 1) * Q_BLOCK, 0), n_lat - band)
            return n * Q_BLOCK, start, 0 if n == 0 else (2 if n == n_blk - 1 else 1)
        start = pl.multiple_of(jnp.clip((n - 1) * Q_BLOCK, 0, n_lat - band), Q_BLOCK)
        cls = jnp.where(n == 0, 0, jnp.where(n == n_blk - 1, 2, 1))
        return pl.multiple_of(n * Q_BLOCK, Q_BLOCK), start, cls

    def head_rows(h):
        return slice(h * Q_BLOCK, (h + 1) * Q_BLOCK)

    def scores(n, s_sc, m_sc):
        q0, start, cls = window(n)
        qn = q_ref[pl.ds(q0, Q_BLOCK), :]
        q = jnp.concatenate([qn[:, h * HEAD_DIM:(h + 1) * HEAD_DIM] for h in range(grp)], axis=0)
        s_ctx = _qk_t(q, kc_sc[...])
        s_loc = _qk_t(q, k_ref[pl.ds(start, band), :])
        bias = bias_sc[cls]
        s_sc[:, 0:past] = s_ctx
        for h in range(grp):
            sl = s_loc[head_rows(h)] + bias
            s_sc[head_rows(h), past:past + band] = sl
            sk = sink_ref[l, g * grp + h]
            m_sc[head_rows(h), :] = jnp.maximum(
                jnp.maximum(jnp.max(s_ctx[head_rows(h)], axis=-1, keepdims=True),
                            jnp.max(sl, axis=-1, keepdims=True)), sk)

    def values(n, s_sc, m_sc):
        q0, start, _ = window(n)
        m = m_sc[...]
        pc = jnp.exp2(s_sc[:, 0:past] - m)
        pw = jnp.exp2(s_sc[:, past:past + band] - m)
        den = jnp.sum(pc, axis=-1, keepdims=True) + jnp.sum(pw, axis=-1, keepdims=True)
        acc = (_mm(pc.astype(BF16), vc_sc[...])
               + _mm(pw.astype(BF16), v_ref[pl.ds(start, band), :]))
        for h in range(grp):
            sk = sink_ref[l, g * grp + h]
            dh = den[head_rows(h)] + jnp.exp2(sk - m[head_rows(h)])
            o_ref[pl.ds(q0, Q_BLOCK), h * HEAD_DIM:(h + 1) * HEAD_DIM] = (
                acc[head_rows(h)] / dh).astype(BF16)

    scores(0, s0_sc, m0_sc)

    def body(k, carry):
        n = 2 * k
        scores(n + 1, s1_sc, m1_sc)
        values(n, s0_sc, m0_sc)
        scores(n + 2, s0_sc, m0_sc)
        values(n + 1, s1_sc, m1_sc)
        return carry

    lax.fori_loop(0, n_blk // 2 - 1, body, 0)
    scores(n_blk - 1, s1_sc, m1_sc)
    values(n_blk - 2, s0_sc, m0_sc)
    values(n_blk - 1, s1_sc, m1_sc)


def _win_attention(zq, cache_k, cache_v, sink2, *, l, n_b, n_lat):
    past = cache_k.shape[2]
    grp = N_HEADS // KV_WIN
    band = 3 * Q_BLOCK
    wq = grp * HEAD_DIM
    qc_blk, kc_blk, vc_blk = OFF_QC // wq, OFF_KC // HEAD_DIM, OFF_VC // HEAD_DIM
    kern = functools.partial(_win_kernel, l=l, past=past, n_lat=n_lat)
    nbytes = (4 * n_lat * wq * 2 + 4 * n_lat * HEAD_DIM * 2 + 4 * past * HEAD_DIM * 4
              + (past + band) * HEAD_DIM * 2 * 3 + 3 * Q_BLOCK * band * 4
              + 4 * grp * Q_BLOCK * (past + band) * 4)
    return pl.pallas_call(
        kern,
        grid=(n_b, KV_WIN),
        in_specs=[
            pl.BlockSpec((n_lat, wq), lambda b, g: (b, qc_blk + g)),
            pl.BlockSpec((n_lat, HEAD_DIM), lambda b, g: (b, kc_blk + g)),
            pl.BlockSpec((n_lat, HEAD_DIM), lambda b, g: (b, vc_blk + g)),
            pl.BlockSpec((None, None, past, HEAD_DIM), lambda b, g: (b, l, 0, g)),
            pl.BlockSpec((None, None, past, HEAD_DIM), lambda b, g: (b, l, 0, g)),
            pl.BlockSpec(memory_space=pltpu.SMEM),
        ],
        out_specs=pl.BlockSpec((n_lat, wq), lambda b, g: (b, g)),
        out_shape=jax.ShapeDtypeStruct((n_b * n_lat, N_HEADS * HEAD_DIM), BF16),
        scratch_shapes=[
            pltpu.VMEM((past, HEAD_DIM), BF16),
            pltpu.VMEM((past, HEAD_DIM), BF16),
            pltpu.VMEM((3, Q_BLOCK, band), F32),
            pltpu.VMEM((grp * Q_BLOCK, past + band), F32),
            pltpu.VMEM((grp * Q_BLOCK, past + band), F32),
            pltpu.VMEM((grp * Q_BLOCK, 1), F32),
            pltpu.VMEM((grp * Q_BLOCK, 1), F32),
        ],
        compiler_params=_params(("parallel", "parallel"), nbytes),
        name="attn_lat_window",
    )(zq, zq, zq, cache_k, cache_v, sink2)


def _merge_kernel(h_ref, oa_ref, ob_ref, oc_ref, wga_ref, wgb_ref, wgc_ref,
                  wba_ref, wbb_ref, wbc_ref, m_ref):
    h = h_ref[...]
    acc = None
    for o_ref, wg_ref, wb_ref in ((oa_ref, wga_ref, wba_ref), (ob_ref, wgb_ref, wbb_ref),
                                  (oc_ref, wgc_ref, wbc_ref)):
        gate = jax.nn.sigmoid(_mm(h, wg_ref[...].astype(BF16)))
        term = gate * _mm(o_ref[...], wb_ref[...].astype(BF16))
        acc = term if acc is None else acc + term
    m_ref[...] = acc.astype(BF16)


def _merge(h, outs, w_in, w_branch, *, l, tm):
    m, d = h.shape
    bw = w_branch.shape[2]
    tn = _divisor_tile(d, 256, 128)
    g0 = OFF_GATES // tn
    per = d // tn

    def o_spec(width, first):
        return pl.BlockSpec((tm, bw), lambda i, j: (i, first))

    in_specs = [pl.BlockSpec((tm, d), lambda i, j: (i, 0))]
    args = [h]
    for o, first in outs:
        in_specs.append(o_spec(bw, first))
        args.append(o)
    for k in range(3):
        in_specs.append(pl.BlockSpec((None, d, tn), lambda i, j, k=k: (l, 0, g0 + k * per + j)))
        args.append(w_in)
    for k in range(3):
        in_specs.append(pl.BlockSpec((None, None, bw, tn), lambda i, j, k=k: (l, k, 0, j)))
        args.append(w_branch)
    nbytes = (2 * tm * d * 2 + 6 * tm * bw * 2 + 6 * d * tn * 4 + 6 * bw * tn * 4
              + 3 * (d + bw) * tn * 2 + 6 * tm * tn * 4)
    return pl.pallas_call(
        _merge_kernel,
        grid=(m // tm, d // tn),
        in_specs=in_specs,
        out_specs=pl.BlockSpec((tm, tn), lambda i, j: (i, j)),
        out_shape=jax.ShapeDtypeStruct((m, d), BF16),
        compiler_params=_params(("parallel", "parallel"), nbytes),
        name="branch_merge",
    )(*args)


def _pack_kv_kernel(*refs, seq):
    n_in = len(refs) - len(KVN_PARTS)
    for l in range(n_in):
        for out_ref, (off, heads) in zip(refs[n_in:], KVN_PARTS):
            for h in range(heads):
                c0 = off + h * HEAD_DIM
                out_ref[l, pl.ds(h, seq, stride=heads), :] = refs[l][:, c0:c0 + HEAD_DIM]


def _pack_context_kv(kvn, *, batch, seq):
    depth = len(kvn)
    kern = functools.partial(_pack_kv_kernel, seq=seq)
    nbytes = 2 * depth * seq * KVN_W * 4 * 2
    return pl.pallas_call(
        kern,
        grid=(batch,),
        in_specs=[pl.BlockSpec((seq, KVN_W), lambda b: (b, 0)) for _ in range(depth)],
        out_specs=[pl.BlockSpec((None, depth, seq * heads, HEAD_DIM), lambda b: (b, 0, 0, 0))
                   for _, heads in KVN_PARTS],
        out_shape=[jax.ShapeDtypeStruct((batch, depth, seq * heads, HEAD_DIM), F32)
                   for _, heads in KVN_PARTS],
        compiler_params=_params(("parallel",), nbytes),
        name="pack_context_kv",
    )(*kvn)


def _rope_tables(n_lat):
    half = HEAD_DIM // 2
    nf = half // 2
    t = np.arange(n_lat)
    row = (t // GRID_W).astype(np.float64)
    col = (t % GRID_W).astype(np.float64)
    inv = 1.0 / (ROPE_THETA ** (np.arange(nf, dtype=np.float64) / nf))
    ang_r = row[:, None] * inv[None, :]
    ang_c = col[:, None] * inv[None, :]
    cr, sr, cc, sc = np.cos(ang_r), np.sin(ang_r), np.cos(ang_c), np.sin(ang_c)
    zero = np.zeros_like(sr)
    cos = np.concatenate([cr, cr, cc, cc], axis=-1)
    sin_a = np.concatenate([-sr, zero, -sc, zero], axis=-1)
    sin_b = np.concatenate([zero, sr, zero, sc], axis=-1)
    return tuple(jnp.asarray(a, F32) for a in (cos, sin_a, sin_b))


def _trunk_layer(x, *, l, tm, row_fn, is_ctx, mods, g_norm, w_ffn1_gu, w_ffn1_down, w_in, g_q, g_k,
                 w_branch, w_o, w_ffn2_gu, w_ffn2_down, attend, rope_tabs, g_final):
    x = _ffn(x, mods, g_norm, w_ffn1_gu, w_ffn1_down, l=l, norm_idx=0, k_shift=0, k_scale=1,
             k_gate=2, tm=tm, row_fn=row_fn)
    res = _inproj(x, mods, g_norm, g_q, g_k, w_in, rope_tabs, l=l, tm=tm, row_fn=row_fn,
                  is_ctx=is_ctx)
    zq, h2 = res[0], res[1]
    outs = attend(zq)
    mm = _merge(h2, outs, w_in, w_branch, l=l, tm=tm)
    x = _ffn(x, mods, g_norm, w_ffn2_gu, w_ffn2_down, l=l, norm_idx=2, k_shift=6, k_scale=7,
             k_gate=8, tm=tm, row_fn=row_fn, g_final=g_final, mix=(mm, w_o, 5))
    return x, (res[2] if is_ctx else None)


def kernel(x_prompt, x_sample, cache_k_gqa, cache_v_gqa, cache_k_nat, cache_v_nat, cache_k_win, cache_v_win, c, c_ctx, w_mod, b_mod, g_norm, w_ffn1_gu, w_ffn1_down, w_in, g_q, g_k, rpb, sink, w_branch, w_o, w_ffn2_gu, w_ffn2_down, g_final):
    batch, seq, d = x_prompt.shape
    n_b, n_lat, _ = x_sample.shape
    depth = w_mod.shape[0]
    past = cache_k_gqa.shape[2]
    rows = n_lat // GRID_W
    assert n_lat % GRID_W == 0 and rows % (2 * NAT_QROWS) == 0 and rows >= NAT_KROWS + NAT_QROWS
    assert n_lat % Q_BLOCK == 0 and n_lat >= 3 * Q_BLOCK and n_b + 1 <= 8

    m_ctx = batch * seq
    tm_ctx = _divisor_tile(m_ctx, 1024, 16)
    tm_lat = _divisor_tile(n_lat, 1024, 16)
    per_lat = n_lat // tm_lat

    cond8 = jnp.zeros((8, d), F32).at[0].set(c_ctx).at[1:1 + n_b].set(c)
    mods = _modulation(cond8, w_mod, b_mod)

    sink2 = sink * LOG2E
    rope_tabs = _rope_tables(n_lat)
    nat_tabs = _nat_row_tables(rpb)
    ck_gqa = cache_k_gqa.reshape(n_b, depth, past, KV_GQA * HEAD_DIM)
    cv_gqa = cache_v_gqa.reshape(n_b, depth, past, KV_GQA * HEAD_DIM)
    ck_nat = cache_k_nat.reshape(n_b, depth, past, N_HEADS * HEAD_DIM)
    cv_nat = cache_v_nat.reshape(n_b, depth, past, N_HEADS * HEAD_DIM)
    ck_win = cache_k_win.reshape(n_b, depth, past, KV_WIN * HEAD_DIM)
    cv_win = cache_v_win.reshape(n_b, depth, past, KV_WIN * HEAD_DIM)

    weights = dict(g_norm=g_norm, w_ffn1_gu=w_ffn1_gu, w_ffn1_down=w_ffn1_down, w_in=w_in,
                   g_q=g_q, g_k=g_k, w_branch=w_branch, w_o=w_o, w_ffn2_gu=w_ffn2_gu,
                   w_ffn2_down=w_ffn2_down, mods=mods)

    y_p = x_prompt.reshape(m_ctx, d)
    y_s = x_sample.reshape(n_b * n_lat, d)
    kvn = []
    for l in range(depth):
        def attend_ctx(zq, l=l):
            o = _ctx_attention(zq, sink2, l=l, batch=batch, seq=seq)
            return [(o, 0), (o, 1), (o, 2)]

        gf = g_final if l == depth - 1 else None
        y_p, kvn_l = _trunk_layer(y_p, l=l, tm=tm_ctx, row_fn=lambda i: 0, is_ctx=True,
                                  attend=attend_ctx, rope_tabs=None, g_final=gf, **weights)
        kvn.append(kvn_l)

        def attend_lat(zq, l=l):
            oa = _lat_gqa_attention(zq, ck_gqa, cv_gqa, l=l, n_b=n_b, n_lat=n_lat)
            ob = _nat_attention(zq, ck_nat, cv_nat, nat_tabs, l=l, n_b=n_b, n_lat=n_lat)
            oc = _win_attention(zq, ck_win, cv_win, sink2, l=l, n_b=n_b, n_lat=n_lat)
            return [(oa, 0), (ob, 0), (oc, 0)]

        y_s, _ = _trunk_layer(y_s, l=l, tm=tm_lat, row_fn=lambda i: 1 + i // per_lat, is_ctx=False,
                              attend=attend_lat, rope_tabs=rope_tabs, g_final=gf, **weights)

    y_prompt = y_p.reshape(batch, seq, d)
    y_sample = y_s.reshape(n_b, n_lat, d)

    new_kv = _pack_context_kv(kvn, batch=batch, seq=seq)
    return (y_prompt, y_sample) + tuple(
        a.reshape(batch, depth, seq, heads, HEAD_DIM) for a, (_, heads) in zip(new_kv, KVN_PARTS))
```

```python
import functools
import math

import numpy as np
import jax
import jax.numpy as jnp
from jax import lax
from jax.experimental import pallas as pl
from jax.experimental.pallas import tpu as pltpu

F32 = jnp.float32
BF16 = jnp.bfloat16

HEAD_DIM = 128
N_HEADS = 8
KV_GQA = 2
KV_WIN = 2
GRID_W = 64
NAT_ROWS = 8
NAT_COLS = 16
WIN_RADIUS = 128
Q_BLOCK = 128
ROPE_THETA = 10000.0
EPS = 1e-6
NEG_INF = -1e30
LOG2E = math.log2(math.e)
Q_SCALE = (HEAD_DIM ** -0.5) * LOG2E

OFF_QA, OFF_KA, OFF_VA = 0, 1024, 1280
OFF_QB, OFF_KB, OFF_VB = 1536, 2560, 3584
OFF_QC, OFF_KC, OFF_VC = 4608, 5632, 5888
OFF_GATES = 6144
QKV_W = 6144
COL_TILE = 512
KVN_W = 3072
KVN_PARTS = ((0, KV_GQA), (256, KV_GQA), (512, N_HEADS), (1536, N_HEADS),
             (2560, KV_WIN), (2816, KV_WIN))

V7X_VMEM_BYTES = 64 * 1024 * 1024
VMEM_CAP_BYTES = 60 * 1024 * 1024
NAT_QROWS = 4
NAT_KROWS = NAT_ROWS + NAT_QROWS


def _vmem_limit(nbytes):
    return int(min(VMEM_CAP_BYTES, max(32 * 1024 * 1024, nbytes * 3 // 2)))


def _params(sem, nbytes):
    return pltpu.CompilerParams(dimension_semantics=sem, vmem_limit_bytes=_vmem_limit(nbytes))


def _divisor_tile(n, cap, mult):
    if n <= cap:
        return n
    t = (cap // mult) * mult
    while t > mult and n % t:
        t -= mult
    assert n % t == 0, (n, cap, mult)
    return t


def _adaln(x, g, shift, scale):
    ms = jnp.mean(x * x, axis=-1, keepdims=True)
    y = x * lax.rsqrt(ms + EPS)
    return (y * g) * (1.0 + scale) + shift


def _headnorm(y, g):
    ms = jnp.mean(y * y, axis=-1, keepdims=True)
    return (y * lax.rsqrt(ms + EPS)) * g


def _silu(a):
    return a * jax.nn.sigmoid(a)


def _qk_t(q, k):
    return lax.dot_general(q, k, (((1,), (1,)), ((), ())), preferred_element_type=F32)


def _mm(a, b):
    return jnp.dot(a, b, preferred_element_type=F32)


def _mod_kernel(c_ref, w_ref, b_ref, o_ref):
    a = _silu(c_ref[...]).astype(BF16)
    o_ref[...] = _mm(a, w_ref[...].astype(BF16)) + b_ref[...]


def _modulation(cond8, w_mod, b_mod):
    depth, d, n = w_mod.shape
    tn = _divisor_tile(n, 1024, 128)
    nbytes = 2 * d * tn * 4 + d * tn * 2 + 4 * 8 * tn * 4 + 8 * d * 4
    return pl.pallas_call(
        _mod_kernel,
        grid=(depth, n // tn),
        in_specs=[
            pl.BlockSpec((8, d), lambda l, j: (0, 0)),
            pl.BlockSpec((None, d, tn), lambda l, j: (l, 0, j)),
            pl.BlockSpec((None, 1, tn), lambda l, j: (l, 0, j)),
        ],
        out_specs=pl.BlockSpec((None, 8, tn), lambda l, j: (l, 0, j)),
        out_shape=jax.ShapeDtypeStruct((depth, 8, n), F32),
        compiler_params=_params(("parallel", "parallel"), nbytes),
        name="modulation",
    )(cond8, w_mod, b_mod.reshape(depth, 1, n))


FFN_SLOTS = 2


def _ffn_kernel(*refs, l, row_fn, norm_idx, tf, d_ff, tm, final_norm, mix):
    x_hbm, g_ref, sh_ref, sc_ref, gt_ref, wgu_hbm, wd_hbm = refs[:7]
    pos = 7
    if mix:
        m_hbm, wo_hbm, gm_ref = refs[pos:pos + 3]
        pos += 3
    if final_norm:
        gf_ref = refs[pos]
        pos += 1
    o_hbm = refs[pos]
    x_buf, o_ref, h_ref, wg_buf, wu_buf, wd_buf, wsem, xsem, osem = refs[pos + 1:pos + 10]
    if mix:
        m_buf, msem = refs[pos + 10:pos + 12]
    n_chunk = d_ff // tf
    d = o_ref.shape[1]
    tn = _divisor_tile(d, 512, 128)
    two = min(tf, d)
    n_o = d // two if mix else 0
    per_tile = n_o + n_chunk
    i = pl.program_id(0)
    n_tiles = pl.num_programs(0)
    g0 = i * per_tile

    def slot_of(g):
        return lax.rem(g, FFN_SLOTS)

    def ffn_copies(c, slot):
        col = c * tf if isinstance(c, int) else pl.multiple_of(c * tf, tf)
        return (
            pltpu.make_async_copy(wgu_hbm.at[l, :, pl.ds(col, tf)], wg_buf.at[slot], wsem.at[0, slot]),
            pltpu.make_async_copy(wgu_hbm.at[l, :, pl.ds(d_ff + col, tf)], wu_buf.at[slot],
                                  wsem.at[1, slot]),
            pltpu.make_async_copy(wd_hbm.at[l, pl.ds(col, tf), :], wd_buf.at[slot], wsem.at[2, slot]),
        )

    def wo_copies(k, slot):
        return (pltpu.make_async_copy(wo_hbm.at[l, :, pl.ds(k * two, two)],
                                      wg_buf.at[slot, :, pl.ds(0, two)], wsem.at[0, slot]),)

    def first_item_copies(slot):
        return wo_copies(0, slot) if mix else ffn_copies(0, slot)

    def rows_copy(tile):
        return pltpu.make_async_copy(x_hbm.at[pl.ds(pl.multiple_of(tile * tm, tm), tm), :],
                                     x_buf, xsem.at[0])

    def mix_copy(tile):
        return pltpu.make_async_copy(m_hbm.at[pl.ds(pl.multiple_of(tile * tm, tm), tm), :],
                                     m_buf, msem.at[0])

    def out_copy(tile):
        return pltpu.make_async_copy(
            o_ref, o_hbm.at[pl.ds(pl.multiple_of(tile * tm, tm), tm), :], osem.at[0])

    def start(copies):
        for cp in copies:
            cp.start()

    def wait(copies):
        for cp in copies:
            cp.wait()

    @pl.when(i == 0)
    def _():
        rows_copy(0).start()
        if mix:
            mix_copy(0).start()
        start(first_item_copies(slot_of(g0)))

    r = row_fn(i)
    gate = 0.5 * gt_ref[pl.ds(r, 1), :]
    rows_copy(i).wait()

    if mix:
        mix_copy(i).wait()
        gate_mix = gm_ref[pl.ds(r, 1), :]
        first_hidden = ffn_copies(0, slot_of(g0 + n_o))
        start(first_hidden[1:])
        for k in range(n_o):
            g = g0 + k
            start(wo_copies(k + 1, slot_of(g + 1)) if k + 1 < n_o else first_hidden[:1])
            wait(wo_copies(k, slot_of(g)))
            cs = slice(k * two, (k + 1) * two)
            w = wg_buf[slot_of(g), :, 0:two].astype(BF16)
            x_buf[:, cs] = x_buf[:, cs] + gate_mix[:, cs] * _mm(m_buf[...], w)

        @pl.when(i + 1 < n_tiles)
        def _():
            mix_copy(i + 1).start()

    gf = g0 + n_o

    def hidden(g, h):
        slot = slot_of(g)
        a = _mm(h, wg_buf[slot].astype(BF16))
        b = _mm(h, wu_buf[slot].astype(BF16))
        return (_silu(a) * b).astype(BF16)

    def project(g, t, first):
        slot = slot_of(g)
        for n in range(d // tn):
            cs = slice(n * tn, (n + 1) * tn)
            y = gate[:, cs] * _mm(t, wd_buf[slot, :, cs].astype(BF16))
            if first:
                o_ref[:, cs] = x_buf[:, cs] + y
            else:
                o_ref[:, cs] += y

    start(ffn_copies(1, slot_of(gf + 1)))
    wait(ffn_copies(0, slot_of(gf)))
    h = _adaln(x_buf[...], g_ref[norm_idx:norm_idx + 1, :],
               sh_ref[pl.ds(r, 1), :], sc_ref[pl.ds(r, 1), :]).astype(BF16)
    h_ref[...] = h
    t0 = hidden(gf, h)

    @pl.when(i > 0)
    def _():
        out_copy(i - 1).wait()

    project(gf, t0, True)

    @pl.when(i + 1 < n_tiles)
    def _():
        rows_copy(i + 1).start()

    def body(c, carry):
        g = gf + c
        start(ffn_copies(c + 1, slot_of(g + 1)))
        wait(ffn_copies(c, slot_of(g)))
        project(g, hidden(g, h_ref[...]), False)
        return carry

    lax.fori_loop(1, n_chunk - 1, body, 0, unroll=2)

    g = gf + n_chunk - 1
    start(first_item_copies(slot_of(g + 1)))
    wait(ffn_copies(n_chunk - 1, slot_of(g)))
    project(g, hidden(g, h_ref[...]), False)

    if final_norm:
        y = o_ref[...]
        ms = jnp.mean(y * y, axis=-1, keepdims=True)
        o_ref[...] = (y * lax.rsqrt(ms + EPS)) * gf_ref[...]
    out_copy(i).start()

    @pl.when(i + 1 == n_tiles)
    def _():
        out_copy(i).wait()
        wait(first_item_copies(slot_of(g0 + per_tile)))


def _ffn(x, mods, g_norm, w_gu, w_down, *, l, norm_idx, k_shift, k_scale, k_gate, tm, row_fn,
         g_final=None, mix=None):
    m, d = x.shape
    d_ff = w_down.shape[1]
    tf = _divisor_tile(d_ff, 512, 128)
    assert d_ff // tf >= 2
    final_norm = g_final is not None
    nbytes = (2 * tm * d * 4 + 2 * tm * d * 2 + 3 * FFN_SLOTS * d * tf * 4 + 3 * d * tf * 2
              + 3 * tm * tf * 4 + 2 * tm * 512 * 4)
    kern = functools.partial(_ffn_kernel, l=l, row_fn=row_fn, norm_idx=norm_idx, tf=tf, d_ff=d_ff,
                             tm=tm, final_norm=final_norm, mix=mix is not None)
    in_specs = [
        pl.BlockSpec(memory_space=pl.ANY),
        pl.BlockSpec((None, 3, d), lambda i: (l, 0, 0)),
        pl.BlockSpec((None, 8, d), lambda i: (l, 0, k_shift)),
        pl.BlockSpec((None, 8, d), lambda i: (l, 0, k_scale)),
        pl.BlockSpec((None, 8, d), lambda i: (l, 0, k_gate)),
        pl.BlockSpec(memory_space=pl.ANY),
        pl.BlockSpec(memory_space=pl.ANY),
    ]
    args = [x, g_norm, mods, mods, mods, w_gu, w_down]
    scratch = [
        pltpu.VMEM((tm, d), F32),
        pltpu.VMEM((tm, d), F32),
        pltpu.VMEM((tm, d), BF16),
        pltpu.VMEM((FFN_SLOTS, d, tf), F32),
        pltpu.VMEM((FFN_SLOTS, d, tf), F32),
        pltpu.VMEM((FFN_SLOTS, tf, d), F32),
        pltpu.SemaphoreType.DMA((3, FFN_SLOTS)),
        pltpu.SemaphoreType.DMA((1,)),
        pltpu.SemaphoreType.DMA((1,)),
    ]
    if mix is not None:
        mm, w_o, k_gate_mix = mix
        in_specs += [pl.BlockSpec(memory_space=pl.ANY), pl.BlockSpec(memory_space=pl.ANY),
                     pl.BlockSpec((None, 8, d), lambda i: (l, 0, k_gate_mix))]
        args += [mm, w_o, mods]
        scratch += [pltpu.VMEM((tm, d), BF16), pltpu.SemaphoreType.DMA((1,))]
    if final_norm:
        in_specs.append(pl.BlockSpec((1, d), lambda i: (0, 0)))
        args.append(g_final.reshape(1, d))
    name = "ffn" + ("_mix" if mix is not None else "") + ("_final" if final_norm else "")
    return pl.pallas_call(
        kern,
        grid=(m // tm,),
        in_specs=in_specs,
        out_specs=pl.BlockSpec(memory_space=pl.ANY),
        out_shape=jax.ShapeDtypeStruct((m, d), F32),
        scratch_shapes=scratch,
        compiler_params=_params(("arbitrary",), nbytes),
        name=name,
    )(*args)


def _kvn_block(j):
    return jnp.where(j <= 2, 0, jnp.where(j >= 9, 5, jnp.clip(j - 4, 1, 4)))


def _inproj_kernel(*refs, is_ctx, row_fn, norm_idx):
    if is_ctx:
        (x_ref, g_ref, sh_ref, sc_ref, gq_ref, gk_ref, w_ref, zq_ref, h_ref, kvn_ref,
         acc_ref) = refs
    else:
        (x_ref, g_ref, sh_ref, sc_ref, gq_ref, gk_ref, cos_ref, sa_ref, sb_ref, w_ref,
         zq_ref, h_ref, acc_ref) = refs
    j = pl.program_id(1)

    def rope(y):
        if is_ctx:
            return y
        return (y * cos_ref[...] + pltpu.roll(y, 96, 1) * sa_ref[...]
                + pltpu.roll(y, 32, 1) * sb_ref[...])

    def cols(hh):
        return slice(hh * HEAD_DIM, (hh + 1) * HEAD_DIM)

    def project(h):
        return _mm(h, w_ref[...].astype(BF16))

    def query_norm_rope(acc):
        for hh in range(4):
            y = _headnorm(acc[:, cols(hh)], gq_ref[...])
            zq_ref[:, cols(hh)] = (rope(y) * Q_SCALE).astype(BF16)

    @pl.when(j == 0)
    def _():
        r = row_fn(pl.program_id(0))
        h = _adaln(x_ref[...], g_ref[norm_idx:norm_idx + 1, :],
                   sh_ref[pl.ds(r, 1), :], sc_ref[pl.ds(r, 1), :]).astype(BF16)
        h_ref[...] = h
        acc_ref[...] = project(h)

    @pl.when((j == 1) | (j == 2))
    def _():
        acc_ref[...] = project(h_ref[...])

    @pl.when(j < 2)
    def _():
        query_norm_rope(acc_ref[...])

    @pl.when(j == 2)
    def _():
        acc = acc_ref[...]
        for hh in range(2):
            y = _headnorm(acc[:, cols(hh)], gk_ref[...])
            if is_ctx:
                kvn_ref[:, cols(hh)] = y
            zq_ref[:, cols(hh)] = rope(y).astype(BF16)
        va = acc[:, 256:512]
        if is_ctx:
            kvn_ref[:, 256:512] = va
        zq_ref[:, 256:512] = va.astype(BF16)

    @pl.when((j == 3) | (j == 4))
    def _():
        zq_ref[...] = (project(h_ref[...]) * Q_SCALE).astype(BF16)

    @pl.when((j >= 5) & (j <= 8))
    def _():
        acc = project(h_ref[...])
        if is_ctx:
            kvn_ref[...] = acc
        zq_ref[...] = acc.astype(BF16)

    @pl.when((j == 9) | (j == 10))
    def _():
        acc = project(h_ref[...])
        for hh in range(4):
            zq_ref[:, cols(hh)] = (rope(acc[:, cols(hh)]) * Q_SCALE).astype(BF16)

    @pl.when(j == 11)
    def _():
        acc = project(h_ref[...])
        for hh in range(2):
            y = acc[:, cols(hh)]
            if is_ctx:
                kvn_ref[:, cols(hh)] = y
            zq_ref[:, cols(hh)] = rope(y).astype(BF16)
        vc = acc[:, 256:512]
        if is_ctx:
            kvn_ref[:, 256:512] = vc
        zq_ref[:, 256:512] = vc.astype(BF16)


def _inproj(x, mods, g_norm, g_q, g_k, w_in, rope_tabs, *, l, tm, row_fn, is_ctx):
    m, d = x.shape
    depth = w_in.shape[0]
    nj = QKV_W // COL_TILE
    in_specs = [
        pl.BlockSpec((tm, d), lambda i, j: (i, 0)),
        pl.BlockSpec((None, 3, d), lambda i, j: (l, 0, 0)),
        pl.BlockSpec((None, 8, d), lambda i, j: (l, 0, 3)),
        pl.BlockSpec((None, 8, d), lambda i, j: (l, 0, 4)),
        pl.BlockSpec((None, 1, HEAD_DIM), lambda i, j: (l, 0, 0)),
        pl.BlockSpec((None, 1, HEAD_DIM), lambda i, j: (l, 0, 0)),
    ]
    args = [x, g_norm, mods, mods, g_q.reshape(depth, 1, HEAD_DIM), g_k.reshape(depth, 1, HEAD_DIM)]
    if not is_ctx:
        n_pos = rope_tabs[0].shape[0]
        per = n_pos // tm
        for tab in rope_tabs:
            in_specs.append(pl.BlockSpec((tm, HEAD_DIM), lambda i, j: (i % per, 0)))
            args.append(tab)
    in_specs.append(pl.BlockSpec((None, d, COL_TILE), lambda i, j: (l, 0, j)))
    args.append(w_in)
    out_specs = [
        pl.BlockSpec((tm, COL_TILE), lambda i, j: (i, j)),
        pl.BlockSpec((tm, d), lambda i, j: (i, 0)),
    ]
    out_shape = [jax.ShapeDtypeStruct((m, QKV_W), BF16), jax.ShapeDtypeStruct((m, d), BF16)]
    if is_ctx:
        out_specs.append(pl.BlockSpec((tm, COL_TILE), lambda i, j: (i, _kvn_block(j))))
        out_shape.append(jax.ShapeDtypeStruct((m, KVN_W), F32))
    nbytes = (2 * tm * d * 4 + 2 * tm * d * 2 + 2 * d * COL_TILE * 4 + d * COL_TILE * 2
              + 3 * tm * COL_TILE * 4 + 2 * tm * COL_TILE * 2 + 6 * tm * HEAD_DIM * 4)
    kern = functools.partial(_inproj_kernel, is_ctx=is_ctx, row_fn=row_fn, norm_idx=1)
    return pl.pallas_call(
        kern,
        grid=(m // tm, nj),
        in_specs=in_specs,
        out_specs=out_specs,
        out_shape=out_shape,
        scratch_shapes=[pltpu.VMEM((tm, COL_TILE), F32)],
        compiler_params=_params(("parallel", "arbitrary"), nbytes),
        name="inproj_ctx" if is_ctx else "inproj_lat",
    )(*args)


def _ctx_attn_kernel(z_ref, sink_ref, o_ref, *, l, seq):
    def attend(q_off, n_q, k_off, v_off, sinks):
        q = jnp.concatenate(
            [z_ref[:, q_off + h * HEAD_DIM:q_off + (h + 1) * HEAD_DIM] for h in range(n_q)], axis=0)
        k = z_ref[:, k_off:k_off + HEAD_DIM]
        v = z_ref[:, v_off:v_off + HEAD_DIM]
        s = _qk_t(q, k)
        m = jnp.max(s, axis=-1, keepdims=True)
        if sinks is not None:
            sink_col = jnp.concatenate(
                [jnp.full((seq, 1), sk, F32) for sk in sinks], axis=0)
            m = jnp.maximum(m, sink_col)
        p = jnp.exp2(s - m)
        den = jnp.sum(p, axis=-1, keepdims=True)
        if sinks is not None:
            den = den + jnp.exp2(sink_col - m)
        return _mm(p.astype(BF16), v) / den

    def store(o, o_off, n_q):
        for h in range(n_q):
            o_ref[:, o_off + h * HEAD_DIM:o_off + (h + 1) * HEAD_DIM] = (
                o[h * seq:(h + 1) * seq].astype(BF16))

    grp = N_HEADS // KV_GQA
    for g in range(KV_GQA):
        o = attend(OFF_QA + g * grp * HEAD_DIM, grp, OFF_KA + g * HEAD_DIM,
                   OFF_VA + g * HEAD_DIM, None)
        store(o, g * grp * HEAD_DIM, grp)
    for h in range(N_HEADS):
        o = attend(OFF_QB + h * HEAD_DIM, 1, OFF_KB + h * HEAD_DIM, OFF_VB + h * HEAD_DIM, None)
        store(o, 1024 + h * HEAD_DIM, 1)
    grp = N_HEADS // KV_WIN
    for g in range(KV_WIN):
        sinks = [sink_ref[l, g * grp + h] for h in range(grp)]
        o = attend(OFF_QC + g * grp * HEAD_DIM, grp, OFF_KC + g * HEAD_DIM,
                   OFF_VC + g * HEAD_DIM, sinks)
        store(o, 2048 + g * grp * HEAD_DIM, grp)


def _ctx_attention(zq, sink2, *, l, batch, seq):
    kern = functools.partial(_ctx_attn_kernel, l=l, seq=seq)
    nbytes = 2 * seq * QKV_W * 2 + 2 * seq * 3072 * 2 + 8 * 4 * seq * seq * 4
    return pl.pallas_call(
        kern,
        grid=(batch,),
        in_specs=[
            pl.BlockSpec((seq, QKV_W), lambda b: (b, 0)),
            pl.BlockSpec(memory_space=pltpu.SMEM),
        ],
        out_specs=pl.BlockSpec((seq, 3072), lambda b: (b, 0)),
        out_shape=jax.ShapeDtypeStruct((batch * seq, 3072), BF16),
        compiler_params=_params(("parallel",), nbytes),
        name="attn_ctx",
    )(zq, sink2)


def _lat_gqa_kernel(q_ref, kl_ref, vl_ref, kc_ref, vc_ref, o_ref, k_sc, v_sc, s0_sc, s1_sc,
                    m0_sc, m1_sc, *, past, n_lat, tq, chunk):
    grp = N_HEADS // KV_GQA
    total = past + n_lat
    n_chunk = total // chunk
    nq = n_lat // tq
    k_sc[0:past, :] = kc_ref[...].astype(BF16)
    k_sc[past:total, :] = kl_ref[...]
    v_sc[0:past, 0:HEAD_DIM] = vc_ref[...].astype(BF16)
    v_sc[past:total, 0:HEAD_DIM] = vl_ref[...]
    v_sc[:, HEAD_DIM:2 * HEAD_DIM] = jnp.ones((total, HEAD_DIM), BF16)

    def row0(t):
        return t * tq if isinstance(t, int) else pl.multiple_of(t * tq, tq)

    def scores(t, s_sc, m_sc):
        qn = q_ref[pl.ds(row0(t), tq), :]
        q = jnp.concatenate([qn[:, h * HEAD_DIM:(h + 1) * HEAD_DIM] for h in range(grp)], axis=0)
        m = None
        for c in range(n_chunk):
            s = _qk_t(q, k_sc[c * chunk:(c + 1) * chunk, :])
            s_sc[:, c * chunk:(c + 1) * chunk] = s
            mc = jnp.max(s, axis=-1, keepdims=True)
            m = mc if m is None else jnp.maximum(m, mc)
        m_sc[...] = m

    def values(t, s_sc, m_sc):
        m = m_sc[...]
        acc = None
        for c in range(n_chunk):
            p = jnp.exp2(s_sc[:, c * chunk:(c + 1) * chunk] - m).astype(BF16)
            term = _mm(p, v_sc[c * chunk:(c + 1) * chunk, :])
            acc = term if acc is None else acc + term
        o = acc[:, 0:HEAD_DIM] / acc[:, HEAD_DIM:2 * HEAD_DIM]
        for h in range(grp):
            o_ref[pl.ds(row0(t), tq), h * HEAD_DIM:(h + 1) * HEAD_DIM] = (
                o[h * tq:(h + 1) * tq].astype(BF16))

    scores(0, s0_sc, m0_sc)

    def body(k, carry):
        t = 2 * k
        scores(t + 1, s1_sc, m1_sc)
        values(t, s0_sc, m0_sc)
        scores(t + 2, s0_sc, m0_sc)
        values(t + 1, s1_sc, m1_sc)
        return carry

    lax.fori_loop(0, nq // 2 - 1, body, 0)
    scores(nq - 1, s1_sc, m1_sc)
    values(nq - 2, s0_sc, m0_sc)
    values(nq - 1, s1_sc, m1_sc)


def _lat_gqa_attention(zq, cache_k, cache_v, *, l, n_b, n_lat):
    past = cache_k.shape[2]
    grp = N_HEADS // KV_GQA
    tq = _divisor_tile(n_lat, 128, 16)
    assert (n_lat // tq) % 2 == 0
    total = past + n_lat
    chunk = _divisor_tile(total, 1536, 128)
    wq = grp * HEAD_DIM
    ka_blk = OFF_KA // HEAD_DIM
    va_blk = OFF_VA // HEAD_DIM
    kern = functools.partial(_lat_gqa_kernel, past=past, n_lat=n_lat, tq=tq, chunk=chunk)
    nbytes = (4 * n_lat * wq * 2 + 4 * n_lat * HEAD_DIM * 2 + 4 * past * HEAD_DIM * 4
              + total * HEAD_DIM * 2 * 3 + 2 * grp * tq * total * 4 + 4 * grp * tq * chunk * 4)
    return pl.pallas_call(
        kern,
        grid=(n_b, KV_GQA),
        in_specs=[
            pl.BlockSpec((n_lat, wq), lambda b, g: (b, g)),
            pl.BlockSpec((n_lat, HEAD_DIM), lambda b, g: (b, ka_blk + g)),
            pl.BlockSpec((n_lat, HEAD_DIM), lambda b, g: (b, va_blk + g)),
            pl.BlockSpec((None, None, past, HEAD_DIM), lambda b, g: (b, l, 0, g)),
            pl.BlockSpec((None, None, past, HEAD_DIM), lambda b, g: (b, l, 0, g)),
        ],
        out_specs=pl.BlockSpec((n_lat, wq), lambda b, g: (b, g)),
        out_shape=jax.ShapeDtypeStruct((n_b * n_lat, N_HEADS * HEAD_DIM), BF16),
        scratch_shapes=[
            pltpu.VMEM((total, HEAD_DIM), BF16),
            pltpu.VMEM((total, 2 * HEAD_DIM), BF16),
            pltpu.VMEM((grp * tq, total), F32),
            pltpu.VMEM((grp * tq, total), F32),
            pltpu.VMEM((grp * tq, 1), F32),
            pltpu.VMEM((grp * tq, 1), F32),
        ],
        compiler_params=_params(("parallel", "parallel"), nbytes),
        name="attn_lat_global",
    )(zq, zq, zq, cache_k, cache_v)


def _nat_row_tables(rpb):
    w, wc = GRID_W, NAT_COLS
    c = np.arange(w)
    cs = np.clip(c - wc // 2, 0, w - wc)
    col_ok = (c[None, :] >= cs[:, None]) & (c[None, :] < cs[:, None] + wc)
    dc = np.clip(c[None, :] - c[:, None] + wc - 1, 0, 2 * wc - 2)
    t_tab = jnp.zeros(rpb.shape[:3] + (w, w), F32)
    for d in range(2 * wc - 1):
        t_tab = jnp.where(jnp.asarray(dc == d), rpb[..., d][..., None, None].astype(F32), t_tab)
    return jnp.where(jnp.asarray(col_ok), t_tab, NEG_INF) * LOG2E


def _nat_tile_plan(rows):
    wr = NAT_ROWS
    plan = []
    for r0 in (0, NAT_QROWS, rows - NAT_QROWS):
        ws = int(np.clip(r0 - wr // 2, 0, rows - NAT_KROWS))
        cls = []
        for a in range(NAT_QROWS):
            qr = r0 + a
            rs = int(np.clip(qr - wr // 2, 0, rows - wr))
            cls.append([(ws + b) - qr + wr - 1 if rs <= ws + b < rs + wr else None
                        for b in range(NAT_KROWS)])
        plan.append(cls)
    return plan


def _nat_kernel(q_ref, k_ref, v_ref, kc_ref, vc_ref, t_ref, o_ref, kc_sc, vc_sc, bias_ref,
                s0_sc, s1_sc, m0_sc, m1_sc, *, rows):
    tq = NAT_QROWS * GRID_W
    tk = NAT_KROWS * GRID_W
    n_tiles = rows // NAT_QROWS
    past = kc_ref.shape[0]
    kc_sc[...] = kc_ref[...].astype(BF16)
    vc_sc[...] = vc_ref[...].astype(BF16)
    neg_blk = jnp.full((GRID_W, GRID_W), NEG_INF * LOG2E, F32)
    for cls, cls_plan in enumerate(_nat_tile_plan(rows)):
        for a, row_plan in enumerate(cls_plan):
            for b, dr in enumerate(row_plan):
                bias_ref[cls, a * GRID_W:(a + 1) * GRID_W, b * GRID_W:(b + 1) * GRID_W] = (
                    neg_blk if dr is None else t_ref[dr])

    def window(t):
        if isinstance(t, int):
            ws = min(max(t * NAT_QROWS - NAT_ROWS // 2, 0), rows - NAT_KROWS)
            cls = 0 if t == 0 else (2 if t == n_tiles - 1 else 1)
            return t * tq, ws * GRID_W, cls
        ws = jnp.clip(t * NAT_QROWS - NAT_ROWS // 2, 0, rows - NAT_KROWS)
        cls = jnp.where(t == 0, 0, jnp.where(t == n_tiles - 1, 2, 1))
        return pl.multiple_of(t * tq, tq), pl.multiple_of(ws * GRID_W, GRID_W), cls

    def scores(t, s_sc, m_sc):
        q0, k0, cls = window(t)
        q = q_ref[pl.ds(q0, tq), :]
        s_loc = _qk_t(q, k_ref[pl.ds(k0, tk), :]) + bias_ref[cls]
        s_ctx = _qk_t(q, kc_sc[...])
        s_sc[:, 0:tk] = s_loc
        s_sc[:, tk:tk + past] = s_ctx
        m_sc[...] = jnp.maximum(jnp.max(s_loc, axis=-1, keepdims=True),
                                jnp.max(s_ctx, axis=-1, keepdims=True))

    def values(t, s_sc, m_sc):
        q0, k0, _ = window(t)
        m = m_sc[...]
        p_loc = jnp.exp2(s_sc[:, 0:tk] - m)
        p_ctx = jnp.exp2(s_sc[:, tk:tk + past] - m)
        den = jnp.sum(p_loc, axis=-1, keepdims=True) + jnp.sum(p_ctx, axis=-1, keepdims=True)
        o = _mm(p_ctx.astype(BF16), vc_sc[...]) + _mm(p_loc.astype(BF16), v_ref[pl.ds(k0, tk), :])
        o_ref[pl.ds(q0, tq), :] = (o / den).astype(BF16)

    scores(0, s0_sc, m0_sc)

    def body(k, carry):
        t = 2 * k
        scores(t + 1, s1_sc, m1_sc)
        values(t, s0_sc, m0_sc)
        scores(t + 2, s0_sc, m0_sc)
        values(t + 1, s1_sc, m1_sc)
        return carry

    lax.fori_loop(0, n_tiles // 2 - 1, body, 0)
    scores(n_tiles - 1, s1_sc, m1_sc)
    values(n_tiles - 2, s0_sc, m0_sc)
    values(n_tiles - 1, s1_sc, m1_sc)


def _nat_attention(zq, cache_k, cache_v, row_tabs, *, l, n_b, n_lat):
    past = cache_k.shape[2]
    rows = n_lat // GRID_W
    tq = NAT_QROWS * GRID_W
    tk = NAT_KROWS * GRID_W
    n_dr = row_tabs.shape[2]
    qb, kb, vb = OFF_QB // HEAD_DIM, OFF_KB // HEAD_DIM, OFF_VB // HEAD_DIM
    kern = functools.partial(_nat_kernel, rows=rows)
    nbytes = (8 * n_lat * HEAD_DIM * 2 + 4 * past * HEAD_DIM * 4 + 2 * past * HEAD_DIM * 2
              + 2 * n_dr * GRID_W * 128 * 4 + 3 * tq * tk * 4 + 6 * tq * (tk + past) * 4)
    return pl.pallas_call(
        kern,
        grid=(n_b, N_HEADS),
        in_specs=[
            pl.BlockSpec((n_lat, HEAD_DIM), lambda b, h: (b, qb + h)),
            pl.BlockSpec((n_lat, HEAD_DIM), lambda b, h: (b, kb + h)),
            pl.BlockSpec((n_lat, HEAD_DIM), lambda b, h: (b, vb + h)),
            pl.BlockSpec((None, None, past, HEAD_DIM), lambda b, h: (b, l, 0, h)),
            pl.BlockSpec((None, None, past, HEAD_DIM), lambda b, h: (b, l, 0, h)),
            pl.BlockSpec((None, None, n_dr, GRID_W, GRID_W), lambda b, h: (l, h, 0, 0, 0)),
        ],
        out_specs=pl.BlockSpec((n_lat, HEAD_DIM), lambda b, h: (b, h)),
        out_shape=jax.ShapeDtypeStruct((n_b * n_lat, N_HEADS * HEAD_DIM), BF16),
        scratch_shapes=[pltpu.VMEM((past, HEAD_DIM), BF16), pltpu.VMEM((past, HEAD_DIM), BF16),
                        pltpu.VMEM((3, tq, tk), F32),
                        pltpu.VMEM((tq, tk + past), F32), pltpu.VMEM((tq, tk + past), F32),
                        pltpu.VMEM((tq, 1), F32), pltpu.VMEM((tq, 1), F32)],
        compiler_params=_params(("parallel", "parallel"), nbytes),
        name="attn_lat_nat",
    )(zq, zq, zq, cache_k, cache_v, row_tabs)


def _win_kernel(q_ref, k_ref, v_ref, kc_ref, vc_ref, sink_ref, o_ref, kc_sc, vc_sc, bias_sc,
                s0_sc, s1_sc, m0_sc, m1_sc, *, l, past, n_lat):
    grp = N_HEADS // KV_WIN
    band = 3 * Q_BLOCK
    n_blk = n_lat // Q_BLOCK
    g = pl.program_id(1)
    kc_sc[...] = kc_ref[...].astype(BF16)
    vc_sc[...] = vc_ref[...].astype(BF16)
    qi = lax.broadcasted_iota(jnp.int32, (Q_BLOCK, band), 0)
    kj = lax.broadcasted_iota(jnp.int32, (Q_BLOCK, band), 1)
    for cls, off in enumerate((0, -Q_BLOCK, -2 * Q_BLOCK)):
        bias_sc[cls] = jnp.where(jnp.abs(kj + off - qi) <= WIN_RADIUS, 0.0, NEG_INF).astype(F32)

    def window(n):
        if isinstance(n, int):
            start = min(max((n - 1) * Q_BLOCK, 0), n_lat - band)
            return n * Q_BLOCK, start, 0 if n == 0 else (2 if n == n_blk - 1 else 1)
        start = pl.multiple_of(jnp.clip((n - 1) * Q_BLOCK, 0, n_lat - band), Q_BLOCK)
        cls = jnp.where(n == 0, 0, jnp.where(n == n_blk - 1, 2, 1))
        return pl.multiple_of(n * Q_BLOCK, Q_BLOCK), start, cls

    def head_rows(h):
        return slice(h * Q_BLOCK, (h + 1) * Q_BLOCK)

    def scores(n, s_sc, m_sc):
        q0, start, cls = window(n)
        qn = q_ref[pl.ds(q0, Q_BLOCK), :]
        q = jnp.concatenate([qn[:, h * HEAD_DIM:(h + 1) * HEAD_DIM] for h in range(grp)], axis=0)
        s_ctx = _qk_t(q, kc_sc[...])
        s_loc = _qk_t(q, k_ref[pl.ds(start, band), :])
        bias = bias_sc[cls]
        s_sc[:, 0:past] = s_ctx
        for h in range(grp):
            sl = s_loc[head_rows(h)] + bias
            s_sc[head_rows(h), past:past + band] = sl
            sk = sink_ref[l, g * grp + h]
            m_sc[head_rows(h), :] = jnp.maximum(
                jnp.maximum(jnp.max(s_ctx[head_rows(h)], axis=-1, keepdims=True),
                            jnp.max(sl, axis=-1, keepdims=True)), sk)

    def values(n, s_sc, m_sc):
        q0, start, _ = window(n)
        m = m_sc[...]
        pc = jnp.exp2(s_sc[:, 0:past] - m)
        pw = jnp.exp2(s_sc[:, past:past + band] - m)
        den = jnp.sum(pc, axis=-1, keepdims=True) + jnp.sum(pw, axis=-1, keepdims=True)
        acc = (_mm(pc.astype(BF16), vc_sc[...])
               + _mm(pw.astype(BF16), v_ref[pl.ds(start, band), :]))
        for h in range(grp):
            sk = sink_ref[l, g * grp + h]
            dh = den[head_rows(h)] + jnp.exp2(sk - m[head_rows(h)])
            o_ref[pl.ds(q0, Q_BLOCK), h * HEAD_DIM:(h + 1) * HEAD_DIM] = (
                acc[head_rows(h)] / dh).astype(BF16)

    scores(0, s0_sc, m0_sc)

    def body(k, carry):
        n = 2 * k
        scores(n + 1, s1_sc, m1_sc)
        values(n, s0_sc, m0_sc)
        scores(n + 2, s0_sc, m0_sc)
        values(n + 1, s1_sc, m1_sc)
        return carry

    lax.fori_loop(0, n_blk // 2 - 1, body, 0)
    scores(n_blk - 1, s1_sc, m1_sc)
    values(n_blk - 2, s0_sc, m0_sc)
    values(n_blk - 1, s1_sc, m1_sc)


def _win_attention(zq, cache_k, cache_v, sink2, *, l, n_b, n_lat):
    past = cache_k.shape[2]
    grp = N_HEADS // KV_WIN
    band = 3 * Q_BLOCK
    wq = grp * HEAD_DIM
    qc_blk, kc_blk, vc_blk = OFF_QC // wq, OFF_KC // HEAD_DIM, OFF_VC // HEAD_DIM
    kern = functools.partial(_win_kernel, l=l, past=past, n_lat=n_lat)
    nbytes = (4 * n_lat * wq * 2 + 4 * n_lat * HEAD_DIM * 2 + 4 * past * HEAD_DIM * 4
              + (past + band) * HEAD_DIM * 2 * 3 + 3 * Q_BLOCK * band * 4
              + 4 * grp * Q_BLOCK * (past + band) * 4)
    return pl.pallas_call(
        kern,
        grid=(n_b, KV_WIN),
        in_specs=[
            pl.BlockSpec((n_lat, wq), lambda b, g: (b, qc_blk + g)),
            pl.BlockSpec((n_lat, HEAD_DIM), lambda b, g: (b, kc_blk + g)),
            pl.BlockSpec((n_lat, HEAD_DIM), lambda b, g: (b, vc_blk + g)),
            pl.BlockSpec((None, None, past, HEAD_DIM), lambda b, g: (b, l, 0, g)),
            pl.BlockSpec((None, None, past, HEAD_DIM), lambda b, g: (b, l, 0, g)),
            pl.BlockSpec(memory_space=pltpu.SMEM),
        ],
        out_specs=pl.BlockSpec((n_lat, wq), lambda b, g: (b, g)),
        out_shape=jax.ShapeDtypeStruct((n_b * n_lat, N_HEADS * HEAD_DIM), BF16),
        scratch_shapes=[
            pltpu.VMEM((past, HEAD_DIM), BF16),
            pltpu.VMEM((past, HEAD_DIM), BF16),
            pltpu.VMEM((3, Q_BLOCK, band), F32),
            pltpu.VMEM((grp * Q_BLOCK, past + band), F32),
            pltpu.VMEM((grp * Q_BLOCK, past + band), F32),
            pltpu.VMEM((grp * Q_BLOCK, 1), F32),
            pltpu.VMEM((grp * Q_BLOCK, 1), F32),
        ],
        compiler_params=_params(("parallel", "parallel"), nbytes),
        name="attn_lat_window",
    )(zq, zq, zq, cache_k, cache_v, sink2)


def _merge_kernel(h_ref, oa_ref, ob_ref, oc_ref, wga_ref, wgb_ref, wgc_ref,
                  wba_ref, wbb_ref, wbc_ref, m_ref):
    h = h_ref[...]
    acc = None
    for o_ref, wg_ref, wb_ref in ((oa_ref, wga_ref, wba_ref), (ob_ref, wgb_ref, wbb_ref),
                                  (oc_ref, wgc_ref, wbc_ref)):
        gate = jax.nn.sigmoid(_mm(h, wg_ref[...].astype(BF16)))
        term = gate * _mm(o_ref[...], wb_ref[...].astype(BF16))
        acc = term if acc is None else acc + term
    m_ref[...] = acc.astype(BF16)


def _merge(h, outs, w_in, w_branch, *, l, tm):
    m, d = h.shape
    bw = w_branch.shape[2]
    tn = _divisor_tile(d, 256, 128)
    g0 = OFF_GATES // tn
    per = d // tn

    def o_spec(width, first):
        return pl.BlockSpec((tm, bw), lambda i, j: (i, first))

    in_specs = [pl.BlockSpec((tm, d), lambda i, j: (i, 0))]
    args = [h]
    for o, first in outs:
        in_specs.append(o_spec(bw, first))
        args.append(o)
    for k in range(3):
        in_specs.append(pl.BlockSpec((None, d, tn), lambda i, j, k=k: (l, 0, g0 + k * per + j)))
        args.append(w_in)
    for k in range(3):
        in_specs.append(pl.BlockSpec((None, None, bw, tn), lambda i, j, k=k: (l, k, 0, j)))
        args.append(w_branch)
    nbytes = (2 * tm * d * 2 + 6 * tm * bw * 2 + 6 * d * tn * 4 + 6 * bw * tn * 4
              + 3 * (d + bw) * tn * 2 + 6 * tm * tn * 4)
    return pl.pallas_call(
        _merge_kernel,
        grid=(m // tm, d // tn),
        in_specs=in_specs,
        out_specs=pl.BlockSpec((tm, tn), lambda i, j: (i, j)),
        out_shape=jax.ShapeDtypeStruct((m, d), BF16),
        compiler_params=_params(("parallel", "parallel"), nbytes),
        name="branch_merge",
    )(*args)


def _pack_kv_kernel(*refs, seq):
    n_in = len(refs) - len(KVN_PARTS)
    for l in range(n_in):
        for out_ref, (off, heads) in zip(refs[n_in:], KVN_PARTS):
            for h in range(heads):
                c0 = off + h * HEAD_DIM
                out_ref[l, pl.ds(h, seq, stride=heads), :] = refs[l][:, c0:c0 + HEAD_DIM]


def _pack_context_kv(kvn, *, batch, seq):
    depth = len(kvn)
    kern = functools.partial(_pack_kv_kernel, seq=seq)
    nbytes = 2 * depth * seq * KVN_W * 4 * 2
    return pl.pallas_call(
        kern,
        grid=(batch,),
        in_specs=[pl.BlockSpec((seq, KVN_W), lambda b: (b, 0)) for _ in range(depth)],
        out_specs=[pl.BlockSpec((None, depth, seq * heads, HEAD_DIM), lambda b: (b, 0, 0, 0))
                   for _, heads in KVN_PARTS],
        out_shape=[jax.ShapeDtypeStruct((batch, depth, seq * heads, HEAD_DIM), F32)
                   for _, heads in KVN_PARTS],
        compiler_params=_params(("parallel",), nbytes),
        name="pack_context_kv",
    )(*kvn)


def _rope_tables(n_lat):
    half = HEAD_DIM // 2
    nf = half // 2
    t = np.arange(n_lat)
    row = (t // GRID_W).astype(np.float64)
    col = (t % GRID_W).astype(np.float64)
    inv = 1.0 / (ROPE_THETA ** (np.arange(nf, dtype=np.float64) / nf))
    ang_r = row[:, None] * inv[None, :]
    ang_c = col[:, None] * inv[None, :]
    cr, sr, cc, sc = np.cos(ang_r), np.sin(ang_r), np.cos(ang_c), np.sin(ang_c)
    zero = np.zeros_like(sr)
    cos = np.concatenate([cr, cr, cc, cc], axis=-1)
    sin_a = np.concatenate([-sr, zero, -sc, zero], axis=-1)
    sin_b = np.concatenate([zero, sr, zero, sc], axis=-1)
    return tuple(jnp.asarray(a, F32) for a in (cos, sin_a, sin_b))


def _trunk_layer(x, *, l, tm, row_fn, is_ctx, mods, g_norm, w_ffn1_gu, w_ffn1_down, w_in, g_q, g_k,
                 w_branch, w_o, w_ffn2_gu, w_ffn2_down, attend, rope_tabs, g_final):
    x = _ffn(x, mods, g_norm, w_ffn1_gu, w_ffn1_down, l=l, norm_idx=0, k_shift=0, k_scale=1,
             k_gate=2, tm=tm, row_fn=row_fn)
    res = _inproj(x, mods, g_norm, g_q, g_k, w_in, rope_tabs, l=l, tm=tm, row_fn=row_fn,
                  is_ctx=is_ctx)
    zq, h2 = res[0], res[1]
    outs = attend(zq)
    mm = _merge(h2, outs, w_in, w_branch, l=l, tm=tm)
    x = _ffn(x, mods, g_norm, w_ffn2_gu, w_ffn2_down, l=l, norm_idx=2, k_shift=6, k_scale=7,
             k_gate=8, tm=tm, row_fn=row_fn, g_final=g_final, mix=(mm, w_o, 5))
    return x, (res[2] if is_ctx else None)


def kernel(x_prompt, x_sample, cache_k_gqa, cache_v_gqa, cache_k_nat, cache_v_nat, cache_k_win, cache_v_win, c, c_ctx, w_mod, b_mod, g_norm, w_ffn1_gu, w_ffn1_down, w_in, g_q, g_k, rpb, sink, w_branch, w_o, w_ffn2_gu, w_ffn2_down, g_final):
    batch, seq, d = x_prompt.shape
    n_b, n_lat, _ = x_sample.shape
    depth = w_mod.shape[0]
    past = cache_k_gqa.shape[2]
    rows = n_lat // GRID_W
    assert n_lat % GRID_W == 0 and rows % (2 * NAT_QROWS) == 0 and rows >= NAT_KROWS + NAT_QROWS
    assert n_lat % Q_BLOCK == 0 and n_lat >= 3 * Q_BLOCK and n_b + 1 <= 8

    m_ctx = batch * seq
    tm_ctx = _divisor_tile(m_ctx, 1024, 16)
    tm_lat = _divisor_tile(n_lat, 1024, 16)
    per_lat = n_lat // tm_lat

    cond8 = jnp.zeros((8, d), F32).at[0].set(c_ctx).at[1:1 + n_b].set(c)
    mods = _modulation(cond8, w_mod, b_mod)

    sink2 = sink * LOG2E
    rope_tabs = _rope_tables(n_lat)
    nat_tabs = _nat_row_tables(rpb)
    ck_gqa = cache_k_gqa.reshape(n_b, depth, past, KV_GQA * HEAD_DIM)
    cv_gqa = cache_v_gqa.reshape(n_b, depth, past, KV_GQA * HEAD_DIM)
    ck_nat = cache_k_nat.reshape(n_b, depth, past, N_HEADS * HEAD_DIM)
    cv_nat = cache_v_nat.reshape(n_b, depth, past, N_HEADS * HEAD_DIM)
    ck_win = cache_k_win.reshape(n_b, depth, past, KV_WIN * HEAD_DIM)
    cv_win = cache_v_win.reshape(n_b, depth, past, KV_WIN * HEAD_DIM)

    weights = dict(g_norm=g_norm, w_ffn1_gu=w_ffn1_gu, w_ffn1_down=w_ffn1_down, w_in=w_in,
                   g_q=g_q, g_k=g_k, w_branch=w_branch, w_o=w_o, w_ffn2_gu=w_ffn2_gu,
                   w_ffn2_down=w_ffn2_down, mods=mods)

    y_p = x_prompt.reshape(m_ctx, d)
    y_s = x_sample.reshape(n_b * n_lat, d)
    kvn = []
    for l in range(depth):
        def attend_ctx(zq, l=l):
            o = _ctx_attention(zq, sink2, l=l, batch=batch, seq=seq)
            return [(o, 0), (o, 1), (o, 2)]

        gf = g_final if l == depth - 1 else None
        y_p, kvn_l = _trunk_layer(y_p, l=l, tm=tm_ctx, row_fn=lambda i: 0, is_ctx=True,
                                  attend=attend_ctx, rope_tabs=None, g_final=gf, **weights)
        kvn.append(kvn_l)

        def attend_lat(zq, l=l):
            oa = _lat_gqa_attention(zq, ck_gqa, cv_gqa, l=l, n_b=n_b, n_lat=n_lat)
            ob = _nat_attention(zq, ck_nat, cv_nat, nat_tabs, l=l, n_b=n_b, n_lat=n_lat)
            oc = _win_attention(zq, ck_win, cv_win, sink2, l=l, n_b=n_b, n_lat=n_lat)
            return [(oa, 0), (ob, 0), (oc, 0)]

        y_s, _ = _trunk_layer(y_s, l=l, tm=tm_lat, row_fn=lambda i: 1 + i // per_lat, is_ctx=False,
                              attend=attend_lat, rope_tabs=rope_tabs, g_final=gf, **weights)

    y_prompt = y_p.reshape(batch, seq, d)
    y_sample = y_s.reshape(n_b, n_lat, d)

    new_kv = _pack_context_kv(kvn, batch=batch, seq=seq)
    return (y_prompt, y_sample) + tuple(
        a.reshape(batch, depth, seq, heads, HEAD_DIM) for a, (_, heads) in zip(new_kv, KVN_PARTS))
```

```python
import functools
import math

import numpy as np
import jax
import jax.numpy as jnp
from jax import lax
from jax.experimental import pallas as pl
from jax.experimental.pallas import tpu as pltpu

F32 = jnp.float32
BF16 = jnp.bfloat16

HEAD_DIM = 128
N_HEADS = 8
KV_GQA = 2
KV_WIN = 2
GRID_W = 64
NAT_ROWS = 8
NAT_COLS = 16
WIN_RADIUS = 128
Q_BLOCK = 128
ROPE_THETA = 10000.0
EPS = 1e-6
NEG_INF = -1e30
LOG2E = math.log2(math.e)
Q_SCALE = (HEAD_DIM ** -0.5) * LOG2E

OFF_QA, OFF_KA, OFF_VA = 0, 1024, 1280
OFF_QB, OFF_KB, OFF_VB = 1536, 2560, 3584
OFF_QC, OFF_KC, OFF_VC = 4608, 5632, 5888
OFF_GATES = 6144
QKV_W = 6144
COL_TILE = 512
KVN_W = 3072
KVN_PARTS = ((0, KV_GQA), (256, KV_GQA), (512, N_HEADS), (1536, N_HEADS),
             (2560, KV_WIN), (2816, KV_WIN))

V7X_VMEM_BYTES = 64 * 1024 * 1024
VMEM_CAP_BYTES = 60 * 1024 * 1024
NAT_QROWS = 4
NAT_KROWS = NAT_ROWS + NAT_QROWS


def _vmem_limit(nbytes):
    return int(min(VMEM_CAP_BYTES, max(32 * 1024 * 1024, nbytes * 3 // 2)))


def _params(sem, nbytes):
    return pltpu.CompilerParams(dimension_semantics=sem, vmem_limit_bytes=_vmem_limit(nbytes))


def _divisor_tile(n, cap, mult):
    if n <= cap:
        return n
    t = (cap // mult) * mult
    while t > mult and n % t:
        t -= mult
    assert n % t == 0, (n, cap, mult)
    return t


def _adaln(x, g, shift, scale):
    ms = jnp.mean(x * x, axis=-1, keepdims=True)
    y = x * lax.rsqrt(ms + EPS)
    return (y * g) * (1.0 + scale) + shift


def _headnorm(y, g):
    ms = jnp.mean(y * y, axis=-1, keepdims=True)
    return (y * lax.rsqrt(ms + EPS)) * g


def _silu(a):
    return a * jax.nn.sigmoid(a)


def _qk_t(q, k):
    return lax.dot_general(q, k, (((1,), (1,)), ((), ())), preferred_element_type=F32)


def _mm(a, b):
    return jnp.dot(a, b, preferred_element_type=F32)


def _mod_kernel(c_ref, w_ref, b_ref, o_ref):
    a = _silu(c_ref[...]).astype(BF16)
    o_ref[...] = _mm(a, w_ref[...].astype(BF16)) + b_ref[...]


def _modulation(cond8, w_mod, b_mod):
    depth, d, n = w_mod.shape
    tn = _divisor_tile(n, 1024, 128)
    nbytes = 2 * d * tn * 4 + d * tn * 2 + 4 * 8 * tn * 4 + 8 * d * 4
    return pl.pallas_call(
        _mod_kernel,
        grid=(depth, n // tn),
        in_specs=[
            pl.BlockSpec((8, d), lambda l, j: (0, 0)),
            pl.BlockSpec((None, d, tn), lambda l, j: (l, 0, j)),
            pl.BlockSpec((None, 1, tn), lambda l, j: (l, 0, j)),
        ],
        out_specs=pl.BlockSpec((None, 8, tn), lambda l, j: (l, 0, j)),
        out_shape=jax.ShapeDtypeStruct((depth, 8, n), F32),
        compiler_params=_params(("parallel", "parallel"), nbytes),
        name="modulation",
    )(cond8, w_mod, b_mod.reshape(depth, 1, n))


FFN_SLOTS = 3


def _ffn_kernel(*refs, l, row_fn, norm_idx, tf, d_ff, tm, final_norm):
    if final_norm:
        (x_hbm, g_ref, sh_ref, sc_ref, gt_ref, wgu_hbm, wd_hbm, gf_ref, o_ref,
         x_buf, h_ref, wg_buf, wu_buf, wd_buf, wsem, xsem) = refs
    else:
        (x_hbm, g_ref, sh_ref, sc_ref, gt_ref, wgu_hbm, wd_hbm, o_ref,
         x_buf, h_ref, wg_buf, wu_buf, wd_buf, wsem, xsem) = refs
    n_chunk = d_ff // tf
    d = o_ref.shape[1]
    tn = _divisor_tile(d, 512, 128)
    i = pl.program_id(0)
    n_tiles = pl.num_programs(0)
    lookahead = FFN_SLOTS - 1

    def weight_copies(c, slot):
        col = pl.multiple_of(c * tf, tf)
        return (
            pltpu.make_async_copy(wgu_hbm.at[l, :, pl.ds(col, tf)], wg_buf.at[slot], wsem.at[0, slot]),
            pltpu.make_async_copy(wgu_hbm.at[l, :, pl.ds(d_ff + col, tf)], wu_buf.at[slot],
                                  wsem.at[1, slot]),
            pltpu.make_async_copy(wd_hbm.at[l, pl.ds(col, tf), :], wd_buf.at[slot], wsem.at[2, slot]),
        )

    def rows_copy(tile):
        return pltpu.make_async_copy(x_hbm.at[pl.ds(pl.multiple_of(tile * tm, tm), tm), :],
                                     x_buf, xsem.at[0])

    def start_weights(g):
        for cp in weight_copies(lax.rem(g, n_chunk), lax.rem(g, FFN_SLOTS)):
            cp.start()

    def wait_weights(g):
        for cp in weight_copies(lax.rem(g, n_chunk), lax.rem(g, FFN_SLOTS)):
            cp.wait()

    @pl.when(i == 0)
    def _():
        rows_copy(0).start()
        for g in range(lookahead):
            start_weights(jnp.int32(g))

    g0 = i * n_chunk
    r = row_fn(i)
    gate = 0.5 * gt_ref[pl.ds(r, 1), :]

    def chunk(g, h):
        slot = lax.rem(g, FFN_SLOTS)
        a = _mm(h, wg_buf[slot].astype(BF16))
        b = _mm(h, wu_buf[slot].astype(BF16))
        t = (_silu(a) * b).astype(BF16)
        for n in range(d // tn):
            cs = slice(n * tn, (n + 1) * tn)
            o_ref[:, cs] += gate[:, cs] * _mm(t, wd_buf[slot, :, cs].astype(BF16))

    rows_copy(i).wait()
    start_weights(g0 + lookahead)
    wait_weights(g0)
    x = x_buf[...]
    h = _adaln(x, g_ref[norm_idx:norm_idx + 1, :],
               sh_ref[pl.ds(r, 1), :], sc_ref[pl.ds(r, 1), :]).astype(BF16)
    h_ref[...] = h
    o_ref[...] = x
    chunk(g0, h)

    @pl.when(i + 1 < n_tiles)
    def _():
        rows_copy(i + 1).start()

    def body(c, carry):
        g = g0 + c
        start_weights(g + lookahead)
        wait_weights(g)
        chunk(g, h_ref[...])
        return carry

    lax.fori_loop(1, n_chunk, body, 0, unroll=2)

    @pl.when(i + 1 == n_tiles)
    def _():
        for k in range(lookahead):
            wait_weights(g0 + n_chunk + k)

    if final_norm:
        y = o_ref[...]
        ms = jnp.mean(y * y, axis=-1, keepdims=True)
        o_ref[...] = (y * lax.rsqrt(ms + EPS)) * gf_ref[...]


def _ffn(x, mods, g_norm, w_gu, w_down, *, l, norm_idx, k_shift, k_scale, k_gate, tm, row_fn,
         g_final=None):
    m, d = x.shape
    d_ff = w_down.shape[1]
    tf = _divisor_tile(d_ff, 256, 128)
    final_norm = g_final is not None
    nbytes = (3 * tm * d * 4 + tm * d * 2 + 3 * FFN_SLOTS * d * tf * 4 + 3 * d * tf * 2
              + 3 * tm * tf * 4 + 2 * tm * 512 * 4)
    kern = functools.partial(_ffn_kernel, l=l, row_fn=row_fn, norm_idx=norm_idx, tf=tf, d_ff=d_ff,
                             tm=tm, final_norm=final_norm)
    in_specs = [
        pl.BlockSpec(memory_space=pl.ANY),
        pl.BlockSpec((None, 3, d), lambda i: (l, 0, 0)),
        pl.BlockSpec((None, 8, d), lambda i: (l, 0, k_shift)),
        pl.BlockSpec((None, 8, d), lambda i: (l, 0, k_scale)),
        pl.BlockSpec((None, 8, d), lambda i: (l, 0, k_gate)),
        pl.BlockSpec(memory_space=pl.ANY),
        pl.BlockSpec(memory_space=pl.ANY),
    ]
    args = [x, g_norm, mods, mods, mods, w_gu, w_down]
    if final_norm:
        in_specs.append(pl.BlockSpec((1, d), lambda i: (0, 0)))
        args.append(g_final.reshape(1, d))
    return pl.pallas_call(
        kern,
        grid=(m // tm,),
        in_specs=in_specs,
        out_specs=pl.BlockSpec((tm, d), lambda i: (i, 0)),
        out_shape=jax.ShapeDtypeStruct((m, d), F32),
        scratch_shapes=[
            pltpu.VMEM((tm, d), F32),
            pltpu.VMEM((tm, d), BF16),
            pltpu.VMEM((FFN_SLOTS, d, tf), F32),
            pltpu.VMEM((FFN_SLOTS, d, tf), F32),
            pltpu.VMEM((FFN_SLOTS, tf, d), F32),
            pltpu.SemaphoreType.DMA((3, FFN_SLOTS)),
            pltpu.SemaphoreType.DMA((1,)),
        ],
        compiler_params=_params(("arbitrary",), nbytes),
        name="ffn_final" if final_norm else "ffn",
    )(*args)


def _kvn_block(j):
    return jnp.where(j <= 2, 0, jnp.where(j >= 9, 5, jnp.clip(j - 4, 1, 4)))


def _inproj_kernel(*refs, is_ctx, row_fn, norm_idx):
    if is_ctx:
        (x_ref, g_ref, sh_ref, sc_ref, gq_ref, gk_ref, w_ref, zq_ref, h_ref, kvn_ref,
         acc_ref) = refs
    else:
        (x_ref, g_ref, sh_ref, sc_ref, gq_ref, gk_ref, cos_ref, sa_ref, sb_ref, w_ref,
         zq_ref, h_ref, acc_ref) = refs
    j = pl.program_id(1)

    def rope(y):
        if is_ctx:
            return y
        return (y * cos_ref[...] + pltpu.roll(y, 96, 1) * sa_ref[...]
                + pltpu.roll(y, 32, 1) * sb_ref[...])

    def cols(hh):
        return slice(hh * HEAD_DIM, (hh + 1) * HEAD_DIM)

    def project(h):
        return _mm(h, w_ref[...].astype(BF16))

    def query_norm_rope(acc):
        for hh in range(4):
            y = _headnorm(acc[:, cols(hh)], gq_ref[...])
            zq_ref[:, cols(hh)] = (rope(y) * Q_SCALE).astype(BF16)

    @pl.when(j == 0)
    def _():
        r = row_fn(pl.program_id(0))
        h = _adaln(x_ref[...], g_ref[norm_idx:norm_idx + 1, :],
                   sh_ref[pl.ds(r, 1), :], sc_ref[pl.ds(r, 1), :]).astype(BF16)
        h_ref[...] = h
        acc_ref[...] = project(h)

    @pl.when((j == 1) | (j == 2))
    def _():
        acc_ref[...] = project(h_ref[...])

    @pl.when(j < 2)
    def _():
        query_norm_rope(acc_ref[...])

    @pl.when(j == 2)
    def _():
        acc = acc_ref[...]
        for hh in range(2):
            y = _headnorm(acc[:, cols(hh)], gk_ref[...])
            if is_ctx:
                kvn_ref[:, cols(hh)] = y
            zq_ref[:, cols(hh)] = rope(y).astype(BF16)
        va = acc[:, 256:512]
        if is_ctx:
            kvn_ref[:, 256:512] = va
        zq_ref[:, 256:512] = va.astype(BF16)

    @pl.when((j == 3) | (j == 4))
    def _():
        zq_ref[...] = (project(h_ref[...]) * Q_SCALE).astype(BF16)

    @pl.when((j >= 5) & (j <= 8))
    def _():
        acc = project(h_ref[...])
        if is_ctx:
            kvn_ref[...] = acc
        zq_ref[...] = acc.astype(BF16)

    @pl.when((j == 9) | (j == 10))
    def _():
        acc = project(h_ref[...])
        for hh in range(4):
            zq_ref[:, cols(hh)] = (rope(acc[:, cols(hh)]) * Q_SCALE).astype(BF16)

    @pl.when(j == 11)
    def _():
        acc = project(h_ref[...])
        for hh in range(2):
            y = acc[:, cols(hh)]
            if is_ctx:
                kvn_ref[:, cols(hh)] = y
            zq_ref[:, cols(hh)] = rope(y).astype(BF16)
        vc = acc[:, 256:512]
        if is_ctx:
            kvn_ref[:, 256:512] = vc
        zq_ref[:, 256:512] = vc.astype(BF16)


def _inproj(x, mods, g_norm, g_q, g_k, w_in, rope_tabs, *, l, tm, row_fn, is_ctx):
    m, d = x.shape
    depth = w_in.shape[0]
    nj = QKV_W // COL_TILE
    in_specs = [
        pl.BlockSpec((tm, d), lambda i, j: (i, 0)),
        pl.BlockSpec((None, 3, d), lambda i, j: (l, 0, 0)),
        pl.BlockSpec((None, 8, d), lambda i, j: (l, 0, 3)),
        pl.BlockSpec((None, 8, d), lambda i, j: (l, 0, 4)),
        pl.BlockSpec((None, 1, HEAD_DIM), lambda i, j: (l, 0, 0)),
        pl.BlockSpec((None, 1, HEAD_DIM), lambda i, j: (l, 0, 0)),
    ]
    args = [x, g_norm, mods, mods, g_q.reshape(depth, 1, HEAD_DIM), g_k.reshape(depth, 1, HEAD_DIM)]
    if not is_ctx:
        n_pos = rope_tabs[0].shape[0]
        per = n_pos // tm
        for tab in rope_tabs:
            in_specs.append(pl.BlockSpec((tm, HEAD_DIM), lambda i, j: (i % per, 0)))
            args.append(tab)
    in_specs.append(pl.BlockSpec((None, d, COL_TILE), lambda i, j: (l, 0, j)))
    args.append(w_in)
    out_specs = [
        pl.BlockSpec((tm, COL_TILE), lambda i, j: (i, j)),
        pl.BlockSpec((tm, d), lambda i, j: (i, 0)),
    ]
    out_shape = [jax.ShapeDtypeStruct((m, QKV_W), BF16), jax.ShapeDtypeStruct((m, d), BF16)]
    if is_ctx:
        out_specs.append(pl.BlockSpec((tm, COL_TILE), lambda i, j: (i, _kvn_block(j))))
        out_shape.append(jax.ShapeDtypeStruct((m, KVN_W), F32))
    nbytes = (2 * tm * d * 4 + 2 * tm * d * 2 + 2 * d * COL_TILE * 4 + d * COL_TILE * 2
              + 3 * tm * COL_TILE * 4 + 2 * tm * COL_TILE * 2 + 6 * tm * HEAD_DIM * 4)
    kern = functools.partial(_inproj_kernel, is_ctx=is_ctx, row_fn=row_fn, norm_idx=1)
    return pl.pallas_call(
        kern,
        grid=(m // tm, nj),
        in_specs=in_specs,
        out_specs=out_specs,
        out_shape=out_shape,
        scratch_shapes=[pltpu.VMEM((tm, COL_TILE), F32)],
        compiler_params=_params(("parallel", "arbitrary"), nbytes),
        name="inproj_ctx" if is_ctx else "inproj_lat",
    )(*args)


def _ctx_attn_kernel(z_ref, sink_ref, o_ref, *, l, seq):
    def attend(q_off, n_q, k_off, v_off, sinks):
        q = jnp.concatenate(
            [z_ref[:, q_off + h * HEAD_DIM:q_off + (h + 1) * HEAD_DIM] for h in range(n_q)], axis=0)
        k = z_ref[:, k_off:k_off + HEAD_DIM]
        v = z_ref[:, v_off:v_off + HEAD_DIM]
        s = _qk_t(q, k)
        m = jnp.max(s, axis=-1, keepdims=True)
        if sinks is not None:
            sink_col = jnp.concatenate(
                [jnp.full((seq, 1), sk, F32) for sk in sinks], axis=0)
            m = jnp.maximum(m, sink_col)
        p = jnp.exp2(s - m)
        den = jnp.sum(p, axis=-1, keepdims=True)
        if sinks is not None:
            den = den + jnp.exp2(sink_col - m)
        return _mm(p.astype(BF16), v) / den

    def store(o, o_off, n_q):
        for h in range(n_q):
            o_ref[:, o_off + h * HEAD_DIM:o_off + (h + 1) * HEAD_DIM] = (
                o[h * seq:(h + 1) * seq].astype(BF16))

    grp = N_HEADS // KV_GQA
    for g in range(KV_GQA):
        o = attend(OFF_QA + g * grp * HEAD_DIM, grp, OFF_KA + g * HEAD_DIM,
                   OFF_VA + g * HEAD_DIM, None)
        store(o, g * grp * HEAD_DIM, grp)
    for h in range(N_HEADS):
        o = attend(OFF_QB + h * HEAD_DIM, 1, OFF_KB + h * HEAD_DIM, OFF_VB + h * HEAD_DIM, None)
        store(o, 1024 + h * HEAD_DIM, 1)
    grp = N_HEADS // KV_WIN
    for g in range(KV_WIN):
        sinks = [sink_ref[l, g * grp + h] for h in range(grp)]
        o = attend(OFF_QC + g * grp * HEAD_DIM, grp, OFF_KC + g * HEAD_DIM,
                   OFF_VC + g * HEAD_DIM, sinks)
        store(o, 2048 + g * grp * HEAD_DIM, grp)


def _ctx_attention(zq, sink2, *, l, batch, seq):
    kern = functools.partial(_ctx_attn_kernel, l=l, seq=seq)
    nbytes = 2 * seq * QKV_W * 2 + 2 * seq * 3072 * 2 + 8 * 4 * seq * seq * 4
    return pl.pallas_call(
        kern,
        grid=(batch,),
        in_specs=[
            pl.BlockSpec((seq, QKV_W), lambda b: (b, 0)),
            pl.BlockSpec(memory_space=pltpu.SMEM),
        ],
        out_specs=pl.BlockSpec((seq, 3072), lambda b: (b, 0)),
        out_shape=jax.ShapeDtypeStruct((batch * seq, 3072), BF16),
        compiler_params=_params(("parallel",), nbytes),
        name="attn_ctx",
    )(zq, sink2)


def _lat_gqa_kernel(q_ref, kl_ref, vl_ref, kc_ref, vc_ref, o_ref, k_sc, v_sc, s0_sc, s1_sc,
                    m0_sc, m1_sc, *, past, n_lat, tq, chunk):
    grp = N_HEADS // KV_GQA
    total = past + n_lat
    n_chunk = total // chunk
    nq = n_lat // tq
    k_sc[0:past, :] = kc_ref[...].astype(BF16)
    k_sc[past:total, :] = kl_ref[...]
    v_sc[0:past, 0:HEAD_DIM] = vc_ref[...].astype(BF16)
    v_sc[past:total, 0:HEAD_DIM] = vl_ref[...]
    v_sc[:, HEAD_DIM:2 * HEAD_DIM] = jnp.ones((total, HEAD_DIM), BF16)

    def row0(t):
        return t * tq if isinstance(t, int) else pl.multiple_of(t * tq, tq)

    def scores(t, s_sc, m_sc):
        qn = q_ref[pl.ds(row0(t), tq), :]
        q = jnp.concatenate([qn[:, h * HEAD_DIM:(h + 1) * HEAD_DIM] for h in range(grp)], axis=0)
        m = None
        for c in range(n_chunk):
            s = _qk_t(q, k_sc[c * chunk:(c + 1) * chunk, :])
            s_sc[:, c * chunk:(c + 1) * chunk] = s
            mc = jnp.max(s, axis=-1, keepdims=True)
            m = mc if m is None else jnp.maximum(m, mc)
        m_sc[...] = m

    def values(t, s_sc, m_sc):
        m = m_sc[...]
        acc = None
        for c in range(n_chunk):
            p = jnp.exp2(s_sc[:, c * chunk:(c + 1) * chunk] - m).astype(BF16)
            term = _mm(p, v_sc[c * chunk:(c + 1) * chunk, :])
            acc = term if acc is None else acc + term
        o = acc[:, 0:HEAD_DIM] / acc[:, HEAD_DIM:2 * HEAD_DIM]
        for h in range(grp):
            o_ref[pl.ds(row0(t), tq), h * HEAD_DIM:(h + 1) * HEAD_DIM] = (
                o[h * tq:(h + 1) * tq].astype(BF16))

    scores(0, s0_sc, m0_sc)

    def body(k, carry):
        t = 2 * k
        scores(t + 1, s1_sc, m1_sc)
        values(t, s0_sc, m0_sc)
        scores(t + 2, s0_sc, m0_sc)
        values(t + 1, s1_sc, m1_sc)
        return carry

    lax.fori_loop(0, nq // 2 - 1, body, 0)
    scores(nq - 1, s1_sc, m1_sc)
    values(nq - 2, s0_sc, m0_sc)
    values(nq - 1, s1_sc, m1_sc)


def _lat_gqa_attention(zq, cache_k, cache_v, *, l, n_b, n_lat):
    past = cache_k.shape[2]
    grp = N_HEADS // KV_GQA
    tq = _divisor_tile(n_lat, 128, 16)
    assert (n_lat // tq) % 2 == 0
    total = past + n_lat
    chunk = _divisor_tile(total, 1536, 128)
    wq = grp * HEAD_DIM
    ka_blk = OFF_KA // HEAD_DIM
    va_blk = OFF_VA // HEAD_DIM
    kern = functools.partial(_lat_gqa_kernel, past=past, n_lat=n_lat, tq=tq, chunk=chunk)
    nbytes = (4 * n_lat * wq * 2 + 4 * n_lat * HEAD_DIM * 2 + 4 * past * HEAD_DIM * 4
              + total * HEAD_DIM * 2 * 3 + 2 * grp * tq * total * 4 + 4 * grp * tq * chunk * 4)
    return pl.pallas_call(
        kern,
        grid=(n_b, KV_GQA),
        in_specs=[
            pl.BlockSpec((n_lat, wq), lambda b, g: (b, g)),
            pl.BlockSpec((n_lat, HEAD_DIM), lambda b, g: (b, ka_blk + g)),
            pl.BlockSpec((n_lat, HEAD_DIM), lambda b, g: (b, va_blk + g)),
            pl.BlockSpec((None, None, past, HEAD_DIM), lambda b, g: (b, l, 0, g)),
            pl.BlockSpec((None, None, past, HEAD_DIM), lambda b, g: (b, l, 0, g)),
        ],
        out_specs=pl.BlockSpec((n_lat, wq), lambda b, g: (b, g)),
        out_shape=jax.ShapeDtypeStruct((n_b * n_lat, N_HEADS * HEAD_DIM), BF16),
        scratch_shapes=[
            pltpu.VMEM((total, HEAD_DIM), BF16),
            pltpu.VMEM((total, 2 * HEAD_DIM), BF16),
            pltpu.VMEM((grp * tq, total), F32),
            pltpu.VMEM((grp * tq, total), F32),
            pltpu.VMEM((grp * tq, 1), F32),
            pltpu.VMEM((grp * tq, 1), F32),
        ],
        compiler_params=_params(("parallel", "parallel"), nbytes),
        name="attn_lat_global",
    )(zq, zq, zq, cache_k, cache_v)


def _nat_row_tables(rpb):
    w, wc = GRID_W, NAT_COLS
    c = np.arange(w)
    cs = np.clip(c - wc // 2, 0, w - wc)
    col_ok = (c[None, :] >= cs[:, None]) & (c[None, :] < cs[:, None] + wc)
    dc = np.clip(c[None, :] - c[:, None] + wc - 1, 0, 2 * wc - 2)
    t_tab = jnp.zeros(rpb.shape[:3] + (w, w), F32)
    for d in range(2 * wc - 1):
        t_tab = jnp.where(jnp.asarray(dc == d), rpb[..., d][..., None, None].astype(F32), t_tab)
    return jnp.where(jnp.asarray(col_ok), t_tab, NEG_INF) * LOG2E


def _nat_tile_plan(rows):
    wr = NAT_ROWS
    plan = []
    for r0 in (0, NAT_QROWS, rows - NAT_QROWS):
        ws = int(np.clip(r0 - wr // 2, 0, rows - NAT_KROWS))
        cls = []
        for a in range(NAT_QROWS):
            qr = r0 + a
            rs = int(np.clip(qr - wr // 2, 0, rows - wr))
            cls.append([(ws + b) - qr + wr - 1 if rs <= ws + b < rs + wr else None
                        for b in range(NAT_KROWS)])
        plan.append(cls)
    return plan


def _nat_kernel(q_ref, k_ref, v_ref, kc_ref, vc_ref, t_ref, o_ref, kc_sc, vc_sc, bias_ref,
                s0_sc, s1_sc, m0_sc, m1_sc, *, rows):
    tq = NAT_QROWS * GRID_W
    tk = NAT_KROWS * GRID_W
    n_tiles = rows // NAT_QROWS
    past = kc_ref.shape[0]
    kc_sc[...] = kc_ref[...].astype(BF16)
    vc_sc[...] = vc_ref[...].astype(BF16)
    neg_blk = jnp.full((GRID_W, GRID_W), NEG_INF * LOG2E, F32)
    for cls, cls_plan in enumerate(_nat_tile_plan(rows)):
        for a, row_plan in enumerate(cls_plan):
            for b, dr in enumerate(row_plan):
                bias_ref[cls, a * GRID_W:(a + 1) * GRID_W, b * GRID_W:(b + 1) * GRID_W] = (
                    neg_blk if dr is None else t_ref[dr])

    def window(t):
        if isinstance(t, int):
            ws = min(max(t * NAT_QROWS - NAT_ROWS // 2, 0), rows - NAT_KROWS)
            cls = 0 if t == 0 else (2 if t == n_tiles - 1 else 1)
            return t * tq, ws * GRID_W, cls
        ws = jnp.clip(t * NAT_QROWS - NAT_ROWS // 2, 0, rows - NAT_KROWS)
        cls = jnp.where(t == 0, 0, jnp.where(t == n_tiles - 1, 2, 1))
        return pl.multiple_of(t * tq, tq), pl.multiple_of(ws * GRID_W, GRID_W), cls

    def scores(t, s_sc, m_sc):
        q0, k0, cls = window(t)
        q = q_ref[pl.ds(q0, tq), :]
        s_loc = _qk_t(q, k_ref[pl.ds(k0, tk), :]) + bias_ref[cls]
        s_ctx = _qk_t(q, kc_sc[...])
        s_sc[:, 0:tk] = s_loc
        s_sc[:, tk:tk + past] = s_ctx
        m_sc[...] = jnp.maximum(jnp.max(s_loc, axis=-1, keepdims=True),
                                jnp.max(s_ctx, axis=-1, keepdims=True))

    def values(t, s_sc, m_sc):
        q0, k0, _ = window(t)
        m = m_sc[...]
        p_loc = jnp.exp2(s_sc[:, 0:tk] - m)
        p_ctx = jnp.exp2(s_sc[:, tk:tk + past] - m)
        den = jnp.sum(p_loc, axis=-1, keepdims=True) + jnp.sum(p_ctx, axis=-1, keepdims=True)
        o = _mm(p_ctx.astype(BF16), vc_sc[...]) + _mm(p_loc.astype(BF16), v_ref[pl.ds(k0, tk), :])
        o_ref[pl.ds(q0, tq), :] = (o / den).astype(BF16)

    scores(0, s0_sc, m0_sc)

    def body(k, carry):
        t = 2 * k
        scores(t + 1, s1_sc, m1_sc)
        values(t, s0_sc, m0_sc)
        scores(t + 2, s0_sc, m0_sc)
        values(t + 1, s1_sc, m1_sc)
        return carry

    lax.fori_loop(0, n_tiles // 2 - 1, body, 0)
    scores(n_tiles - 1, s1_sc, m1_sc)
    values(n_tiles - 2, s0_sc, m0_sc)
    values(n_tiles - 1, s1_sc, m1_sc)


def _nat_attention(zq, cache_k, cache_v, row_tabs, *, l, n_b, n_lat):
    past = cache_k.shape[2]
    rows = n_lat // GRID_W
    tq = NAT_QROWS * GRID_W
    tk = NAT_KROWS * GRID_W
    n_dr = row_tabs.shape[2]
    qb, kb, vb = OFF_QB // HEAD_DIM, OFF_KB // HEAD_DIM, OFF_VB // HEAD_DIM
    kern = functools.partial(_nat_kernel, rows=rows)
    nbytes = (8 * n_lat * HEAD_DIM * 2 + 4 * past * HEAD_DIM * 4 + 2 * past * HEAD_DIM * 2
              + 2 * n_dr * GRID_W * 128 * 4 + 3 * tq * tk * 4 + 6 * tq * (tk + past) * 4)
    return pl.pallas_call(
        kern,
        grid=(n_b, N_HEADS),
        in_specs=[
            pl.BlockSpec((n_lat, HEAD_DIM), lambda b, h: (b, qb + h)),
            pl.BlockSpec((n_lat, HEAD_DIM), lambda b, h: (b, kb + h)),
            pl.BlockSpec((n_lat, HEAD_DIM), lambda b, h: (b, vb + h)),
            pl.BlockSpec((None, None, past, HEAD_DIM), lambda b, h: (b, l, 0, h)),
            pl.BlockSpec((None, None, past, HEAD_DIM), lambda b, h: (b, l, 0, h)),
            pl.BlockSpec((None, None, n_dr, GRID_W, GRID_W), lambda b, h: (l, h, 0, 0, 0)),
        ],
        out_specs=pl.BlockSpec((n_lat, HEAD_DIM), lambda b, h: (b, h)),
        out_shape=jax.ShapeDtypeStruct((n_b * n_lat, N_HEADS * HEAD_DIM), BF16),
        scratch_shapes=[pltpu.VMEM((past, HEAD_DIM), BF16), pltpu.VMEM((past, HEAD_DIM), BF16),
                        pltpu.VMEM((3, tq, tk), F32),
                        pltpu.VMEM((tq, tk + past), F32), pltpu.VMEM((tq, tk + past), F32),
                        pltpu.VMEM((tq, 1), F32), pltpu.VMEM((tq, 1), F32)],
        compiler_params=_params(("parallel", "parallel"), nbytes),
        name="attn_lat_nat",
    )(zq, zq, zq, cache_k, cache_v, row_tabs)


def _win_kernel(q_ref, k_ref, v_ref, kc_ref, vc_ref, sink_ref, o_ref, kc_sc, vc_sc, bias_sc,
                s0_sc, s1_sc, m0_sc, m1_sc, *, l, past, n_lat):
    grp = N_HEADS // KV_WIN
    band = 3 * Q_BLOCK
    n_blk = n_lat // Q_BLOCK
    g = pl.program_id(1)
    kc_sc[...] = kc_ref[...].astype(BF16)
    vc_sc[...] = vc_ref[...].astype(BF16)
    qi = lax.broadcasted_iota(jnp.int32, (Q_BLOCK, band), 0)
    kj = lax.broadcasted_iota(jnp.int32, (Q_BLOCK, band), 1)
    for cls, off in enumerate((0, -Q_BLOCK, -2 * Q_BLOCK)):
        bias_sc[cls] = jnp.where(jnp.abs(kj + off - qi) <= WIN_RADIUS, 0.0, NEG_INF).astype(F32)

    def window(n):
        if isinstance(n, int):
            start = min(max((n - 1) * Q_BLOCK, 0), n_lat - band)
            return n * Q_BLOCK, start, 0 if n == 0 else (2 if n == n_blk - 1 else 1)
        start = pl.multiple_of(jnp.clip((n - 1) * Q_BLOCK, 0, n_lat - band), Q_BLOCK)
        cls = jnp.where(n == 0, 0, jnp.where(n == n_blk - 1, 2, 1))
        return pl.multiple_of(n * Q_BLOCK, Q_BLOCK), start, cls

    def head_rows(h):
        return slice(h * Q_BLOCK, (h + 1) * Q_BLOCK)

    def scores(n, s_sc, m_sc):
        q0, start, cls = window(n)
        qn = q_ref[pl.ds(q0, Q_BLOCK), :]
        q = jnp.concatenate([qn[:, h * HEAD_DIM:(h + 1) * HEAD_DIM] for h in range(grp)], axis=0)
        s_ctx = _qk_t(q, kc_sc[...])
        s_loc = _qk_t(q, k_ref[pl.ds(start, band), :])
        bias = bias_sc[cls]
        s_sc[:, 0:past] = s_ctx
        for h in range(grp):
            sl = s_loc[head_rows(h)] + bias
            s_sc[head_rows(h), past:past + band] = sl
            sk = sink_ref[l, g * grp + h]
            m_sc[head_rows(h), :] = jnp.maximum(
                jnp.maximum(jnp.max(s_ctx[head_rows(h)], axis=-1, keepdims=True),
                            jnp.max(sl, axis=-1, keepdims=True)), sk)

    def values(n, s_sc, m_sc):
        q0, start, _ = window(n)
        m = m_sc[...]
        pc = jnp.exp2(s_sc[:, 0:past] - m)
        pw = jnp.exp2(s_sc[:, past:past + band] - m)
        den = jnp.sum(pc, axis=-1, keepdims=True) + jnp.sum(pw, axis=-1, keepdims=True)
        acc = (_mm(pc.astype(BF16), vc_sc[...])
               + _mm(pw.astype(BF16), v_ref[pl.ds(start, band), :]))
        for h in range(grp):
            sk = sink_ref[l, g * grp + h]
            dh = den[head_rows(h)] + jnp.exp2(sk - m[head_rows(h)])
            o_ref[pl.ds(q0, Q_BLOCK), h * HEAD_DIM:(h + 1) * HEAD_DIM] = (
                acc[head_rows(h)] / dh).astype(BF16)

    scores(0, s0_sc, m0_sc)

    def body(k, carry):
        n = 2 * k
        scores(n + 1, s1_sc, m1_sc)
        values(n, s0_sc, m0_sc)
        scores(n + 2, s0_sc, m0_sc)
        values(n + 1, s1_sc, m1_sc)
        return carry

    lax.fori_loop(0, n_blk // 2 - 1, body, 0)
    scores(n_blk - 1, s1_sc, m1_sc)
    values(n_blk - 2, s0_sc, m0_sc)
    values(n_blk - 1, s1_sc, m1_sc)


def _win_attention(zq, cache_k, cache_v, sink2, *, l, n_b, n_lat):
    past = cache_k.shape[2]
    grp = N_HEADS // KV_WIN
    band = 3 * Q_BLOCK
    wq = grp * HEAD_DIM
    qc_blk, kc_blk, vc_blk = OFF_QC // wq, OFF_KC // HEAD_DIM, OFF_VC // HEAD_DIM
    kern = functools.partial(_win_kernel, l=l, past=past, n_lat=n_lat)
    nbytes = (4 * n_lat * wq * 2 + 4 * n_lat * HEAD_DIM * 2 + 4 * past * HEAD_DIM * 4
              + (past + band) * HEAD_DIM * 2 * 3 + 3 * Q_BLOCK * band * 4
              + 4 * grp * Q_BLOCK * (past + band) * 4)
    return pl.pallas_call(
        kern,
        grid=(n_b, KV_WIN),
        in_specs=[
            pl.BlockSpec((n_lat, wq), lambda b, g: (b, qc_blk + g)),
            pl.BlockSpec((n_lat, HEAD_DIM), lambda b, g: (b, kc_blk + g)),
            pl.BlockSpec((n_lat, HEAD_DIM), lambda b, g: (b, vc_blk + g)),
            pl.BlockSpec((None, None, past, HEAD_DIM), lambda b, g: (b, l, 0, g)),
            pl.BlockSpec((None, None, past, HEAD_DIM), lambda b, g: (b, l, 0, g)),
            pl.BlockSpec(memory_space=pltpu.SMEM),
        ],
        out_specs=pl.BlockSpec((n_lat, wq), lambda b, g: (b, g)),
        out_shape=jax.ShapeDtypeStruct((n_b * n_lat, N_HEADS * HEAD_DIM), BF16),
        scratch_shapes=[
            pltpu.VMEM((past, HEAD_DIM), BF16),
            pltpu.VMEM((past, HEAD_DIM), BF16),
            pltpu.VMEM((3, Q_BLOCK, band), F32),
            pltpu.VMEM((grp * Q_BLOCK, past + band), F32),
            pltpu.VMEM((grp * Q_BLOCK, past + band), F32),
            pltpu.VMEM((grp * Q_BLOCK, 1), F32),
            pltpu.VMEM((grp * Q_BLOCK, 1), F32),
        ],
        compiler_params=_params(("parallel", "parallel"), nbytes),
        name="attn_lat_window",
    )(zq, zq, zq, cache_k, cache_v, sink2)


def _merge_kernel(h_ref, oa_ref, ob_ref, oc_ref, wga_ref, wgb_ref, wgc_ref,
                  wba_ref, wbb_ref, wbc_ref, m_ref):
    h = h_ref[...]
    acc = None
    for o_ref, wg_ref, wb_ref in ((oa_ref, wga_ref, wba_ref), (ob_ref, wgb_ref, wbb_ref),
                                  (oc_ref, wgc_ref, wbc_ref)):
        gate = jax.nn.sigmoid(_mm(h, wg_ref[...].astype(BF16)))
        term = gate * _mm(o_ref[...], wb_ref[...].astype(BF16))
        acc = term if acc is None else acc + term
    m_ref[...] = acc.astype(BF16)


def _merge(h, outs, w_in, w_branch, *, l, tm):
    m, d = h.shape
    bw = w_branch.shape[2]
    tn = _divisor_tile(d, 256, 128)
    g0 = OFF_GATES // tn
    per = d // tn

    def o_spec(width, first):
        return pl.BlockSpec((tm, bw), lambda i, j: (i, first))

    in_specs = [pl.BlockSpec((tm, d), lambda i, j: (i, 0))]
    args = [h]
    for o, first in outs:
        in_specs.append(o_spec(bw, first))
        args.append(o)
    for k in range(3):
        in_specs.append(pl.BlockSpec((None, d, tn), lambda i, j, k=k: (l, 0, g0 + k * per + j)))
        args.append(w_in)
    for k in range(3):
        in_specs.append(pl.BlockSpec((None, None, bw, tn), lambda i, j, k=k: (l, k, 0, j)))
        args.append(w_branch)
    nbytes = (2 * tm * d * 2 + 6 * tm * bw * 2 + 6 * d * tn * 4 + 6 * bw * tn * 4
              + 3 * (d + bw) * tn * 2 + 6 * tm * tn * 4)
    return pl.pallas_call(
        _merge_kernel,
        grid=(m // tm, d // tn),
        in_specs=in_specs,
        out_specs=pl.BlockSpec((tm, tn), lambda i, j: (i, j)),
        out_shape=jax.ShapeDtypeStruct((m, d), BF16),
        compiler_params=_params(("parallel", "parallel"), nbytes),
        name="branch_merge",
    )(*args)


def _outproj_kernel(m_ref, w_ref, x_ref, gt_ref, o_ref, *, row_fn):
    r = row_fn(pl.program_id(0))
    y = _mm(m_ref[...], w_ref[...].astype(BF16))
    o_ref[...] = x_ref[...] + gt_ref[pl.ds(r, 1), :] * y


def _outproj(mm, w_o, x, mods, *, l, tm, row_fn):
    m, d = x.shape
    tn = _divisor_tile(d, 512, 128)
    kern = functools.partial(_outproj_kernel, row_fn=row_fn)
    nbytes = 2 * tm * d * 2 + 2 * d * tn * 4 + d * tn * 2 + 5 * tm * tn * 4
    return pl.pallas_call(
        kern,
        grid=(m // tm, d // tn),
        in_specs=[
            pl.BlockSpec((tm, d), lambda i, j: (i, 0)),
            pl.BlockSpec((None, d, tn), lambda i, j: (l, 0, j)),
            pl.BlockSpec((tm, tn), lambda i, j: (i, j)),
            pl.BlockSpec((None, 8, tn), lambda i, j: (l, 0, 5 * (d // tn) + j)),
        ],
        out_specs=pl.BlockSpec((tm, tn), lambda i, j: (i, j)),
        out_shape=jax.ShapeDtypeStruct((m, d), F32),
        compiler_params=_params(("parallel", "parallel"), nbytes),
        name="out_proj",
    )(mm, w_o, x, mods)


def _pack_kv_kernel(*refs, seq):
    n_in = len(refs) - len(KVN_PARTS)
    for l in range(n_in):
        for out_ref, (off, heads) in zip(refs[n_in:], KVN_PARTS):
            for h in range(heads):
                c0 = off + h * HEAD_DIM
                out_ref[l, pl.ds(h, seq, stride=heads), :] = refs[l][:, c0:c0 + HEAD_DIM]


def _pack_context_kv(kvn, *, batch, seq):
    depth = len(kvn)
    kern = functools.partial(_pack_kv_kernel, seq=seq)
    nbytes = 2 * depth * seq * KVN_W * 4 * 2
    return pl.pallas_call(
        kern,
        grid=(batch,),
        in_specs=[pl.BlockSpec((seq, KVN_W), lambda b: (b, 0)) for _ in range(depth)],
        out_specs=[pl.BlockSpec((None, depth, seq * heads, HEAD_DIM), lambda b: (b, 0, 0, 0))
                   for _, heads in KVN_PARTS],
        out_shape=[jax.ShapeDtypeStruct((batch, depth, seq * heads, HEAD_DIM), F32)
                   for _, heads in KVN_PARTS],
        compiler_params=_params(("parallel",), nbytes),
        name="pack_context_kv",
    )(*kvn)


def _rope_tables(n_lat):
    half = HEAD_DIM // 2
    nf = half // 2
    t = np.arange(n_lat)
    row = (t // GRID_W).astype(np.float64)
    col = (t % GRID_W).astype(np.float64)
    inv = 1.0 / (ROPE_THETA ** (np.arange(nf, dtype=np.float64) / nf))
    ang_r = row[:, None] * inv[None, :]
    ang_c = col[:, None] * inv[None, :]
    cr, sr, cc, sc = np.cos(ang_r), np.sin(ang_r), np.cos(ang_c), np.sin(ang_c)
    zero = np.zeros_like(sr)
    cos = np.concatenate([cr, cr, cc, cc], axis=-1)
    sin_a = np.concatenate([-sr, zero, -sc, zero], axis=-1)
    sin_b = np.concatenate([zero, sr, zero, sc], axis=-1)
    return tuple(jnp.asarray(a, F32) for a in (cos, sin_a, sin_b))


def _trunk_layer(x, *, l, tm, row_fn, is_ctx, mods, g_norm, w_ffn1_gu, w_ffn1_down, w_in, g_q, g_k,
                 w_branch, w_o, w_ffn2_gu, w_ffn2_down, attend, rope_tabs, g_final):
    x = _ffn(x, mods, g_norm, w_ffn1_gu, w_ffn1_down, l=l, norm_idx=0, k_shift=0, k_scale=1,
             k_gate=2, tm=tm, row_fn=row_fn)
    res = _inproj(x, mods, g_norm, g_q, g_k, w_in, rope_tabs, l=l, tm=tm, row_fn=row_fn,
                  is_ctx=is_ctx)
    zq, h2 = res[0], res[1]
    outs = attend(zq)
    mm = _merge(h2, outs, w_in, w_branch, l=l, tm=tm)
    x = _outproj(mm, w_o, x, mods, l=l, tm=tm, row_fn=row_fn)
    x = _ffn(x, mods, g_norm, w_ffn2_gu, w_ffn2_down, l=l, norm_idx=2, k_shift=6, k_scale=7,
             k_gate=8, tm=tm, row_fn=row_fn, g_final=g_final)
    return x, (res[2] if is_ctx else None)


def kernel(x_prompt, x_sample, cache_k_gqa, cache_v_gqa, cache_k_nat, cache_v_nat, cache_k_win, cache_v_win, c, c_ctx, w_mod, b_mod, g_norm, w_ffn1_gu, w_ffn1_down, w_in, g_q, g_k, rpb, sink, w_branch, w_o, w_ffn2_gu, w_ffn2_down, g_final):
    batch, seq, d = x_prompt.shape
    n_b, n_lat, _ = x_sample.shape
    depth = w_mod.shape[0]
    past = cache_k_gqa.shape[2]
    rows = n_lat // GRID_W
    assert n_lat % GRID_W == 0 and rows % (2 * NAT_QROWS) == 0 and rows >= NAT_KROWS + NAT_QROWS
    assert n_lat % Q_BLOCK == 0 and n_lat >= 3 * Q_BLOCK and n_b + 1 <= 8

    m_ctx = batch * seq
    tm_ctx = _divisor_tile(m_ctx, 1024, 16)
    tm_lat = _divisor_tile(n_lat, 1024, 16)
    per_lat = n_lat // tm_lat

    cond8 = jnp.zeros((8, d), F32).at[0].set(c_ctx).at[1:1 + n_b].set(c)
    mods = _modulation(cond8, w_mod, b_mod)

    sink2 = sink * LOG2E
    rope_tabs = _rope_tables(n_lat)
    nat_tabs = _nat_row_tables(rpb)
    ck_gqa = cache_k_gqa.reshape(n_b, depth, past, KV_GQA * HEAD_DIM)
    cv_gqa = cache_v_gqa.reshape(n_b, depth, past, KV_GQA * HEAD_DIM)
    ck_nat = cache_k_nat.reshape(n_b, depth, past, N_HEADS * HEAD_DIM)
    cv_nat = cache_v_nat.reshape(n_b, depth, past, N_HEADS * HEAD_DIM)
    ck_win = cache_k_win.reshape(n_b, depth, past, KV_WIN * HEAD_DIM)
    cv_win = cache_v_win.reshape(n_b, depth, past, KV_WIN * HEAD_DIM)

    weights = dict(g_norm=g_norm, w_ffn1_gu=w_ffn1_gu, w_ffn1_down=w_ffn1_down, w_in=w_in,
                   g_q=g_q, g_k=g_k, w_branch=w_branch, w_o=w_o, w_ffn2_gu=w_ffn2_gu,
                   w_ffn2_down=w_ffn2_down, mods=mods)

    y_p = x_prompt.reshape(m_ctx, d)
    y_s = x_sample.reshape(n_b * n_lat, d)
    kvn = []
    for l in range(depth):
        def attend_ctx(zq, l=l):
            o = _ctx_attention(zq, sink2, l=l, batch=batch, seq=seq)
            return [(o, 0), (o, 1), (o, 2)]

        gf = g_final if l == depth - 1 else None
        y_p, kvn_l = _trunk_layer(y_p, l=l, tm=tm_ctx, row_fn=lambda i: 0, is_ctx=True,
                                  attend=attend_ctx, rope_tabs=None, g_final=gf, **weights)
        kvn.append(kvn_l)

        def attend_lat(zq, l=l):
            oa = _lat_gqa_attention(zq, ck_gqa, cv_gqa, l=l, n_b=n_b, n_lat=n_lat)
            ob = _nat_attention(zq, ck_nat, cv_nat, nat_tabs, l=l, n_b=n_b, n_lat=n_lat)
            oc = _win_attention(zq, ck_win, cv_win, sink2, l=l, n_b=n_b, n_lat=n_lat)
            return [(oa, 0), (ob, 0), (oc, 0)]

        y_s, _ = _trunk_layer(y_s, l=l, tm=tm_lat, row_fn=lambda i: 1 + i // per_lat, is_ctx=False,
                              attend=attend_lat, rope_tabs=rope_tabs, g_final=gf, **weights)

    y_prompt = y_p.reshape(batch, seq, d)
    y_sample = y_s.reshape(n_b, n_lat, d)

    new_kv = _pack_context_kv(kvn, batch=batch, seq=seq)
    return (y_prompt, y_sample) + tuple(
        a.reshape(batch, depth, seq, heads, HEAD_DIM) for a, (_, heads) in zip(new_kv, KVN_PARTS))
```

```python
import functools
import math

import numpy as np
import jax
import jax.numpy as jnp
from jax import lax
from jax.experimental import pallas as pl
from jax.experimental.pallas import tpu as pltpu

F32 = jnp.float32
BF16 = jnp.bfloat16

HEAD_DIM = 128
N_HEADS = 8
KV_GQA = 2
KV_WIN = 2
GRID_W = 64
NAT_ROWS = 8
NAT_COLS = 16
WIN_RADIUS = 128
Q_BLOCK = 128
ROPE_THETA = 10000.0
EPS = 1e-6
NEG_INF = -1e30
LOG2E = math.log2(math.e)
Q_SCALE = (HEAD_DIM ** -0.5) * LOG2E

OFF_QA, OFF_KA, OFF_VA = 0, 1024, 1280
OFF_QB, OFF_KB, OFF_VB = 1536, 2560, 3584
OFF_QC, OFF_KC, OFF_VC = 4608, 5632, 5888
OFF_GATES = 6144
QKV_W = 6144
COL_TILE = 512
KVN_W = 3072
KVN_PARTS = ((0, KV_GQA), (256, KV_GQA), (512, N_HEADS), (1536, N_HEADS),
             (2560, KV_WIN), (2816, KV_WIN))

V7X_VMEM_BYTES = 64 * 1024 * 1024
VMEM_CAP_BYTES = 60 * 1024 * 1024
NAT_QROWS = 4
NAT_KROWS = NAT_ROWS + NAT_QROWS


def _vmem_limit(nbytes):
    return int(min(VMEM_CAP_BYTES, max(32 * 1024 * 1024, nbytes * 3 // 2)))


def _params(sem, nbytes):
    return pltpu.CompilerParams(dimension_semantics=sem, vmem_limit_bytes=_vmem_limit(nbytes))


def _divisor_tile(n, cap, mult):
    if n <= cap:
        return n
    t = (cap // mult) * mult
    while t > mult and n % t:
        t -= mult
    assert n % t == 0, (n, cap, mult)
    return t


def _adaln(x, g, shift, scale):
    ms = jnp.mean(x * x, axis=-1, keepdims=True)
    y = x * lax.rsqrt(ms + EPS)
    return (y * g) * (1.0 + scale) + shift


def _headnorm(y, g):
    ms = jnp.mean(y * y, axis=-1, keepdims=True)
    return (y * lax.rsqrt(ms + EPS)) * g


def _silu(a):
    return a * jax.nn.sigmoid(a)


def _qk_t(q, k):
    return lax.dot_general(q, k, (((1,), (1,)), ((), ())), preferred_element_type=F32)


def _mm(a, b):
    return jnp.dot(a, b, preferred_element_type=F32)


def _mod_kernel(c_ref, w_ref, b_ref, o_ref):
    a = _silu(c_ref[...]).astype(BF16)
    o_ref[...] = _mm(a, w_ref[...].astype(BF16)) + b_ref[...]


def _modulation(cond8, w_mod, b_mod):
    depth, d, n = w_mod.shape
    tn = _divisor_tile(n, 1024, 128)
    nbytes = 2 * d * tn * 4 + d * tn * 2 + 4 * 8 * tn * 4 + 8 * d * 4
    return pl.pallas_call(
        _mod_kernel,
        grid=(depth, n // tn),
        in_specs=[
            pl.BlockSpec((8, d), lambda l, j: (0, 0)),
            pl.BlockSpec((None, d, tn), lambda l, j: (l, 0, j)),
            pl.BlockSpec((None, 1, tn), lambda l, j: (l, 0, j)),
        ],
        out_specs=pl.BlockSpec((None, 8, tn), lambda l, j: (l, 0, j)),
        out_shape=jax.ShapeDtypeStruct((depth, 8, n), F32),
        compiler_params=_params(("parallel", "parallel"), nbytes),
        name="modulation",
    )(cond8, w_mod, b_mod.reshape(depth, 1, n))


FFN_SLOTS = 3


def _ffn_kernel(*refs, l, row_fn, norm_idx, tf, d_ff, tm, final_norm):
    if final_norm:
        (x_hbm, g_ref, sh_ref, sc_ref, gt_ref, wgu_hbm, wd_hbm, gf_ref, o_ref,
         x_buf, h_ref, wg_buf, wu_buf, wd_buf, wsem, xsem) = refs
    else:
        (x_hbm, g_ref, sh_ref, sc_ref, gt_ref, wgu_hbm, wd_hbm, o_ref,
         x_buf, h_ref, wg_buf, wu_buf, wd_buf, wsem, xsem) = refs
    n_chunk = d_ff // tf
    d = o_ref.shape[1]
    tn = _divisor_tile(d, 512, 128)
    i = pl.program_id(0)
    n_tiles = pl.num_programs(0)
    lookahead = FFN_SLOTS - 1

    def weight_copies(c, slot):
        col = pl.multiple_of(c * tf, tf)
        return (
            pltpu.make_async_copy(wgu_hbm.at[l, :, pl.ds(col, tf)], wg_buf.at[slot], wsem.at[0, slot]),
            pltpu.make_async_copy(wgu_hbm.at[l, :, pl.ds(d_ff + col, tf)], wu_buf.at[slot],
                                  wsem.at[1, slot]),
            pltpu.make_async_copy(wd_hbm.at[l, pl.ds(col, tf), :], wd_buf.at[slot], wsem.at[2, slot]),
        )

    def rows_copy(tile):
        return pltpu.make_async_copy(x_hbm.at[pl.ds(pl.multiple_of(tile * tm, tm), tm), :],
                                     x_buf, xsem.at[0])

    def start_weights(g):
        for cp in weight_copies(lax.rem(g, n_chunk), lax.rem(g, FFN_SLOTS)):
            cp.start(priority=1)

    def wait_weights(g):
        for cp in weight_copies(lax.rem(g, n_chunk), lax.rem(g, FFN_SLOTS)):
            cp.wait()

    @pl.when(i == 0)
    def _():
        rows_copy(0).start()
        for g in range(lookahead):
            start_weights(jnp.int32(g))

    g0 = i * n_chunk
    r = row_fn(i)
    gate = 0.5 * gt_ref[pl.ds(r, 1), :]

    def chunk(g, h):
        slot = lax.rem(g, FFN_SLOTS)
        a = _mm(h, wg_buf[slot].astype(BF16))
        b = _mm(h, wu_buf[slot].astype(BF16))
        t = (_silu(a) * b).astype(BF16)
        for n in range(d // tn):
            cs = slice(n * tn, (n + 1) * tn)
            o_ref[:, cs] += gate[:, cs] * _mm(t, wd_buf[slot, :, cs].astype(BF16))

    rows_copy(i).wait()
    start_weights(g0 + lookahead)
    wait_weights(g0)
    x = x_buf[...]
    h = _adaln(x, g_ref[norm_idx:norm_idx + 1, :],
               sh_ref[pl.ds(r, 1), :], sc_ref[pl.ds(r, 1), :]).astype(BF16)
    h_ref[...] = h
    o_ref[...] = x
    chunk(g0, h)

    @pl.when(i + 1 < n_tiles)
    def _():
        rows_copy(i + 1).start()

    def body(c, carry):
        g = g0 + c
        start_weights(g + lookahead)
        wait_weights(g)
        chunk(g, h_ref[...])
        return carry

    lax.fori_loop(1, n_chunk, body, 0, unroll=2)

    @pl.when(i + 1 == n_tiles)
    def _():
        for k in range(lookahead):
            wait_weights(g0 + n_chunk + k)

    if final_norm:
        y = o_ref[...]
        ms = jnp.mean(y * y, axis=-1, keepdims=True)
        o_ref[...] = (y * lax.rsqrt(ms + EPS)) * gf_ref[...]


def _ffn(x, mods, g_norm, w_gu, w_down, *, l, norm_idx, k_shift, k_scale, k_gate, tm, row_fn,
         g_final=None):
    m, d = x.shape
    d_ff = w_down.shape[1]
    tf = _divisor_tile(d_ff, 256, 128)
    final_norm = g_final is not None
    nbytes = (3 * tm * d * 4 + tm * d * 2 + 3 * FFN_SLOTS * d * tf * 4 + 3 * d * tf * 2
              + 3 * tm * tf * 4 + 2 * tm * 512 * 4)
    kern = functools.partial(_ffn_kernel, l=l, row_fn=row_fn, norm_idx=norm_idx, tf=tf, d_ff=d_ff,
                             tm=tm, final_norm=final_norm)
    in_specs = [
        pl.BlockSpec(memory_space=pl.ANY),
        pl.BlockSpec((None, 3, d), lambda i: (l, 0, 0)),
        pl.BlockSpec((None, 8, d), lambda i: (l, 0, k_shift)),
        pl.BlockSpec((None, 8, d), lambda i: (l, 0, k_scale)),
        pl.BlockSpec((None, 8, d), lambda i: (l, 0, k_gate)),
        pl.BlockSpec(memory_space=pl.ANY),
        pl.BlockSpec(memory_space=pl.ANY),
    ]
    args = [x, g_norm, mods, mods, mods, w_gu, w_down]
    if final_norm:
        in_specs.append(pl.BlockSpec((1, d), lambda i: (0, 0)))
        args.append(g_final.reshape(1, d))
    return pl.pallas_call(
        kern,
        grid=(m // tm,),
        in_specs=in_specs,
        out_specs=pl.BlockSpec((tm, d), lambda i: (i, 0)),
        out_shape=jax.ShapeDtypeStruct((m, d), F32),
        scratch_shapes=[
            pltpu.VMEM((tm, d), F32),
            pltpu.VMEM((tm, d), BF16),
            pltpu.VMEM((FFN_SLOTS, d, tf), F32),
            pltpu.VMEM((FFN_SLOTS, d, tf), F32),
            pltpu.VMEM((FFN_SLOTS, tf, d), F32),
            pltpu.SemaphoreType.DMA((3, FFN_SLOTS)),
            pltpu.SemaphoreType.DMA((1,)),
        ],
        compiler_params=_params(("arbitrary",), nbytes),
        name="ffn_final" if final_norm else "ffn",
    )(*args)


def _kvn_block(j):
    return jnp.where(j <= 2, 0, jnp.where(j >= 9, 5, jnp.clip(j - 4, 1, 4)))


def _inproj_kernel(*refs, is_ctx, row_fn, norm_idx):
    if is_ctx:
        (x_ref, g_ref, sh_ref, sc_ref, gq_ref, gk_ref, w_ref, zq_ref, h_ref, kvn_ref,
         acc_ref) = refs
    else:
        (x_ref, g_ref, sh_ref, sc_ref, gq_ref, gk_ref, cos_ref, sa_ref, sb_ref, w_ref,
         zq_ref, h_ref, acc_ref) = refs
    j = pl.program_id(1)

    def rope(y):
        if is_ctx:
            return y
        return (y * cos_ref[...] + pltpu.roll(y, 96, 1) * sa_ref[...]
                + pltpu.roll(y, 32, 1) * sb_ref[...])

    def cols(hh):
        return slice(hh * HEAD_DIM, (hh + 1) * HEAD_DIM)

    def project(h):
        return _mm(h, w_ref[...].astype(BF16))

    def query_norm_rope(acc):
        for hh in range(4):
            y = _headnorm(acc[:, cols(hh)], gq_ref[...])
            zq_ref[:, cols(hh)] = (rope(y) * Q_SCALE).astype(BF16)

    @pl.when(j == 0)
    def _():
        r = row_fn(pl.program_id(0))
        h = _adaln(x_ref[...], g_ref[norm_idx:norm_idx + 1, :],
                   sh_ref[pl.ds(r, 1), :], sc_ref[pl.ds(r, 1), :]).astype(BF16)
        h_ref[...] = h
        acc_ref[...] = project(h)

    @pl.when((j == 1) | (j == 2))
    def _():
        acc_ref[...] = project(h_ref[...])

    @pl.when(j < 2)
    def _():
        query_norm_rope(acc_ref[...])

    @pl.when(j == 2)
    def _():
        acc = acc_ref[...]
        for hh in range(2):
            y = _headnorm(acc[:, cols(hh)], gk_ref[...])
            if is_ctx:
                kvn_ref[:, cols(hh)] = y
            zq_ref[:, cols(hh)] = rope(y).astype(BF16)
        va = acc[:, 256:512]
        if is_ctx:
            kvn_ref[:, 256:512] = va
        zq_ref[:, 256:512] = va.astype(BF16)

    @pl.when((j == 3) | (j == 4))
    def _():
        zq_ref[...] = (project(h_ref[...]) * Q_SCALE).astype(BF16)

    @pl.when((j >= 5) & (j <= 8))
    def _():
        acc = project(h_ref[...])
        if is_ctx:
            kvn_ref[...] = acc
        zq_ref[...] = acc.astype(BF16)

    @pl.when((j == 9) | (j == 10))
    def _():
        acc = project(h_ref[...])
        for hh in range(4):
            zq_ref[:, cols(hh)] = (rope(acc[:, cols(hh)]) * Q_SCALE).astype(BF16)

    @pl.when(j == 11)
    def _():
        acc = project(h_ref[...])
        for hh in range(2):
            y = acc[:, cols(hh)]
            if is_ctx:
                kvn_ref[:, cols(hh)] = y
            zq_ref[:, cols(hh)] = rope(y).astype(BF16)
        vc = acc[:, 256:512]
        if is_ctx:
            kvn_ref[:, 256:512] = vc
        zq_ref[:, 256:512] = vc.astype(BF16)


def _inproj(x, mods, g_norm, g_q, g_k, w_in, rope_tabs, *, l, tm, row_fn, is_ctx):
    m, d = x.shape
    depth = w_in.shape[0]
    nj = QKV_W // COL_TILE
    in_specs = [
        pl.BlockSpec((tm, d), lambda i, j: (i, 0)),
        pl.BlockSpec((None, 3, d), lambda i, j: (l, 0, 0)),
        pl.BlockSpec((None, 8, d), lambda i, j: (l, 0, 3)),
        pl.BlockSpec((None, 8, d), lambda i, j: (l, 0, 4)),
        pl.BlockSpec((None, 1, HEAD_DIM), lambda i, j: (l, 0, 0)),
        pl.BlockSpec((None, 1, HEAD_DIM), lambda i, j: (l, 0, 0)),
    ]
    args = [x, g_norm, mods, mods, g_q.reshape(depth, 1, HEAD_DIM), g_k.reshape(depth, 1, HEAD_DIM)]
    if not is_ctx:
        n_pos = rope_tabs[0].shape[0]
        per = n_pos // tm
        for tab in rope_tabs:
            in_specs.append(pl.BlockSpec((tm, HEAD_DIM), lambda i, j: (i % per, 0)))
            args.append(tab)
    in_specs.append(pl.BlockSpec((None, d, COL_TILE), lambda i, j: (l, 0, j)))
    args.append(w_in)
    out_specs = [
        pl.BlockSpec((tm, COL_TILE), lambda i, j: (i, j)),
        pl.BlockSpec((tm, d), lambda i, j: (i, 0)),
    ]
    out_shape = [jax.ShapeDtypeStruct((m, QKV_W), BF16), jax.ShapeDtypeStruct((m, d), BF16)]
    if is_ctx:
        out_specs.append(pl.BlockSpec((tm, COL_TILE), lambda i, j: (i, _kvn_block(j))))
        out_shape.append(jax.ShapeDtypeStruct((m, KVN_W), F32))
    nbytes = (2 * tm * d * 4 + 2 * tm * d * 2 + 2 * d * COL_TILE * 4 + d * COL_TILE * 2
              + 3 * tm * COL_TILE * 4 + 2 * tm * COL_TILE * 2 + 6 * tm * HEAD_DIM * 4)
    kern = functools.partial(_inproj_kernel, is_ctx=is_ctx, row_fn=row_fn, norm_idx=1)
    return pl.pallas_call(
        kern,
        grid=(m // tm, nj),
        in_specs=in_specs,
        out_specs=out_specs,
        out_shape=out_shape,
        scratch_shapes=[pltpu.VMEM((tm, COL_TILE), F32)],
        compiler_params=_params(("parallel", "arbitrary"), nbytes),
        name="inproj_ctx" if is_ctx else "inproj_lat",
    )(*args)


def _ctx_attn_kernel(z_ref, sink_ref, o_ref, *, l, seq):
    def attend(q_off, n_q, k_off, v_off, sinks):
        q = jnp.concatenate(
            [z_ref[:, q_off + h * HEAD_DIM:q_off + (h + 1) * HEAD_DIM] for h in range(n_q)], axis=0)
        k = z_ref[:, k_off:k_off + HEAD_DIM]
        v = z_ref[:, v_off:v_off + HEAD_DIM]
        s = _qk_t(q, k)
        m = jnp.max(s, axis=-1, keepdims=True)
        if sinks is not None:
            sink_col = jnp.concatenate(
                [jnp.full((seq, 1), sk, F32) for sk in sinks], axis=0)
            m = jnp.maximum(m, sink_col)
        p = jnp.exp2(s - m)
        den = jnp.sum(p, axis=-1, keepdims=True)
        if sinks is not None:
            den = den + jnp.exp2(sink_col - m)
        return _mm(p.astype(BF16), v) / den

    def store(o, o_off, n_q):
        for h in range(n_q):
            o_ref[:, o_off + h * HEAD_DIM:o_off + (h + 1) * HEAD_DIM] = (
                o[h * seq:(h + 1) * seq].astype(BF16))

    grp = N_HEADS // KV_GQA
    for g in range(KV_GQA):
        o = attend(OFF_QA + g * grp * HEAD_DIM, grp, OFF_KA + g * HEAD_DIM,
                   OFF_VA + g * HEAD_DIM, None)
        store(o, g * grp * HEAD_DIM, grp)
    for h in range(N_HEADS):
        o = attend(OFF_QB + h * HEAD_DIM, 1, OFF_KB + h * HEAD_DIM, OFF_VB + h * HEAD_DIM, None)
        store(o, 1024 + h * HEAD_DIM, 1)
    grp = N_HEADS // KV_WIN
    for g in range(KV_WIN):
        sinks = [sink_ref[l, g * grp + h] for h in range(grp)]
        o = attend(OFF_QC + g * grp * HEAD_DIM, grp, OFF_KC + g * HEAD_DIM,
                   OFF_VC + g * HEAD_DIM, sinks)
        store(o, 2048 + g * grp * HEAD_DIM, grp)


def _ctx_attention(zq, sink2, *, l, batch, seq):
    kern = functools.partial(_ctx_attn_kernel, l=l, seq=seq)
    nbytes = 2 * seq * QKV_W * 2 + 2 * seq * 3072 * 2 + 8 * 4 * seq * seq * 4
    return pl.pallas_call(
        kern,
        grid=(batch,),
        in_specs=[
            pl.BlockSpec((seq, QKV_W), lambda b: (b, 0)),
            pl.BlockSpec(memory_space=pltpu.SMEM),
        ],
        out_specs=pl.BlockSpec((seq, 3072), lambda b: (b, 0)),
        out_shape=jax.ShapeDtypeStruct((batch * seq, 3072), BF16),
        compiler_params=_params(("parallel",), nbytes),
        name="attn_ctx",
    )(zq, sink2)


def _lat_gqa_kernel(q_ref, kl_ref, vl_ref, kc_ref, vc_ref, o_ref, k_sc, v_sc, s0_sc, s1_sc,
                    m0_sc, m1_sc, *, past, n_lat, tq, chunk):
    grp = N_HEADS // KV_GQA
    total = past + n_lat
    n_chunk = total // chunk
    nq = n_lat // tq
    k_sc[0:past, :] = kc_ref[...].astype(BF16)
    k_sc[past:total, :] = kl_ref[...]
    v_sc[0:past, 0:HEAD_DIM] = vc_ref[...].astype(BF16)
    v_sc[past:total, 0:HEAD_DIM] = vl_ref[...]
    v_sc[:, HEAD_DIM:2 * HEAD_DIM] = jnp.ones((total, HEAD_DIM), BF16)

    def row0(t):
        return t * tq if isinstance(t, int) else pl.multiple_of(t * tq, tq)

    def scores(t, s_sc, m_sc):
        qn = q_ref[pl.ds(row0(t), tq), :]
        q = jnp.concatenate([qn[:, h * HEAD_DIM:(h + 1) * HEAD_DIM] for h in range(grp)], axis=0)
        m = None
        for c in range(n_chunk):
            s = _qk_t(q, k_sc[c * chunk:(c + 1) * chunk, :])
            s_sc[:, c * chunk:(c + 1) * chunk] = s
            mc = jnp.max(s, axis=-1, keepdims=True)
            m = mc if m is None else jnp.maximum(m, mc)
        m_sc[...] = m

    def values(t, s_sc, m_sc):
        m = m_sc[...]
        acc = None
        for c in range(n_chunk):
            p = jnp.exp2(s_sc[:, c * chunk:(c + 1) * chunk] - m).astype(BF16)
            term = _mm(p, v_sc[c * chunk:(c + 1) * chunk, :])
            acc = term if acc is None else acc + term
        o = acc[:, 0:HEAD_DIM] / acc[:, HEAD_DIM:2 * HEAD_DIM]
        for h in range(grp):
            o_ref[pl.ds(row0(t), tq), h * HEAD_DIM:(h + 1) * HEAD_DIM] = (
                o[h * tq:(h + 1) * tq].astype(BF16))

    scores(0, s0_sc, m0_sc)

    def body(k, carry):
        t = 2 * k
        scores(t + 1, s1_sc, m1_sc)
        values(t, s0_sc, m0_sc)
        scores(t + 2, s0_sc, m0_sc)
        values(t + 1, s1_sc, m1_sc)
        return carry

    lax.fori_loop(0, nq // 2 - 1, body, 0)
    scores(nq - 1, s1_sc, m1_sc)
    values(nq - 2, s0_sc, m0_sc)
    values(nq - 1, s1_sc, m1_sc)


def _lat_gqa_attention(zq, cache_k, cache_v, *, l, n_b, n_lat):
    past = cache_k.shape[2]
    grp = N_HEADS // KV_GQA
    tq = _divisor_tile(n_lat, 128, 16)
    assert (n_lat // tq) % 2 == 0
    total = past + n_lat
    chunk = _divisor_tile(total, 1536, 128)
    wq = grp * HEAD_DIM
    ka_blk = OFF_KA // HEAD_DIM
    va_blk = OFF_VA // HEAD_DIM
    kern = functools.partial(_lat_gqa_kernel, past=past, n_lat=n_lat, tq=tq, chunk=chunk)
    nbytes = (4 * n_lat * wq * 2 + 4 * n_lat * HEAD_DIM * 2 + 4 * past * HEAD_DIM * 4
              + total * HEAD_DIM * 2 * 3 + 2 * grp * tq * total * 4 + 4 * grp * tq * chunk * 4)
    return pl.pallas_call(
        kern,
        grid=(n_b, KV_GQA),
        in_specs=[
            pl.BlockSpec((n_lat, wq), lambda b, g: (b, g)),
            pl.BlockSpec((n_lat, HEAD_DIM), lambda b, g: (b, ka_blk + g)),
            pl.BlockSpec((n_lat, HEAD_DIM), lambda b, g: (b, va_blk + g)),
            pl.BlockSpec((None, None, past, HEAD_DIM), lambda b, g: (b, l, 0, g)),
            pl.BlockSpec((None, None, past, HEAD_DIM), lambda b, g: (b, l, 0, g)),
        ],
        out_specs=pl.BlockSpec((n_lat, wq), lambda b, g: (b, g)),
        out_shape=jax.ShapeDtypeStruct((n_b * n_lat, N_HEADS * HEAD_DIM), BF16),
        scratch_shapes=[
            pltpu.VMEM((total, HEAD_DIM), BF16),
            pltpu.VMEM((total, 2 * HEAD_DIM), BF16),
            pltpu.VMEM((grp * tq, total), F32),
            pltpu.VMEM((grp * tq, total), F32),
            pltpu.VMEM((grp * tq, 1), F32),
            pltpu.VMEM((grp * tq, 1), F32),
        ],
        compiler_params=_params(("parallel", "parallel"), nbytes),
        name="attn_lat_global",
    )(zq, zq, zq, cache_k, cache_v)


def _nat_row_tables(rpb):
    w, wc = GRID_W, NAT_COLS
    c = np.arange(w)
    cs = np.clip(c - wc // 2, 0, w - wc)
    col_ok = (c[None, :] >= cs[:, None]) & (c[None, :] < cs[:, None] + wc)
    dc = np.clip(c[None, :] - c[:, None] + wc - 1, 0, 2 * wc - 2)
    t_tab = jnp.zeros(rpb.shape[:3] + (w, w), F32)
    for d in range(2 * wc - 1):
        t_tab = jnp.where(jnp.asarray(dc == d), rpb[..., d][..., None, None].astype(F32), t_tab)
    return jnp.where(jnp.asarray(col_ok), t_tab, NEG_INF) * LOG2E


def _nat_tile_plan(rows):
    wr = NAT_ROWS
    plan = []
    for r0 in (0, NAT_QROWS, rows - NAT_QROWS):
        ws = int(np.clip(r0 - wr // 2, 0, rows - NAT_KROWS))
        cls = []
        for a in range(NAT_QROWS):
            qr = r0 + a
            rs = int(np.clip(qr - wr // 2, 0, rows - wr))
            cls.append([(ws + b) - qr + wr - 1 if rs <= ws + b < rs + wr else None
                        for b in range(NAT_KROWS)])
        plan.append(cls)
    return plan


def _nat_kernel(q_ref, k_ref, v_ref, kc_ref, vc_ref, t_ref, o_ref, kc_sc, vc_sc, bias_ref,
                s0_sc, s1_sc, m0_sc, m1_sc, *, rows):
    tq = NAT_QROWS * GRID_W
    tk = NAT_KROWS * GRID_W
    n_tiles = rows // NAT_QROWS
    past = kc_ref.shape[0]
    kc_sc[...] = kc_ref[...].astype(BF16)
    vc_sc[...] = vc_ref[...].astype(BF16)
    neg_blk = jnp.full((GRID_W, GRID_W), NEG_INF * LOG2E, F32)
    for cls, cls_plan in enumerate(_nat_tile_plan(rows)):
        for a, row_plan in enumerate(cls_plan):
            for b, dr in enumerate(row_plan):
                bias_ref[cls, a * GRID_W:(a + 1) * GRID_W, b * GRID_W:(b + 1) * GRID_W] = (
                    neg_blk if dr is None else t_ref[dr])

    def window(t):
        if isinstance(t, int):
            ws = min(max(t * NAT_QROWS - NAT_ROWS // 2, 0), rows - NAT_KROWS)
            cls = 0 if t == 0 else (2 if t == n_tiles - 1 else 1)
            return t * tq, ws * GRID_W, cls
        ws = jnp.clip(t * NAT_QROWS - NAT_ROWS // 2, 0, rows - NAT_KROWS)
        cls = jnp.where(t == 0, 0, jnp.where(t == n_tiles - 1, 2, 1))
        return pl.multiple_of(t * tq, tq), pl.multiple_of(ws * GRID_W, GRID_W), cls

    def scores(t, s_sc, m_sc):
        q0, k0, cls = window(t)
        q = q_ref[pl.ds(q0, tq), :]
        s_loc = _qk_t(q, k_ref[pl.ds(k0, tk), :]) + bias_ref[cls]
        s_ctx = _qk_t(q, kc_sc[...])
        s_sc[:, 0:tk] = s_loc
        s_sc[:, tk:tk + past] = s_ctx
        m_sc[...] = jnp.maximum(jnp.max(s_loc, axis=-1, keepdims=True),
                                jnp.max(s_ctx, axis=-1, keepdims=True))

    def values(t, s_sc, m_sc):
        q0, k0, _ = window(t)
        m = m_sc[...]
        p_loc = jnp.exp2(s_sc[:, 0:tk] - m)
        p_ctx = jnp.exp2(s_sc[:, tk:tk + past] - m)
        den = jnp.sum(p_loc, axis=-1, keepdims=True) + jnp.sum(p_ctx, axis=-1, keepdims=True)
        o = _mm(p_ctx.astype(BF16), vc_sc[...]) + _mm(p_loc.astype(BF16), v_ref[pl.ds(k0, tk), :])
        o_ref[pl.ds(q0, tq), :] = (o / den).astype(BF16)

    scores(0, s0_sc, m0_sc)

    def body(k, carry):
        t = 2 * k
        scores(t + 1, s1_sc, m1_sc)
        values(t, s0_sc, m0_sc)
        scores(t + 2, s0_sc, m0_sc)
        values(t + 1, s1_sc, m1_sc)
        return carry

    lax.fori_loop(0, n_tiles // 2 - 1, body, 0)
    scores(n_tiles - 1, s1_sc, m1_sc)
    values(n_tiles - 2, s0_sc, m0_sc)
    values(n_tiles - 1, s1_sc, m1_sc)


def _nat_attention(zq, cache_k, cache_v, row_tabs, *, l, n_b, n_lat):
    past = cache_k.shape[2]
    rows = n_lat // GRID_W
    tq = NAT_QROWS * GRID_W
    tk = NAT_KROWS * GRID_W
    n_dr = row_tabs.shape[2]
    qb, kb, vb = OFF_QB // HEAD_DIM, OFF_KB // HEAD_DIM, OFF_VB // HEAD_DIM
    kern = functools.partial(_nat_kernel, rows=rows)
    nbytes = (8 * n_lat * HEAD_DIM * 2 + 4 * past * HEAD_DIM * 4 + 2 * past * HEAD_DIM * 2
              + 2 * n_dr * GRID_W * 128 * 4 + 3 * tq * tk * 4 + 6 * tq * (tk + past) * 4)
    return pl.pallas_call(
        kern,
        grid=(n_b, N_HEADS),
        in_specs=[
            pl.BlockSpec((n_lat, HEAD_DIM), lambda b, h: (b, qb + h)),
            pl.BlockSpec((n_lat, HEAD_DIM), lambda b, h: (b, kb + h)),
            pl.BlockSpec((n_lat, HEAD_DIM), lambda b, h: (b, vb + h)),
            pl.BlockSpec((None, None, past, HEAD_DIM), lambda b, h: (b, l, 0, h)),
            pl.BlockSpec((None, None, past, HEAD_DIM), lambda b, h: (b, l, 0, h)),
            pl.BlockSpec((None, None, n_dr, GRID_W, GRID_W), lambda b, h: (l, h, 0, 0, 0)),
        ],
        out_specs=pl.BlockSpec((n_lat, HEAD_DIM), lambda b, h: (b, h)),
        out_shape=jax.ShapeDtypeStruct((n_b * n_lat, N_HEADS * HEAD_DIM), BF16),
        scratch_shapes=[pltpu.VMEM((past, HEAD_DIM), BF16), pltpu.VMEM((past, HEAD_DIM), BF16),
                        pltpu.VMEM((3, tq, tk), F32),
                        pltpu.VMEM((tq, tk + past), F32), pltpu.VMEM((tq, tk + past), F32),
                        pltpu.VMEM((tq, 1), F32), pltpu.VMEM((tq, 1), F32)],
        compiler_params=_params(("parallel", "parallel"), nbytes),
        name="attn_lat_nat",
    )(zq, zq, zq, cache_k, cache_v, row_tabs)


def _win_kernel(q_ref, k_ref, v_ref, kc_ref, vc_ref, sink_ref, o_ref, kc_sc, vc_sc, bias_sc,
                s0_sc, s1_sc, m0_sc, m1_sc, *, l, past, n_lat):
    grp = N_HEADS // KV_WIN
    band = 3 * Q_BLOCK
    n_blk = n_lat // Q_BLOCK
    g = pl.program_id(1)
    kc_sc[...] = kc_ref[...].astype(BF16)
    vc_sc[...] = vc_ref[...].astype(BF16)
    qi = lax.broadcasted_iota(jnp.int32, (Q_BLOCK, band), 0)
    kj = lax.broadcasted_iota(jnp.int32, (Q_BLOCK, band), 1)
    for cls, off in enumerate((0, -Q_BLOCK, -2 * Q_BLOCK)):
        bias_sc[cls] = jnp.where(jnp.abs(kj + off - qi) <= WIN_RADIUS, 0.0, NEG_INF).astype(F32)

    def window(n):
        if isinstance(n, int):
            start = min(max((n - 1) * Q_BLOCK, 0), n_lat - band)
            return n * Q_BLOCK, start, 0 if n == 0 else (2 if n == n_blk - 1 else 1)
        start = pl.multiple_of(jnp.clip((n - 1) * Q_BLOCK, 0, n_lat - band), Q_BLOCK)
        cls = jnp.where(n == 0, 0, jnp.where(n == n_blk - 1, 2, 1))
        return pl.multiple_of(n * Q_BLOCK, Q_BLOCK), start, cls

    def head_rows(h):
        return slice(h * Q_BLOCK, (h + 1) * Q_BLOCK)

    def scores(n, s_sc, m_sc):
        q0, start, cls = window(n)
        qn = q_ref[pl.ds(q0, Q_BLOCK), :]
        q = jnp.concatenate([qn[:, h * HEAD_DIM:(h + 1) * HEAD_DIM] for h in range(grp)], axis=0)
        s_ctx = _qk_t(q, kc_sc[...])
        s_loc = _qk_t(q, k_ref[pl.ds(start, band), :])
        bias = bias_sc[cls]
        s_sc[:, 0:past] = s_ctx
        for h in range(grp):
            sl = s_loc[head_rows(h)] + bias
            s_sc[head_rows(h), past:past + band] = sl
            sk = sink_ref[l, g * grp + h]
            m_sc[head_rows(h), :] = jnp.maximum(
                jnp.maximum(jnp.max(s_ctx[head_rows(h)], axis=-1, keepdims=True),
                            jnp.max(sl, axis=-1, keepdims=True)), sk)

    def values(n, s_sc, m_sc):
        q0, start, _ = window(n)
        m = m_sc[...]
        pc = jnp.exp2(s_sc[:, 0:past] - m)
        pw = jnp.exp2(s_sc[:, past:past + band] - m)
        den = jnp.sum(pc, axis=-1, keepdims=True) + jnp.sum(pw, axis=-1, keepdims=True)
        acc = (_mm(pc.astype(BF16), vc_sc[...])
               + _mm(pw.astype(BF16), v_ref[pl.ds(start, band), :]))
        for h in range(grp):
            sk = sink_ref[l, g * grp + h]
            dh = den[head_rows(h)] + jnp.exp2(sk - m[head_rows(h)])
            o_ref[pl.ds(q0, Q_BLOCK), h * HEAD_DIM:(h + 1) * HEAD_DIM] = (
                acc[head_rows(h)] / dh).astype(BF16)

    scores(0, s0_sc, m0_sc)

    def body(k, carry):
        n = 2 * k
        scores(n + 1, s1_sc, m1_sc)
        values(n, s0_sc, m0_sc)
        scores(n + 2, s0_sc, m0_sc)
        values(n + 1, s1_sc, m1_sc)
        return carry

    lax.fori_loop(0, n_blk // 2 - 1, body, 0)
    scores(n_blk - 1, s1_sc, m1_sc)
    values(n_blk - 2, s0_sc, m0_sc)
    values(n_blk - 1, s1_sc, m1_sc)


def _win_attention(zq, cache_k, cache_v, sink2, *, l, n_b, n_lat):
    past = cache_k.shape[2]
    grp = N_HEADS // KV_WIN
    band = 3 * Q_BLOCK
    wq = grp * HEAD_DIM
    qc_blk, kc_blk, vc_blk = OFF_QC // wq, OFF_KC // HEAD_DIM, OFF_VC // HEAD_DIM
    kern = functools.partial(_win_kernel, l=l, past=past, n_lat=n_lat)
    nbytes = (4 * n_lat * wq * 2 + 4 * n_lat * HEAD_DIM * 2 + 4 * past * HEAD_DIM * 4
              + (past + band) * HEAD_DIM * 2 * 3 + 3 * Q_BLOCK * band * 4
              + 4 * grp * Q_BLOCK * (past + band) * 4)
    return pl.pallas_call(
        kern,
        grid=(n_b, KV_WIN),
        in_specs=[
            pl.BlockSpec((n_lat, wq), lambda b, g: (b, qc_blk + g)),
            pl.BlockSpec((n_lat, HEAD_DIM), lambda b, g: (b, kc_blk + g)),
            pl.BlockSpec((n_lat, HEAD_DIM), lambda b, g: (b, vc_blk + g)),
            pl.BlockSpec((None, None, past, HEAD_DIM), lambda b, g: (b, l, 0, g)),
            pl.BlockSpec((None, None, past, HEAD_DIM), lambda b, g: (b, l, 0, g)),
            pl.BlockSpec(memory_space=pltpu.SMEM),
        ],
        out_specs=pl.BlockSpec((n_lat, wq), lambda b, g: (b, g)),
        out_shape=jax.ShapeDtypeStruct((n_b * n_lat, N_HEADS * HEAD_DIM), BF16),
        scratch_shapes=[
            pltpu.VMEM((past, HEAD_DIM), BF16),
            pltpu.VMEM((past, HEAD_DIM), BF16),
            pltpu.VMEM((3, Q_BLOCK, band), F32),
            pltpu.VMEM((grp * Q_BLOCK, past + band), F32),
            pltpu.VMEM((grp * Q_BLOCK, past + band), F32),
            pltpu.VMEM((grp * Q_BLOCK, 1), F32),
            pltpu.VMEM((grp * Q_BLOCK, 1), F32),
        ],
        compiler_params=_params(("parallel", "parallel"), nbytes),
        name="attn_lat_window",
    )(zq, zq, zq, cache_k, cache_v, sink2)


def _merge_kernel(h_ref, oa_ref, ob_ref, oc_ref, wga_ref, wgb_ref, wgc_ref,
                  wba_ref, wbb_ref, wbc_ref, m_ref):
    h = h_ref[...]
    acc = None
    for o_ref, wg_ref, wb_ref in ((oa_ref, wga_ref, wba_ref), (ob_ref, wgb_ref, wbb_ref),
                                  (oc_ref, wgc_ref, wbc_ref)):
        gate = jax.nn.sigmoid(_mm(h, wg_ref[...].astype(BF16)))
        term = gate * _mm(o_ref[...], wb_ref[...].astype(BF16))
        acc = term if acc is None else acc + term
    m_ref[...] = acc.astype(BF16)


def _merge(h, outs, w_in, w_branch, *, l, tm):
    m, d = h.shape
    bw = w_branch.shape[2]
    tn = _divisor_tile(d, 256, 128)
    g0 = OFF_GATES // tn
    per = d // tn

    def o_spec(width, first):
        return pl.BlockSpec((tm, bw), lambda i, j: (i, first))

    in_specs = [pl.BlockSpec((tm, d), lambda i, j: (i, 0))]
    args = [h]
    for o, first in outs:
        in_specs.append(o_spec(bw, first))
        args.append(o)
    for k in range(3):
        in_specs.append(pl.BlockSpec((None, d, tn), lambda i, j, k=k: (l, 0, g0 + k * per + j)))
        args.append(w_in)
    for k in range(3):
        in_specs.append(pl.BlockSpec((None, None, bw, tn), lambda i, j, k=k: (l, k, 0, j)))
        args.append(w_branch)
    nbytes = (2 * tm * d * 2 + 6 * tm * bw * 2 + 6 * d * tn * 4 + 6 * bw * tn * 4
              + 3 * (d + bw) * tn * 2 + 6 * tm * tn * 4)
    return pl.pallas_call(
        _merge_kernel,
        grid=(m // tm, d // tn),
        in_specs=in_specs,
        out_specs=pl.BlockSpec((tm, tn), lambda i, j: (i, j)),
        out_shape=jax.ShapeDtypeStruct((m, d), BF16),
        compiler_params=_params(("parallel", "parallel"), nbytes),
        name="branch_merge",
    )(*args)


def _outproj_kernel(m_ref, w_ref, x_ref, gt_ref, o_ref, *, row_fn):
    r = row_fn(pl.program_id(0))
    y = _mm(m_ref[...], w_ref[...].astype(BF16))
    o_ref[...] = x_ref[...] + gt_ref[pl.ds(r, 1), :] * y


def _outproj(mm, w_o, x, mods, *, l, tm, row_fn):
    m, d = x.shape
    tn = _divisor_tile(d, 512, 128)
    kern = functools.partial(_outproj_kernel, row_fn=row_fn)
    nbytes = 2 * tm * d * 2 + 2 * d * tn * 4 + d * tn * 2 + 5 * tm * tn * 4
    return pl.pallas_call(
        kern,
        grid=(m // tm, d // tn),
        in_specs=[
            pl.BlockSpec((tm, d), lambda i, j: (i, 0)),
            pl.BlockSpec((None, d, tn), lambda i, j: (l, 0, j)),
            pl.BlockSpec((tm, tn), lambda i, j: (i, j)),
            pl.BlockSpec((None, 8, tn), lambda i, j: (l, 0, 5 * (d // tn) + j)),
        ],
        out_specs=pl.BlockSpec((tm, tn), lambda i, j: (i, j)),
        out_shape=jax.ShapeDtypeStruct((m, d), F32),
        compiler_params=_params(("parallel", "parallel"), nbytes),
        name="out_proj",
    )(mm, w_o, x, mods)


def _pack_kv_kernel(*refs, seq):
    n_in = len(refs) - len(KVN_PARTS)
    for l in range(n_in):
        for out_ref, (off, heads) in zip(refs[n_in:], KVN_PARTS):
            for h in range(heads):
                c0 = off + h * HEAD_DIM
                out_ref[l, pl.ds(h, seq, stride=heads), :] = refs[l][:, c0:c0 + HEAD_DIM]


def _pack_context_kv(kvn, *, batch, seq):
    depth = len(kvn)
    kern = functools.partial(_pack_kv_kernel, seq=seq)
    nbytes = 2 * depth * seq * KVN_W * 4 * 2
    return pl.pallas_call(
        kern,
        grid=(batch,),
        in_specs=[pl.BlockSpec((seq, KVN_W), lambda b: (b, 0)) for _ in range(depth)],
        out_specs=[pl.BlockSpec((None, depth, seq * heads, HEAD_DIM), lambda b: (b, 0, 0, 0))
                   for _, heads in KVN_PARTS],
        out_shape=[jax.ShapeDtypeStruct((batch, depth, seq * heads, HEAD_DIM), F32)
                   for _, heads in KVN_PARTS],
        compiler_params=_params(("parallel",), nbytes),
        name="pack_context_kv",
    )(*kvn)


def _rope_tables(n_lat):
    half = HEAD_DIM // 2
    nf = half // 2
    t = np.arange(n_lat)
    row = (t // GRID_W).astype(np.float64)
    col = (t % GRID_W).astype(np.float64)
    inv = 1.0 / (ROPE_THETA ** (np.arange(nf, dtype=np.float64) / nf))
    ang_r = row[:, None] * inv[None, :]
    ang_c = col[:, None] * inv[None, :]
    cr, sr, cc, sc = np.cos(ang_r), np.sin(ang_r), np.cos(ang_c), np.sin(ang_c)
    zero = np.zeros_like(sr)
    cos = np.concatenate([cr, cr, cc, cc], axis=-1)
    sin_a = np.concatenate([-sr, zero, -sc, zero], axis=-1)
    sin_b = np.concatenate([zero, sr, zero, sc], axis=-1)
    return tuple(jnp.asarray(a, F32) for a in (cos, sin_a, sin_b))


def _trunk_layer(x, *, l, tm, row_fn, is_ctx, mods, g_norm, w_ffn1_gu, w_ffn1_down, w_in, g_q, g_k,
                 w_branch, w_o, w_ffn2_gu, w_ffn2_down, attend, rope_tabs, g_final):
    x = _ffn(x, mods, g_norm, w_ffn1_gu, w_ffn1_down, l=l, norm_idx=0, k_shift=0, k_scale=1,
             k_gate=2, tm=tm, row_fn=row_fn)
    res = _inproj(x, mods, g_norm, g_q, g_k, w_in, rope_tabs, l=l, tm=tm, row_fn=row_fn,
                  is_ctx=is_ctx)
    zq, h2 = res[0], res[1]
    outs = attend(zq)
    mm = _merge(h2, outs, w_in, w_branch, l=l, tm=tm)
    x = _outproj(mm, w_o, x, mods, l=l, tm=tm, row_fn=row_fn)
    x = _ffn(x, mods, g_norm, w_ffn2_gu, w_ffn2_down, l=l, norm_idx=2, k_shift=6, k_scale=7,
             k_gate=8, tm=tm, row_fn=row_fn, g_final=g_final)
    return x, (res[2] if is_ctx else None)


def kernel(x_prompt, x_sample, cache_k_gqa, cache_v_gqa, cache_k_nat, cache_v_nat, cache_k_win, cache_v_win, c, c_ctx, w_mod, b_mod, g_norm, w_ffn1_gu, w_ffn1_down, w_in, g_q, g_k, rpb, sink, w_branch, w_o, w_ffn2_gu, w_ffn2_down, g_final):
    batch, seq, d = x_prompt.shape
    n_b, n_lat, _ = x_sample.shape
    depth = w_mod.shape[0]
    past = cache_k_gqa.shape[2]
    rows = n_lat // GRID_W
    assert n_lat % GRID_W == 0 and rows % (2 * NAT_QROWS) == 0 and rows >= NAT_KROWS + NAT_QROWS
    assert n_lat % Q_BLOCK == 0 and n_lat >= 3 * Q_BLOCK and n_b + 1 <= 8

    m_ctx = batch * seq
    tm_ctx = _divisor_tile(m_ctx, 1024, 16)
    tm_lat = _divisor_tile(n_lat, 1024, 16)
    per_lat = n_lat // tm_lat

    cond8 = jnp.zeros((8, d), F32).at[0].set(c_ctx).at[1:1 + n_b].set(c)
    mods = _modulation(cond8, w_mod, b_mod)

    sink2 = sink * LOG2E
    rope_tabs = _rope_tables(n_lat)
    nat_tabs = _nat_row_tables(rpb)
    ck_gqa = cache_k_gqa.reshape(n_b, depth, past, KV_GQA * HEAD_DIM)
    cv_gqa = cache_v_gqa.reshape(n_b, depth, past, KV_GQA * HEAD_DIM)
    ck_nat = cache_k_nat.reshape(n_b, depth, past, N_HEADS * HEAD_DIM)
    cv_nat = cache_v_nat.reshape(n_b, depth, past, N_HEADS * HEAD_DIM)
    ck_win = cache_k_win.reshape(n_b, depth, past, KV_WIN * HEAD_DIM)
    cv_win = cache_v_win.reshape(n_b, depth, past, KV_WIN * HEAD_DIM)

    weights = dict(g_norm=g_norm, w_ffn1_gu=w_ffn1_gu, w_ffn1_down=w_ffn1_down, w_in=w_in,
                   g_q=g_q, g_k=g_k, w_branch=w_branch, w_o=w_o, w_ffn2_gu=w_ffn2_gu,
                   w_ffn2_down=w_ffn2_down, mods=mods)

    y_p = x_prompt.reshape(m_ctx, d)
    y_s = x_sample.reshape(n_b * n_lat, d)
    kvn = []
    for l in range(depth):
        def attend_ctx(zq, l=l):
            o = _ctx_attention(zq, sink2, l=l, batch=batch, seq=seq)
            return [(o, 0), (o, 1), (o, 2)]

        gf = g_final if l == depth - 1 else None
        y_p, kvn_l = _trunk_layer(y_p, l=l, tm=tm_ctx, row_fn=lambda i: 0, is_ctx=True,
                                  attend=attend_ctx, rope_tabs=None, g_final=gf, **weights)
        kvn.append(kvn_l)

        def attend_lat(zq, l=l):
            oa = _lat_gqa_attention(zq, ck_gqa, cv_gqa, l=l, n_b=n_b, n_lat=n_lat)
            ob = _nat_attention(zq, ck_nat, cv_nat, nat_tabs, l=l, n_b=n_b, n_lat=n_lat)
            oc = _win_attention(zq, ck_win, cv_win, sink2, l=l, n_b=n_b, n_lat=n_lat)
            return [(oa, 0), (ob, 0), (oc, 0)]

        y_s, _ = _trunk_layer(y_s, l=l, tm=tm_lat, row_fn=lambda i: 1 + i // per_lat, is_ctx=False,
                              attend=attend_lat, rope_tabs=rope_tabs, g_final=gf, **weights)

    y_prompt = y_p.reshape(batch, seq, d)
    y_sample = y_s.reshape(n_b, n_lat, d)

    new_kv = _pack_context_kv(kvn, batch=batch, seq=seq)
    return (y_prompt, y_sample) + tuple(
        a.reshape(batch, depth, seq, heads, HEAD_DIM) for a, (_, heads) in zip(new_kv, KVN_PARTS))
```
